```python
import numpy as np
import jax
import jax.numpy as jnp
from jax import lax


D_MODEL = 2048
BATCH = 4
SEQ = 2048
DEPTH = 1

GLA_HEADS = 4
GLA_DK = D_MODEL // 2
GLA_DV = D_MODEL
GLA_HK = GLA_DK // GLA_HEADS
GLA_HV = GLA_DV // GLA_HEADS
GLA_GATE_RANK = 16
GLA_TAU = 16.0
GLA_CHUNK = 64

NSA_HEADS = 16
NSA_KV_HEADS = 4
NSA_GROUP = NSA_HEADS // NSA_KV_HEADS
NSA_HD = D_MODEL // NSA_HEADS
CMP_STRIDE = 16
CMP_LEN = 2 * CMP_STRIDE
CMP_HIDDEN = 256
SLC_BLOCK = 64
SLC_TOPN = 16
WINDOW = 512
NSA_Q_BLOCK = 32

D_FF = ((8 * D_MODEL // 3 + 255) // 256) * 256

IN_SPLIT_SIZES = (GLA_DK, GLA_DK, GLA_DV, GLA_DV, GLA_GATE_RANK,
                  NSA_HEADS * NSA_HD, 6 * NSA_KV_HEADS * NSA_HD, 3 * NSA_HEADS,
                  2 * D_MODEL)
IN_WIDTH = sum(IN_SPLIT_SIZES)

NEG_INF = -1e30
RMS_EPS = 1e-6

kernel_name = 'hybrid_gla_nsa_block'


def rms_norm(x, g):
    xf = x.astype(jnp.float32)
    y = xf * lax.rsqrt(jnp.mean(xf * xf, axis=-1, keepdims=True) + RMS_EPS)
    return (y * g.astype(jnp.float32)).astype(x.dtype)


def masked_softmax(s, valid):
    s = jnp.where(valid, s.astype(jnp.float32), NEG_INF)
    return jnp.where(valid, jax.nn.softmax(s, axis=-1), 0.0)


def gla_mixer(q, k, v, r, gate_lr, w_gate, b_gate, norm_g):
    B, S, _ = q.shape
    C = GLA_CHUNK
    n = S // C
    H = GLA_HEADS
    f32 = jnp.float32
    log_a = jax.nn.log_sigmoid((gate_lr @ w_gate + b_gate).astype(f32)) / GLA_TAU

    def heads(t, d):
        return t.astype(f32).reshape(B, n, C, H, d).transpose(0, 3, 1, 2, 4)

    qh = heads(q, GLA_HK) * (GLA_HK ** -0.5)
    kh = heads(k, GLA_HK)
    vh = heads(v, GLA_HV)
    b = jnp.cumsum(heads(log_a, GLA_HK), axis=3)
    b_last = b[:, :, :, -1:]
    q_dec = qh * jnp.exp(b)
    k_dec = kh * jnp.exp(-b)
    k_end = kh * jnp.exp(b_last - b)
    causal = jnp.tril(jnp.ones((C, C), dtype=bool))
    attn = jnp.where(causal, jnp.einsum('bhncd,bhnsd->bhncs', q_dec, k_dec), 0.0)
    o_intra = jnp.einsum('bhncs,bhnse->bhnce', attn, vh)

    def step(state, inp):
        q_c, k_c, v_c, decay = inp
        o = jnp.einsum('bhcd,bhde->bhce', q_c, state)
        state = decay[..., None] * state + jnp.einsum('bhcd,bhce->bhde', k_c, v_c)
        return state, o

    xs = (jnp.moveaxis(q_dec, 2, 0), jnp.moveaxis(k_end, 2, 0), jnp.moveaxis(vh, 2, 0),
          jnp.moveaxis(jnp.exp(b_last[:, :, :, 0]), 2, 0))
    state0 = jnp.zeros((B, H, GLA_HK, GLA_HV), f32)
    _, o_inter = lax.scan(step, state0, xs)
    o = o_intra + jnp.moveaxis(o_inter, 0, 2)
    o = o.transpose(0, 2, 3, 1, 4).reshape(B, S, H, GLA_HV)
    o = rms_norm(o, norm_g) * jax.nn.silu(r.astype(f32).reshape(B, S, H, GLA_HV))
    return o.reshape(B, S, GLA_DV).astype(q.dtype)


def compress_blocks(t, pe, w1, w2):
    b, hk, s, hd = t.shape
    chunks = t.reshape(b, hk, s // CMP_STRIDE, CMP_STRIDE, hd)
    blocks = jnp.concatenate([chunks[:, :, :-1], chunks[:, :, 1:]], axis=3) + pe
    flat = blocks.reshape(b, hk, blocks.shape[2], CMP_LEN * hd)
    return jax.nn.silu(flat @ w1) @ w2


def cmp_to_slc_weights(n_cmp, n_slc):
    c0 = np.arange(n_cmp)[:, None] * CMP_STRIDE
    s0 = np.arange(n_slc)[None, :] * SLC_BLOCK
    ov = np.clip(np.minimum(c0 + CMP_LEN, s0 + SLC_BLOCK) - np.maximum(c0, s0), 0, None)
    return jnp.asarray(ov / CMP_LEN, dtype=jnp.float32)


def nsa_mixer(q, kv, gate_logits, pe_k, w_ck1, w_ck2, pe_v, w_cv1, w_cv2):
    B, S, _ = q.shape
    Hk, G, hd, Q = NSA_KV_HEADS, NSA_GROUP, NSA_HD, NSA_Q_BLOCK
    qh = q.reshape(B, S, Hk, G, hd).transpose(0, 2, 3, 1, 4) * (hd ** -0.5)
    kvh = kv.reshape(B, S, 6, Hk, hd).transpose(2, 0, 3, 1, 4)
    k_c_raw, v_c_raw, k_s, v_s, k_w, v_w = (kvh[i] for i in range(6))
    k_cmp = compress_blocks(k_c_raw, pe_k, w_ck1, w_ck2)
    v_cmp = compress_blocks(v_c_raw, pe_v, w_cv1, w_cv2)
    n_cmp = k_cmp.shape[2]
    n_slc = S // SLC_BLOCK
    top_n = min(SLC_TOPN, n_slc)
    m_cs = cmp_to_slc_weights(n_cmp, n_slc)
    cmp_end = jnp.arange(n_cmp) * CMP_STRIDE + CMP_LEN - 1
    blk = jnp.arange(n_slc)
    k_sb = k_s.reshape(B, Hk, n_slc, SLC_BLOCK, hd)
    v_sb = v_s.reshape(B, Hk, n_slc, SLC_BLOCK, hd)
    pad = ((0, 0), (0, 0), (WINDOW, 0), (0, 0))
    k_wp = jnp.pad(k_w, pad)
    v_wp = jnp.pad(v_w, pad)
    gates = jax.nn.sigmoid(gate_logits.astype(jnp.float32)).reshape(B, S, 3, Hk, G).transpose(2, 0, 3, 4, 1)
    gather = jax.vmap(jax.vmap(lambda blocks, idx: blocks[idx]))

    def block(i):
        start = i * Q
        t = start + jnp.arange(Q)
        qb = lax.dynamic_slice_in_dim(qh, start, Q, axis=3)
        s_c = jnp.einsum('bhgqd,bhcd->bhgqc', qb, k_cmp)
        p_c = masked_softmax(s_c, cmp_end[None, :] <= t[:, None])
        o_c = jnp.einsum('bhgqc,bhcd->bhgqd', p_c, v_cmp)
        imp = jnp.einsum('bhgqc,cn->bhqn', p_c, m_cs)
        cur = t[:, None] // SLC_BLOCK
        forced = (blk[None, :] == 0) | (blk[None, :] == cur) | (blk[None, :] == cur - 1)
        imp = jnp.where(forced, jnp.inf, imp)
        imp = jnp.where(blk[None, :] * SLC_BLOCK <= t[:, None], imp, -jnp.inf)
        _, idx = lax.top_k(imp, top_n)
        k_sel = gather(k_sb, idx)
        v_sel = gather(v_sb, idx)
        pos = idx[..., None] * SLC_BLOCK + jnp.arange(SLC_BLOCK)
        valid_sel = (pos <= t[:, None, None])[:, :, None].reshape(B, Hk, 1, Q, top_n * SLC_BLOCK)
        s_s = jnp.einsum('bhgqd,bhqnkd->bhgqnk', qb, k_sel).reshape(B, Hk, G, Q, top_n * SLC_BLOCK)
        p_s = masked_softmax(s_s, valid_sel)
        o_s = jnp.einsum('bhgqm,bhqmd->bhgqd', p_s, v_sel.reshape(B, Hk, Q, top_n * SLC_BLOCK, hd))
        k_win = lax.dynamic_slice_in_dim(k_wp, start, Q + WINDOW, axis=2)
        v_win = lax.dynamic_slice_in_dim(v_wp, start, Q + WINDOW, axis=2)
        kpos = start - WINDOW + jnp.arange(Q + WINDOW)
        valid_w = (kpos[None, :] <= t[:, None]) & (kpos[None, :] > t[:, None] - WINDOW) & (kpos[None, :] >= 0)
        s_w = jnp.einsum('bhgqd,bhkd->bhgqk', qb, k_win)
        p_w = masked_softmax(s_w, valid_w)
        o_w = jnp.einsum('bhgqk,bhkd->bhgqd', p_w, v_win)
        gb = lax.dynamic_slice_in_dim(gates, start, Q, axis=4)[..., None]
        return gb[0] * o_c + gb[1] * o_s + gb[2] * o_w

    out = lax.map(block, jnp.arange(S // Q))
    out = out.transpose(1, 0, 4, 2, 3, 5).reshape(B, S, NSA_HEADS * hd)
    return out.astype(q.dtype)


def setup_inputs(seed: int = 0) -> dict:
    key = jax.random.key(seed)
    ks = jax.random.split(key, 24)
    L = DEPTH
    f32 = jnp.float32

    def nrm(k, shape, fan_in):
        return jax.random.normal(k, shape, f32) * (fan_in ** -0.5)

    def gain(k, shape):
        return 1.0 + 0.02 * jax.random.normal(k, shape, f32)

    cin = CMP_LEN * NSA_HD
    return {
        'x': jax.random.normal(ks[0], (BATCH, SEQ, D_MODEL), f32),
        'g_mix': gain(ks[1], (L, D_MODEL)),
        'w_in': nrm(ks[2], (L, D_MODEL, IN_WIDTH), D_MODEL),
        'w_gla_gate': nrm(ks[3], (L, GLA_GATE_RANK, GLA_DK), GLA_GATE_RANK),
        'b_gla_gate': 0.1 * jax.random.normal(ks[4], (L, GLA_DK), f32),
        'g_gla_out': gain(ks[5], (L, GLA_HV)),
        'pe_cmp_k': 0.1 * jax.random.normal(ks[6], (L, CMP_LEN, NSA_HD), f32),
        'w_cmp_k1': nrm(ks[7], (L, cin, CMP_HIDDEN), cin),
        'w_cmp_k2': nrm(ks[8], (L, CMP_HIDDEN, NSA_HD), CMP_HIDDEN),
        'pe_cmp_v': 0.1 * jax.random.normal(ks[9], (L, CMP_LEN, NSA_HD), f32),
        'w_cmp_v1': nrm(ks[10], (L, cin, CMP_HIDDEN), cin),
        'w_cmp_v2': nrm(ks[11], (L, CMP_HIDDEN, NSA_HD), CMP_HIDDEN),
        'w_proj_gla': nrm(ks[12], (L, GLA_DV, D_MODEL), GLA_DV),
        'w_proj_nsa': nrm(ks[13], (L, NSA_HEADS * NSA_HD, D_MODEL), NSA_HEADS * NSA_HD),
        'w_out': nrm(ks[14], (L, D_MODEL, D_MODEL), D_MODEL),
        'g_ffn': gain(ks[15], (L, D_MODEL)),
        'w_ffn_gate': nrm(ks[16], (L, D_MODEL, D_FF), D_MODEL),
        'w_ffn_up': nrm(ks[17], (L, D_MODEL, D_FF), D_MODEL),
        'w_ffn_down': nrm(ks[18], (L, D_FF, D_MODEL), D_FF),
        'g_final': gain(ks[19], (D_MODEL,)),
    }


def reference(x, g_mix, w_in, w_gla_gate, b_gla_gate, g_gla_out, pe_cmp_k, w_cmp_k1, w_cmp_k2,
              pe_cmp_v, w_cmp_v1, w_cmp_v2, w_proj_gla, w_proj_nsa, w_out, g_ffn,
              w_ffn_gate, w_ffn_up, w_ffn_down, g_final):
    split_points = [int(v) for v in np.cumsum(IN_SPLIT_SIZES)[:-1]]
    for l in range(DEPTH):
        h = rms_norm(x, g_mix[l])
        z = h @ w_in[l]
        gq, gk, gv, gr, glr, nq, nkv, ngate, mgate = jnp.split(z, split_points, axis=-1)
        o_gla = gla_mixer(gq, gk, gv, gr, glr, w_gla_gate[l], b_gla_gate[l], g_gla_out[l])
        o_nsa = nsa_mixer(nq, nkv, ngate, pe_cmp_k[l], w_cmp_k1[l], w_cmp_k2[l],
                          pe_cmp_v[l], w_cmp_v1[l], w_cmp_v2[l])
        gate_a, gate_b = jnp.split(jax.nn.sigmoid(mgate), 2, axis=-1)
        merged = gate_a * (o_gla @ w_proj_gla[l]) + gate_b * (o_nsa @ w_proj_nsa[l])
        x = x + merged @ w_out[l]
        h = rms_norm(x, g_ffn[l])
        x = x + (jax.nn.silu(h @ w_ffn_gate[l]) * (h @ w_ffn_up[l])) @ w_ffn_down[l]
    return rms_norm(x, g_final)
```

```python
import functools

import numpy as np
import jax
import jax.numpy as jnp
from jax import lax
from jax.experimental import pallas as pl
from jax.experimental.pallas import tpu as pltpu

F32 = jnp.float32
BF16 = jnp.bfloat16

D_MODEL = 2048
GLA_HEADS = 4
GLA_DK = D_MODEL // 2
GLA_DV = D_MODEL
GLA_HK = GLA_DK // GLA_HEADS
GLA_HV = GLA_DV // GLA_HEADS
GLA_GATE_RANK = 16
GLA_TAU = 16.0
GLA_CHUNK = 64

NSA_HEADS = 16
NSA_KV_HEADS = 4
NSA_GROUP = NSA_HEADS // NSA_KV_HEADS
NSA_HD = D_MODEL // NSA_HEADS
CMP_STRIDE = 16
CMP_LEN = 2 * CMP_STRIDE
CMP_HIDDEN = 256
SLC_BLOCK = 64
SLC_TOPN = 16
WINDOW = 512
D_FF = ((8 * D_MODEL // 3 + 255) // 256) * 256

IN_SPLIT_SIZES = (GLA_DK, GLA_DK, GLA_DV, GLA_DV, GLA_GATE_RANK,
                  NSA_HEADS * NSA_HD, 6 * NSA_KV_HEADS * NSA_HD, 3 * NSA_HEADS,
                  2 * D_MODEL)

NEG_INF = -1e30
RMS_EPS = 1e-6

Z_GQ = 0
Z_GK = Z_GQ + GLA_DK
Z_GV = Z_GK + GLA_DK
Z_GR = Z_GV + GLA_DV
Z_NQ = Z_GR + GLA_DV
Z_NKV = Z_NQ + NSA_HEADS * NSA_HD
Z_MG = Z_NKV + 6 * NSA_KV_HEADS * NSA_HD
Z_SMALL = Z_MG + 2 * D_MODEL
LANES = 128
Z_WIDTH = Z_SMALL + LANES
SMALL_GLR = 0
SMALL_NGATE = GLA_GATE_RANK

VMEM_LIMIT = 56 * 1024 * 1024


def _dot(a, b):
    return jnp.dot(a, b, preferred_element_type=F32)


def _dot_nt(a, b):
    return lax.dot_general(a, b, (((1,), (1,)), ((), ())), preferred_element_type=F32)


def _sigmoid(x):
    return 1.0 / (1.0 + jnp.exp(-x))


def _split3(x):
    hi = x.astype(BF16)
    r1 = x - hi.astype(F32)
    mid = r1.astype(BF16)
    lo = (r1 - mid.astype(F32)).astype(BF16)
    return hi, mid, lo


def _params(sem):
    return pltpu.CompilerParams(dimension_semantics=sem, vmem_limit_bytes=VMEM_LIMIT)


def _in_proj_kernel(x_ref, g_ref, w_ref, z_ref, h_scr):
    @pl.when(pl.program_id(1) == 0)
    def _():
        xf = x_ref[...]
        ms = jnp.mean(xf * xf, axis=-1, keepdims=True)
        h_scr[...] = (xf * lax.rsqrt(ms + RMS_EPS) * g_ref[...]).astype(BF16)

    z_ref[...] = _dot(h_scr[...], w_ref[...]).astype(z_ref.dtype)


def _in_proj(x2, g, w_pack, tm=1024, tn=1408):
    m = x2.shape[0]
    return pl.pallas_call(
        _in_proj_kernel,
        grid=(m // tm, Z_WIDTH // tn),
        in_specs=[
            pl.BlockSpec((tm, D_MODEL), lambda i, j: (i, 0)),
            pl.BlockSpec((1, D_MODEL), lambda i, j: (0, 0)),
            pl.BlockSpec((D_MODEL, tn), lambda i, j: (0, j)),
        ],
        out_specs=pl.BlockSpec((tm, tn), lambda i, j: (i, j)),
        out_shape=jax.ShapeDtypeStruct((m, Z_WIDTH), BF16),
        scratch_shapes=[pltpu.VMEM((tm, D_MODEL), BF16)],
        compiler_params=_params(("parallel", "arbitrary")),
        name="in_proj",
    )(x2, g, w_pack)


def _gla_kernel(q_ref, k_ref, v_ref, r_ref, sm_ref, wg_ref, bg_ref, gn_ref, o_ref,
                state_ref, *, n_chunks):
    @pl.when(pl.program_id(2) == 0)
    def _():
        state_ref[...] = jnp.zeros_like(state_ref)

    C = GLA_CHUNK
    row = lax.broadcasted_iota(jnp.int32, (C, C), 0)
    col = lax.broadcasted_iota(jnp.int32, (C, C), 1)
    causal = row >= col
    tri = jnp.where(causal, 1.0, 0.0).astype(BF16)
    wg = wg_ref[...]
    bg = bg_ref[...]
    gn = gn_ref[...]
    scale = GLA_HK ** -0.5

    def chunk(c, carry):
        rows = pl.ds(pl.multiple_of(c * C, C), C)
        q = q_ref[rows, :].astype(F32) * scale
        k = k_ref[rows, :].astype(F32)
        v = v_ref[rows, :]
        u = _dot(sm_ref[rows, :], wg) + bg
        log_a = (jnp.minimum(u, 0.0) - jnp.log(1.0 + jnp.exp(-jnp.abs(u)))) / GLA_TAU
        hi, mid, lo = _split3(log_a)
        b = _dot(tri, hi) + _dot(tri, mid) + _dot(tri, lo)
        b_last = b[C - 1:C, :]
        q_dec = (q * jnp.exp(b)).astype(BF16)
        k_dec = (k * jnp.exp(-b)).astype(BF16)
        k_end = (k * jnp.exp(b_last - b)).astype(BF16)
        attn = jnp.where(causal, _dot_nt(q_dec, k_dec), 0.0).astype(BF16)
        state = state_ref[...]
        o = _dot(attn, v) + _dot_nt(q_dec, state.astype(BF16))
        kv = lax.dot_general(v, k_end, (((0,), (0,)), ((), ())),
                             preferred_element_type=F32)
        state_ref[...] = state * jnp.exp(b_last) + kv
        ms = jnp.mean(o * o, axis=-1, keepdims=True)
        y = o * lax.rsqrt(ms + RMS_EPS) * gn
        r = r_ref[rows, :].astype(F32)
        o_ref[rows, :] = (y * (r * _sigmoid(r))).astype(o_ref.dtype)
        return carry

    lax.fori_loop(0, n_chunks, chunk, 0)


def _gla(z3, w_gate_pad, b_gate, g_out, ts=512):
    bsz, s, _ = z3.shape
    qb, vb = GLA_HK, GLA_HV
    return pl.pallas_call(
        functools.partial(_gla_kernel, n_chunks=ts // GLA_CHUNK),
        grid=(bsz, GLA_HEADS, s // ts),
        in_specs=[
            pl.BlockSpec((None, ts, qb), lambda b, h, i: (b, i, Z_GQ // qb + h)),
            pl.BlockSpec((None, ts, qb), lambda b, h, i: (b, i, Z_GK // qb + h)),
            pl.BlockSpec((None, ts, vb), lambda b, h, i: (b, i, Z_GV // vb + h)),
            pl.BlockSpec((None, ts, vb), lambda b, h, i: (b, i, Z_GR // vb + h)),
            pl.BlockSpec((None, ts, LANES), lambda b, h, i: (b, i, Z_SMALL // LANES)),
            pl.BlockSpec((LANES, qb), lambda b, h, i: (0, h)),
            pl.BlockSpec((1, qb), lambda b, h, i: (0, h)),
            pl.BlockSpec((1, vb), lambda b, h, i: (0, 0)),
        ],
        out_specs=pl.BlockSpec((None, ts, vb), lambda b, h, i: (b, i, h)),
        out_shape=jax.ShapeDtypeStruct((bsz, s, GLA_DV), BF16),
        scratch_shapes=[pltpu.VMEM((GLA_HV, GLA_HK), F32)],
        compiler_params=_params(("parallel", "parallel", "arbitrary")),
        name="gla",
    )(z3, z3, z3, z3, z3, w_gate_pad, b_gate, g_out)


def _cmp_kernel(t_ref, pe_ref, w1_ref, w2_ref, o_ref):
    half = CMP_STRIDE * NSA_HD
    t = t_ref[...].astype(F32)
    pe = pe_ref[...]
    lo = (t + pe[:, :half]).astype(BF16)
    hi = (t + pe[:, half:]).astype(BF16)
    a = _dot(lo, w1_ref[:half, :])
    bm = _dot(hi, w1_ref[half:, :])
    n = bm.shape[0]
    pre = a + pltpu.roll(bm, n - 1, 0)
    hid = (pre * _sigmoid(pre)).astype(BF16)
    o_ref[...] = _dot(hid, w2_ref[...]).astype(o_ref.dtype)


def _compress(tc, pe, w1, w2):
    _, bsz, hk, nchunk, width = tc.shape
    return pl.pallas_call(
        _cmp_kernel,
        grid=(2, bsz, hk),
        in_specs=[
            pl.BlockSpec((None, None, None, nchunk, width), lambda kv, b, h: (kv, b, h, 0, 0)),
            pl.BlockSpec((None, 1, width * 2), lambda kv, b, h: (kv, 0, 0)),
            pl.BlockSpec((None, width * 2, CMP_HIDDEN), lambda kv, b, h: (kv, 0, 0)),
            pl.BlockSpec((None, CMP_HIDDEN, NSA_HD), lambda kv, b, h: (kv, 0, 0)),
        ],
        out_specs=pl.BlockSpec((None, None, None, nchunk, NSA_HD), lambda kv, b, h: (kv, b, h, 0, 0)),
        out_shape=jax.ShapeDtypeStruct((2, bsz, hk, nchunk, NSA_HD), BF16),
        compiler_params=_params(("parallel", "parallel", "parallel")),
        name="nsa_compress",
    )(tc, pe, w1, w2)


NSA_TQ = 128
NSA_TK = 256


def _nsa_kernel(q_ref, sm_ref, kc_ref, vc_ref, ks_ref, vs_ref, kw_ref, vw_ref, mcs_ref, e_ref,
                o_ref, bias_scr, *, seq):
    G, TQ, TK, HD = NSA_GROUP, NSA_TQ, NSA_TK, NSA_HD
    R = G * TQ
    hk = pl.program_id(1)
    qs = pl.program_id(2) * TQ

    qall = q_ref[...]
    q4 = jnp.concatenate([qall[:, g * HD:(g + 1) * HD] for g in range(G)], axis=0)
    q4 = (q4.astype(F32) * (HD ** -0.5)).astype(BF16)
    t_rows = qs + (lax.broadcasted_iota(jnp.int32, (R, 1), 0) & (TQ - 1))

    nc = kc_ref.shape[0]
    s_c = _dot_nt(q4, kc_ref[...])
    cmp_end = lax.broadcasted_iota(jnp.int32, (R, nc), 1) * CMP_STRIDE + (CMP_LEN - 1)
    valid_c = cmp_end <= t_rows
    s_c = jnp.where(valid_c, s_c, NEG_INF)
    e_c = jnp.where(valid_c, jnp.exp(s_c - jnp.max(s_c, axis=-1, keepdims=True)), 0.0)
    l_c = jnp.sum(e_c, axis=-1, keepdims=True)
    p_c = e_c / jnp.where(l_c > 0.0, l_c, 1.0)
    o_c = _dot(p_c.astype(BF16), vc_ref[...])

    p_sum = p_c[0:TQ]
    for g in range(1, G):
        p_sum = p_sum + p_c[g * TQ:(g + 1) * TQ]
    mcs = mcs_ref[...]
    nb = mcs.shape[0]
    imp = sum(_dot_nt(mcs, part) for part in _split3(p_sum))
    blk = lax.broadcasted_iota(jnp.int32, (nb, TQ), 0)
    tq = qs + lax.broadcasted_iota(jnp.int32, (nb, TQ), 1)
    cur = tq // SLC_BLOCK
    forced = (blk == 0) | (blk == cur) | (blk == cur - 1)
    imp = jnp.where(forced, jnp.inf, imp)
    imp = jnp.where(blk * SLC_BLOCK <= tq, imp, -jnp.inf)
    rank = jnp.zeros((nb, TQ), F32)
    for kb in range(nb):
        rk = imp[kb:kb + 1, :]
        ahead = jnp.where(rk > imp, 1.0, jnp.where(rk == imp, jnp.where(blk > kb, 1.0, 0.0), 0.0))
        rank = rank + ahead
    sel_t = jnp.where(rank < float(min(SLC_TOPN, nb)), 1.0, 0.0)
    sel = sel_t.T.astype(BF16)
    hit = _dot(sel, e_ref[...])
    kpos_all = lax.broadcasted_iota(jnp.int32, (TQ, seq), 1)
    tq_col = qs + lax.broadcasted_iota(jnp.int32, (TQ, 1), 0)
    bias_scr[...] = jnp.where(hit > 0.5, jnp.where(kpos_all <= tq_col, 0.0, NEG_INF), NEG_INF)

    n_kv = (qs + TQ + TK - 1) // TK

    def kv_step(j, carry):
        m, l, acc = carry
        koff = pl.multiple_of(j * TK, TK)
        s = _dot_nt(q4, ks_ref[pl.ds(koff, TK), :])
        bias = bias_scr[:, pl.ds(koff, TK)]
        s = (s.reshape(G, TQ, TK) + bias[None]).reshape(R, TK)
        m_new = jnp.maximum(m, jnp.max(s, axis=-1, keepdims=True))
        alpha = jnp.exp(m - m_new)
        p = jnp.exp(s - m_new)
        l = alpha * l + jnp.sum(p, axis=-1, keepdims=True)
        acc = alpha * acc + _dot(p.astype(BF16), vs_ref[pl.ds(koff, TK), :])
        return m_new, l, acc

    m0 = jnp.full((R, 1), NEG_INF, F32)
    l0 = jnp.zeros((R, 1), F32)
    a0 = jnp.zeros((R, HD), F32)
    _, l_s, acc_s = lax.fori_loop(0, n_kv, kv_step, (m0, l0, a0))
    o_s = acc_s / l_s

    span = WINDOW + TQ
    w0 = pl.multiple_of(jnp.maximum(qs - WINDOW, 0), TQ)
    s_w = _dot_nt(q4, kw_ref[pl.ds(w0, span), :])
    kpos = w0 + lax.broadcasted_iota(jnp.int32, (R, span), 1)
    valid_w = (kpos <= t_rows) & (kpos > t_rows - WINDOW)
    s_w = jnp.where(valid_w, s_w, NEG_INF)
    e_w = jnp.exp(s_w - jnp.max(s_w, axis=-1, keepdims=True))
    e_w = jnp.where(valid_w, e_w, 0.0)
    o_w = _dot(e_w.astype(BF16), vw_ref[pl.ds(w0, span), :]) / jnp.sum(e_w, axis=-1, keepdims=True)

    gates = _sigmoid(sm_ref[...].astype(F32))
    lane = lax.broadcasted_iota(jnp.int32, (TQ, LANES), 1)

    def gate_col(branch):
        cols = []
        for g in range(G):
            target = SMALL_NGATE + branch * NSA_HEADS + hk * G + g
            cols.append(jnp.sum(jnp.where(lane == target, gates, 0.0), axis=-1, keepdims=True))
        return jnp.concatenate(cols, axis=0)

    o = gate_col(0) * o_c + gate_col(1) * o_s + gate_col(2) * o_w
    for g in range(G):
        o_ref[:, g * HD:(g + 1) * HD] = o[g * TQ:(g + 1) * TQ].astype(o_ref.dtype)


def _nsa(z3, cmp, mcs_t, expand):
    bsz, s, _ = z3.shape
    hkv, hd, tq = NSA_KV_HEADS, NSA_HD, NSA_TQ
    qw = NSA_GROUP * hd
    nc = cmp.shape[3]
    kv_col = lambda i: (Z_NKV + i * hkv * hd) // hd

    def kv_spec(i):
        return pl.BlockSpec((None, s, hd), lambda b, h, q: (b, 0, kv_col(i) + h))

    return pl.pallas_call(
        functools.partial(_nsa_kernel, seq=s),
        grid=(bsz, hkv, s // tq),
        in_specs=[
            pl.BlockSpec((None, tq, qw), lambda b, h, q: (b, q, Z_NQ // qw + h)),
            pl.BlockSpec((None, tq, LANES), lambda b, h, q: (b, q, Z_SMALL // LANES)),
            pl.BlockSpec((None, None, None, nc, hd), lambda b, h, q: (0, b, h, 0, 0)),
            pl.BlockSpec((None, None, None, nc, hd), lambda b, h, q: (1, b, h, 0, 0)),
            kv_spec(2), kv_spec(3), kv_spec(4), kv_spec(5),
            pl.BlockSpec(mcs_t.shape, lambda b, h, q: (0, 0)),
            pl.BlockSpec(expand.shape, lambda b, h, q: (0, 0)),
        ],
        out_specs=pl.BlockSpec((None, tq, qw), lambda b, h, q: (b, q, h)),
        out_shape=jax.ShapeDtypeStruct((bsz, s, NSA_HEADS * hd), BF16),
        scratch_shapes=[pltpu.VMEM((tq, s), F32)],
        compiler_params=_params(("parallel", "parallel", "arbitrary")),
        name="nsa_attn",
    )(z3, z3, cmp, cmp, z3, z3, z3, z3, mcs_t, expand)


def _merge_kernel(og_ref, on_ref, wg_ref, wn_ref, ga_ref, gb_ref, o_ref):
    a = _dot(og_ref[...], wg_ref[...])
    b = _dot(on_ref[...], wn_ref[...])
    ga = _sigmoid(ga_ref[...].astype(F32))
    gb = _sigmoid(gb_ref[...].astype(F32))
    o_ref[...] = (ga * a + gb * b).astype(o_ref.dtype)


def _merge(o_gla, o_nsa, w_pg, w_pn, z2, tm=1024, tn=512):
    m = o_gla.shape[0]
    return pl.pallas_call(
        _merge_kernel,
        grid=(m // tm, D_MODEL // tn),
        in_specs=[
            pl.BlockSpec((tm, GLA_DV), lambda i, j: (i, 0)),
            pl.BlockSpec((tm, NSA_HEADS * NSA_HD), lambda i, j: (i, 0)),
            pl.BlockSpec((GLA_DV, tn), lambda i, j: (0, j)),
            pl.BlockSpec((NSA_HEADS * NSA_HD, tn), lambda i, j: (0, j)),
            pl.BlockSpec((tm, tn), lambda i, j: (i, Z_MG // tn + j)),
            pl.BlockSpec((tm, tn), lambda i, j: (i, (Z_MG + D_MODEL) // tn + j)),
        ],
        out_specs=pl.BlockSpec((tm, tn), lambda i, j: (i, j)),
        out_shape=jax.ShapeDtypeStruct((m, D_MODEL), BF16),
        compiler_params=_params(("parallel", "parallel")),
        name="merge_proj",
    )(o_gla, o_nsa, w_pg, w_pn, z2, z2)


def _out_proj_kernel(mg_ref, w_ref, x_ref, g_ref, x1_ref, h_ref):
    x1 = x_ref[...] + _dot(mg_ref[...], w_ref[...])
    x1_ref[...] = x1
    ms = jnp.mean(x1 * x1, axis=-1, keepdims=True)
    h_ref[...] = (x1 * lax.rsqrt(ms + RMS_EPS) * g_ref[...]).astype(h_ref.dtype)


def _out_proj(merged, w_out, x2, g_ffn, tm=512):
    m = merged.shape[0]
    return pl.pallas_call(
        _out_proj_kernel,
        grid=(m // tm,),
        in_specs=[
            pl.BlockSpec((tm, D_MODEL), lambda i: (i, 0)),
            pl.BlockSpec((D_MODEL, D_MODEL), lambda i: (0, 0)),
            pl.BlockSpec((tm, D_MODEL), lambda i: (i, 0)),
            pl.BlockSpec((1, D_MODEL), lambda i: (0, 0)),
        ],
        out_specs=[
            pl.BlockSpec((tm, D_MODEL), lambda i: (i, 0)),
            pl.BlockSpec((tm, D_MODEL), lambda i: (i, 0)),
        ],
        out_shape=[jax.ShapeDtypeStruct((m, D_MODEL), F32),
                   jax.ShapeDtypeStruct((m, D_MODEL), BF16)],
        compiler_params=_params(("parallel",)),
        name="out_proj",
    )(merged, w_out, x2, g_ffn)


def _ffn_up_kernel(h_ref, wg_ref, wu_ref, o_ref):
    h = h_ref[...]
    g = _dot(h, wg_ref[...])
    u = _dot(h, wu_ref[...])
    o_ref[...] = (g * _sigmoid(g) * u).astype(o_ref.dtype)


def _ffn_up(h2, w_gate, w_up, tm=1024, tf=512):
    m = h2.shape[0]
    return pl.pallas_call(
        _ffn_up_kernel,
        grid=(m // tm, D_FF // tf),
        in_specs=[
            pl.BlockSpec((tm, D_MODEL), lambda i, j: (i, 0)),
            pl.BlockSpec((D_MODEL, tf), lambda i, j: (0, j)),
            pl.BlockSpec((D_MODEL, tf), lambda i, j: (0, j)),
        ],
        out_specs=pl.BlockSpec((tm, tf), lambda i, j: (i, j)),
        out_shape=jax.ShapeDtypeStruct((m, D_FF), BF16),
        compiler_params=_params(("parallel", "parallel")),
        name="ffn_up",
    )(h2, w_gate, w_up)


def _ffn_down_kernel(a_ref, w_ref, x1_ref, g_ref, o_ref, acc_ref, *, final_norm):
    k = pl.program_id(1)

    @pl.when(k == 0)
    def _():
        acc_ref[...] = x1_ref[...]

    acc_ref[...] += _dot(a_ref[...], w_ref[...])

    @pl.when(k == pl.num_programs(1) - 1)
    def _():
        y = acc_ref[...]
        if final_norm:
            ms = jnp.mean(y * y, axis=-1, keepdims=True)
            y = y * lax.rsqrt(ms + RMS_EPS) * g_ref[...]
        o_ref[...] = y


def _ffn_down(act, w_down, x1, g_final, final_norm, tm=512, tk=1408):
    m = act.shape[0]
    return pl.pallas_call(
        functools.partial(_ffn_down_kernel, final_norm=final_norm),
        grid=(m // tm, D_FF // tk),
        in_specs=[
            pl.BlockSpec((tm, tk), lambda i, k: (i, k)),
            pl.BlockSpec((tk, D_MODEL), lambda i, k: (k, 0)),
            pl.BlockSpec((tm, D_MODEL), lambda i, k: (i, 0)),
            pl.BlockSpec((1, D_MODEL), lambda i, k: (0, 0)),
        ],
        out_specs=pl.BlockSpec((tm, D_MODEL), lambda i, k: (i, 0)),
        out_shape=jax.ShapeDtypeStruct((m, D_MODEL), F32),
        scratch_shapes=[pltpu.VMEM((tm, D_MODEL), F32)],
        compiler_params=_params(("parallel", "arbitrary")),
        name="ffn_down",
    )(act, w_down, x1, g_final)


def _pack_w_in(w):
    o = np.cumsum((0,) + IN_SPLIT_SIZES)
    main = [w[:, o[0]:o[4]], w[:, o[5]:o[7]], w[:, o[8]:o[9]]]
    small = [w[:, o[4]:o[5]], w[:, o[7]:o[8]]]
    pad = jnp.zeros((w.shape[0], LANES - GLA_GATE_RANK - 3 * NSA_HEADS), w.dtype)
    return jnp.concatenate(main + small + [pad], axis=1).astype(BF16)


def _selection_constants(seq):
    n_cmp = seq // CMP_STRIDE - 1
    n_slc = seq // SLC_BLOCK
    c0 = np.arange(n_cmp)[:, None] * CMP_STRIDE
    s0 = np.arange(n_slc)[None, :] * SLC_BLOCK
    ov = np.clip(np.minimum(c0 + CMP_LEN, s0 + SLC_BLOCK) - np.maximum(c0, s0), 0, None)
    m_cs = np.zeros((n_cmp + 1, n_slc), np.float32)
    m_cs[:n_cmp] = ov / CMP_LEN
    expand = (np.arange(seq)[None, :] // SLC_BLOCK == np.arange(n_slc)[:, None]).astype(np.float32)
    return jnp.asarray(m_cs.T, BF16), jnp.asarray(expand, BF16)


def kernel(x, g_mix, w_in, w_gla_gate, b_gla_gate, g_gla_out, pe_cmp_k, w_cmp_k1, w_cmp_k2,
           pe_cmp_v, w_cmp_v1, w_cmp_v2, w_proj_gla, w_proj_nsa, w_out, g_ffn,
           w_ffn_gate, w_ffn_up, w_ffn_down, g_final):
    bsz, seq, d = x.shape
    m = bsz * seq
    depth = w_in.shape[0]
    hkv, hd = NSA_KV_HEADS, NSA_HD
    mcs_t, expand = _selection_constants(seq)
    xc = x.reshape(m, d)
    for l in range(depth):
        z2 = _in_proj(xc, g_mix[l][None], _pack_w_in(w_in[l]))
        z3 = z2.reshape(bsz, seq, Z_WIDTH)

        w_gate_pad = jnp.zeros((LANES, GLA_DK), F32).at[SMALL_GLR:SMALL_GLR + GLA_GATE_RANK].set(w_gla_gate[l])
        o_gla = _gla(z3, w_gate_pad.astype(BF16), b_gla_gate[l][None], g_gla_out[l][None])

        tc = z3[:, :, Z_NKV:Z_NKV + 2 * hkv * hd]
        tc = tc.reshape(bsz, seq // CMP_STRIDE, CMP_STRIDE, 2, hkv, hd)
        tc = tc.transpose(3, 0, 4, 1, 2, 5).reshape(2, bsz, hkv, seq // CMP_STRIDE, CMP_STRIDE * hd)
        pe = jnp.stack([pe_cmp_k[l], pe_cmp_v[l]]).reshape(2, 1, CMP_LEN * hd)
        w1 = jnp.stack([w_cmp_k1[l], w_cmp_v1[l]]).astype(BF16)
        w2 = jnp.stack([w_cmp_k2[l], w_cmp_v2[l]]).astype(BF16)
        cmp = _compress(tc, pe, w1, w2)

        o_nsa = _nsa(z3, cmp, mcs_t, expand)

        merged = _merge(o_gla.reshape(m, GLA_DV), o_nsa.reshape(m, NSA_HEADS * hd),
                        w_proj_gla[l].astype(BF16), w_proj_nsa[l].astype(BF16), z2)
        x1, h2 = _out_proj(merged, w_out[l].astype(BF16), xc, g_ffn[l][None])
        act = _ffn_up(h2, w_ffn_gate[l].astype(BF16), w_ffn_up[l].astype(BF16))
        xc = _ffn_down(act, w_ffn_down[l].astype(BF16), x1, g_final[None],
                       final_norm=(l == depth - 1))
    return xc.reshape(bsz, seq, d)
```

```python
import functools

import numpy as np
import jax
import jax.numpy as jnp
from jax import lax
from jax.experimental import pallas as pl
from jax.experimental.pallas import tpu as pltpu

F32 = jnp.float32
BF16 = jnp.bfloat16

D_MODEL = 2048
GLA_HEADS = 4
GLA_DK = D_MODEL // 2
GLA_DV = D_MODEL
GLA_HK = GLA_DK // GLA_HEADS
GLA_HV = GLA_DV // GLA_HEADS
GLA_GATE_RANK = 16
GLA_TAU = 16.0
GLA_CHUNK = 64

NSA_HEADS = 16
NSA_KV_HEADS = 4
NSA_GROUP = NSA_HEADS // NSA_KV_HEADS
NSA_HD = D_MODEL // NSA_HEADS
CMP_STRIDE = 16
CMP_LEN = 2 * CMP_STRIDE
CMP_HIDDEN = 256
SLC_BLOCK = 64
SLC_TOPN = 16
WINDOW = 512
D_FF = ((8 * D_MODEL // 3 + 255) // 256) * 256

IN_SPLIT_SIZES = (GLA_DK, GLA_DK, GLA_DV, GLA_DV, GLA_GATE_RANK,
                  NSA_HEADS * NSA_HD, 6 * NSA_KV_HEADS * NSA_HD, 3 * NSA_HEADS,
                  2 * D_MODEL)

NEG_INF = -1e30
RMS_EPS = 1e-6

Z_GQ = 0
Z_GK = Z_GQ + GLA_DK
Z_GV = Z_GK + GLA_DK
Z_GR = Z_GV + GLA_DV
Z_NQ = Z_GR + GLA_DV
Z_NKV = Z_NQ + NSA_HEADS * NSA_HD
Z_MG = Z_NKV + 6 * NSA_KV_HEADS * NSA_HD
Z_SMALL = Z_MG + 2 * D_MODEL
LANES = 128
Z_WIDTH = Z_SMALL + LANES
SMALL_GLR = 0
SMALL_NGATE = GLA_GATE_RANK

VMEM_LIMIT = 56 * 1024 * 1024


def _dot(a, b):
    return jnp.dot(a, b, preferred_element_type=F32)


def _dot_nt(a, b):
    return lax.dot_general(a, b, (((1,), (1,)), ((), ())), preferred_element_type=F32)


def _dot_tn(a, b):
    return lax.dot_general(a, b, (((0,), (0,)), ((), ())), preferred_element_type=F32)


def _sigmoid(x):
    return 1.0 / (1.0 + jnp.exp(-x))


def _split2(x):
    hi = x.astype(BF16)
    lo = (x - hi.astype(F32)).astype(BF16)
    return hi, lo


def _split3(x):
    hi = x.astype(BF16)
    r1 = x - hi.astype(F32)
    mid = r1.astype(BF16)
    lo = (r1 - mid.astype(F32)).astype(BF16)
    return hi, mid, lo


def _params(sem):
    return pltpu.CompilerParams(dimension_semantics=sem, vmem_limit_bytes=VMEM_LIMIT)


def _in_proj_kernel(x_ref, g_ref, w_ref, z_ref, h_scr):
    @pl.when(pl.program_id(1) == 0)
    def _():
        xf = x_ref[...]
        ms = jnp.mean(xf * xf, axis=-1, keepdims=True)
        h_scr[...] = (xf * lax.rsqrt(ms + RMS_EPS) * g_ref[...]).astype(BF16)

    z_ref[...] = _dot(h_scr[...], w_ref[...]).astype(z_ref.dtype)


def _in_proj(x2, g, w_pack, tm=1024, tn=1408):
    m = x2.shape[0]
    return pl.pallas_call(
        _in_proj_kernel,
        grid=(m // tm, Z_WIDTH // tn),
        in_specs=[
            pl.BlockSpec((tm, D_MODEL), lambda i, j: (i, 0)),
            pl.BlockSpec((1, D_MODEL), lambda i, j: (0, 0)),
            pl.BlockSpec((D_MODEL, tn), lambda i, j: (0, j)),
        ],
        out_specs=pl.BlockSpec((tm, tn), lambda i, j: (i, j)),
        out_shape=jax.ShapeDtypeStruct((m, Z_WIDTH), BF16),
        scratch_shapes=[pltpu.VMEM((tm, D_MODEL), BF16)],
        compiler_params=_params(("parallel", "arbitrary")),
        name="in_proj",
    )(x2, g, w_pack)


def _gla_kernel(q_ref, k_ref, v_ref, r_ref, sm_ref, wg_ref, bg_ref, gn_ref, o_ref,
                state_ref, *, n_chunks):
    @pl.when(pl.program_id(2) == 0)
    def _():
        state_ref[...] = jnp.zeros_like(state_ref)

    C = GLA_CHUNK
    ts = n_chunks * C
    row = lax.broadcasted_iota(jnp.int32, (ts, ts), 0)
    col = lax.broadcasted_iota(jnp.int32, (ts, ts), 1)
    tri = jnp.where((row >= col) & ((row // C) == (col // C)), 1.0, 0.0).astype(BF16)
    crow = lax.broadcasted_iota(jnp.int32, (C, C), 0)
    ccol = lax.broadcasted_iota(jnp.int32, (C, C), 1)
    causal = crow >= ccol

    u = _dot(sm_ref[...], wg_ref[...]) + bg_ref[...]
    log_a = (jnp.minimum(u, 0.0) - jnp.log(1.0 + jnp.exp(-jnp.abs(u)))) / GLA_TAU
    hi, lo = _split2(log_a)
    b = _dot(tri, hi) + _dot(tri, lo)
    q = q_ref[...].astype(F32) * (GLA_HK ** -0.5)
    k = k_ref[...].astype(F32)
    q_dec = (q * jnp.exp(b)).astype(BF16)
    k_dec = (k * jnp.exp(-b)).astype(BF16)
    gn = gn_ref[...]

    state = state_ref[...]
    for c in range(n_chunks):
        rows = slice(c * C, (c + 1) * C)
        b_c = b[rows]
        b_last = b_c[C - 1:C, :]
        k_end = (k[rows] * jnp.exp(b_last - b_c)).astype(BF16)
        v_c = v_ref[rows, :]
        attn = jnp.where(causal, _dot_nt(q_dec[rows], k_dec[rows]), 0.0).astype(BF16)
        o = _dot(attn, v_c) + _dot_nt(q_dec[rows], state.astype(BF16))
        state = state * jnp.exp(b_last) + _dot_tn(v_c, k_end)
        ms = jnp.mean(o * o, axis=-1, keepdims=True)
        y = o * lax.rsqrt(ms + RMS_EPS) * gn
        r = r_ref[rows, :].astype(F32)
        o_ref[rows, :] = (y * (r * _sigmoid(r))).astype(o_ref.dtype)
    state_ref[...] = state


def _gla(z3, w_gate_pad, b_gate, g_out, ts=512):
    bsz, s, _ = z3.shape
    qb, vb = GLA_HK, GLA_HV
    return pl.pallas_call(
        functools.partial(_gla_kernel, n_chunks=ts // GLA_CHUNK),
        grid=(bsz, GLA_HEADS, s // ts),
        in_specs=[
            pl.BlockSpec((None, ts, qb), lambda b, h, i: (b, i, Z_GQ // qb + h)),
            pl.BlockSpec((None, ts, qb), lambda b, h, i: (b, i, Z_GK // qb + h)),
            pl.BlockSpec((None, ts, vb), lambda b, h, i: (b, i, Z_GV // vb + h)),
            pl.BlockSpec((None, ts, vb), lambda b, h, i: (b, i, Z_GR // vb + h)),
            pl.BlockSpec((None, ts, LANES), lambda b, h, i: (b, i, Z_SMALL // LANES)),
            pl.BlockSpec((LANES, qb), lambda b, h, i: (0, h)),
            pl.BlockSpec((1, qb), lambda b, h, i: (0, h)),
            pl.BlockSpec((1, vb), lambda b, h, i: (0, 0)),
        ],
        out_specs=pl.BlockSpec((None, ts, vb), lambda b, h, i: (b, i, h)),
        out_shape=jax.ShapeDtypeStruct((bsz, s, GLA_DV), BF16),
        scratch_shapes=[pltpu.VMEM((GLA_HV, GLA_HK), F32)],
        compiler_params=_params(("parallel", "parallel", "arbitrary")),
        name="gla",
    )(z3, z3, z3, z3, z3, w_gate_pad, b_gate, g_out)


def _cmp_kernel(t_ref, pe_ref, w1_ref, w2_ref, o_ref, tf_scr, lo_scr, hi_scr):
    hd = NSA_HD
    half = CMP_STRIDE * hd
    nchunk = lo_scr.shape[0]
    tf_scr[...] = t_ref[...].astype(F32)
    for r in range(CMP_STRIDE):
        t_r = tf_scr[pl.ds(r, nchunk, stride=CMP_STRIDE), :]
        lo_scr[:, r * hd:(r + 1) * hd] = (t_r + pe_ref[r:r + 1, :]).astype(BF16)
        hi_scr[:, r * hd:(r + 1) * hd] = (t_r + pe_ref[CMP_STRIDE + r:CMP_STRIDE + r + 1, :]).astype(BF16)
    a = _dot(lo_scr[...], w1_ref[:half, :].astype(BF16))
    bm = _dot(hi_scr[...], w1_ref[half:, :].astype(BF16))
    pre = a + pltpu.roll(bm, nchunk - 1, 0)
    hid = (pre * _sigmoid(pre)).astype(BF16)
    o_ref[...] = _dot(hid, w2_ref[...].astype(BF16)).astype(o_ref.dtype)


def _compress(z3, pe, w1, w2):
    bsz, s, _ = z3.shape
    hkv, hd = NSA_KV_HEADS, NSA_HD
    nchunk = s // CMP_STRIDE
    width = CMP_STRIDE * hd
    return pl.pallas_call(
        _cmp_kernel,
        grid=(2, bsz, hkv),
        in_specs=[
            pl.BlockSpec((None, s, hd), lambda kv, b, h: (b, 0, Z_NKV // hd + kv * hkv + h)),
            pl.BlockSpec((None, CMP_LEN, hd), lambda kv, b, h: (kv, 0, 0)),
            pl.BlockSpec((None, 2 * width, CMP_HIDDEN), lambda kv, b, h: (kv, 0, 0)),
            pl.BlockSpec((None, CMP_HIDDEN, hd), lambda kv, b, h: (kv, 0, 0)),
        ],
        out_specs=pl.BlockSpec((None, None, None, nchunk, hd), lambda kv, b, h: (kv, b, h, 0, 0)),
        out_shape=jax.ShapeDtypeStruct((2, bsz, hkv, nchunk, hd), BF16),
        scratch_shapes=[pltpu.VMEM((s, hd), F32),
                        pltpu.VMEM((nchunk, width), BF16),
                        pltpu.VMEM((nchunk, width), BF16)],
        compiler_params=_params(("parallel", "parallel", "parallel")),
        name="nsa_compress",
    )(z3, pe, w1, w2)


NSA_TQ = 128
NSA_TK = 256


def _nsa_kernel(q_ref, sm_ref, kc_ref, vc_ref, ks_ref, vs_ref, kw_ref, vw_ref, mcs_ref,
                o_ref, sel_scr, gate_scr):
    G, TQ, TK, HD = NSA_GROUP, NSA_TQ, NSA_TK, NSA_HD
    R = G * TQ
    hk = pl.program_id(1)
    qs = pl.program_id(2) * TQ

    def lanes4(x):
        return jnp.concatenate([x] * G, axis=1)

    qall = q_ref[...]
    q4 = jnp.concatenate([qall[:, g * HD:(g + 1) * HD] for g in range(G)], axis=0)
    q4 = (q4.astype(F32) * (HD ** -0.5)).astype(BF16)

    nc = kc_ref.shape[0]
    tq_c = qs + lax.broadcasted_iota(jnp.int32, (nc, TQ), 1)
    cmp_end = lax.broadcasted_iota(jnp.int32, (nc, TQ), 0) * CMP_STRIDE + (CMP_LEN - 1)
    valid_c = lanes4(jnp.where(cmp_end <= tq_c, 1.0, 0.0)) > 0.5
    s_c = jnp.where(valid_c, _dot_nt(kc_ref[...], q4), NEG_INF)
    e_c = jnp.where(valid_c, jnp.exp(s_c - jnp.max(s_c, axis=0, keepdims=True)), 0.0)
    l_c = jnp.sum(e_c, axis=0, keepdims=True)
    p_c = e_c / jnp.where(l_c > 0.0, l_c, 1.0)
    o_c = _dot_tn(vc_ref[...], p_c.astype(BF16))

    p_sum = p_c[:, 0:TQ]
    for g in range(1, G):
        p_sum = p_sum + p_c[:, g * TQ:(g + 1) * TQ]
    mcs = mcs_ref[...]
    nb = mcs.shape[0]
    imp = sum(_dot(mcs, part) for part in _split3(p_sum))
    blk = lax.broadcasted_iota(jnp.int32, (nb, TQ), 0)
    tq_b = qs + lax.broadcasted_iota(jnp.int32, (nb, TQ), 1)
    cur = tq_b // SLC_BLOCK
    forced = (blk == 0) | (blk == cur) | (blk == cur - 1)
    imp = jnp.where(forced, jnp.inf, imp)
    imp = jnp.where(blk * SLC_BLOCK <= tq_b, imp, -jnp.inf)
    rank = jnp.zeros((nb, TQ), F32)
    for kb in range(nb):
        rk = imp[kb:kb + 1, :]
        ahead = jnp.where(rk > imp, 1.0, jnp.where(rk == imp, jnp.where(blk > kb, 1.0, 0.0), 0.0))
        rank = rank + ahead
    sel_scr[...] = jnp.where(rank < float(min(SLC_TOPN, nb)), 0.0, NEG_INF)

    n_kv = (qs + TQ + TK - 1) // TK
    blocks_per_tile = TK // SLC_BLOCK
    tq_k = qs + lax.broadcasted_iota(jnp.int32, (TK, TQ), 1)
    krow = lax.broadcasted_iota(jnp.int32, (TK, TQ), 0)

    def kv_step(j, carry):
        m, l, acc = carry
        koff = pl.multiple_of(j * TK, TK)
        s = _dot_nt(ks_ref[pl.ds(koff, TK), :], q4)
        bias = jnp.concatenate(
            [jnp.broadcast_to(sel_scr[pl.ds(j * blocks_per_tile + i, 1), :], (SLC_BLOCK, TQ))
             for i in range(blocks_per_tile)], axis=0)
        bias = jnp.where(koff + krow <= tq_k, bias, NEG_INF)
        s = s + lanes4(bias)
        m_new = jnp.maximum(m, jnp.max(s, axis=0, keepdims=True))
        alpha = jnp.exp(m - m_new)
        p = jnp.exp(s - m_new)
        l = alpha * l + jnp.sum(p, axis=0, keepdims=True)
        acc = alpha * acc + _dot_tn(vs_ref[pl.ds(koff, TK), :], p.astype(BF16))
        return m_new, l, acc

    m0 = jnp.full((1, R), NEG_INF, F32)
    l0 = jnp.zeros((1, R), F32)
    a0 = jnp.zeros((HD, R), F32)
    _, l_s, acc_s = lax.fori_loop(0, n_kv, kv_step, (m0, l0, a0))
    o_s = acc_s / l_s

    span = WINDOW + TQ
    w0 = pl.multiple_of(jnp.maximum(qs - WINDOW, 0), TQ)
    kpos = w0 + lax.broadcasted_iota(jnp.int32, (span, TQ), 0)
    tq_w = qs + lax.broadcasted_iota(jnp.int32, (span, TQ), 1)
    bias_w = jnp.where(kpos <= tq_w, jnp.where(kpos > tq_w - WINDOW, 0.0, NEG_INF), NEG_INF)
    s_w = _dot_nt(kw_ref[pl.ds(w0, span), :], q4) + lanes4(bias_w)
    e_w = jnp.exp(s_w - jnp.max(s_w, axis=0, keepdims=True))
    o_w = _dot_tn(vw_ref[pl.ds(w0, span), :], e_w.astype(BF16)) / jnp.sum(e_w, axis=0, keepdims=True)

    gate_scr[...] = _sigmoid(sm_ref[...].astype(F32)).T

    def gate_row(branch):
        base = SMALL_NGATE + branch * NSA_HEADS + hk * G
        return jnp.concatenate([gate_scr[pl.ds(base + g, 1), :] for g in range(G)], axis=1)

    o = gate_row(0) * o_c + gate_row(1) * o_s + gate_row(2) * o_w
    for g in range(G):
        o_ref[:, g * HD:(g + 1) * HD] = o[:, g * TQ:(g + 1) * TQ].T.astype(o_ref.dtype)


def _nsa(z3, cmp, mcs_t):
    bsz, s, _ = z3.shape
    hkv, hd, tq = NSA_KV_HEADS, NSA_HD, NSA_TQ
    qw = NSA_GROUP * hd
    nc = cmp.shape[3]
    nb = mcs_t.shape[0]
    kv_col = lambda i: (Z_NKV + i * hkv * hd) // hd

    def kv_spec(i):
        return pl.BlockSpec((None, s, hd), lambda b, h, q: (b, 0, kv_col(i) + h))

    return pl.pallas_call(
        _nsa_kernel,
        grid=(bsz, hkv, s // tq),
        in_specs=[
            pl.BlockSpec((None, tq, qw), lambda b, h, q: (b, q, Z_NQ // qw + h)),
            pl.BlockSpec((None, tq, LANES), lambda b, h, q: (b, q, Z_SMALL // LANES)),
            pl.BlockSpec((None, None, None, nc, hd), lambda b, h, q: (0, b, h, 0, 0)),
            pl.BlockSpec((None, None, None, nc, hd), lambda b, h, q: (1, b, h, 0, 0)),
            kv_spec(2), kv_spec(3), kv_spec(4), kv_spec(5),
            pl.BlockSpec(mcs_t.shape, lambda b, h, q: (0, 0)),
        ],
        out_specs=pl.BlockSpec((None, tq, qw), lambda b, h, q: (b, q, h)),
        out_shape=jax.ShapeDtypeStruct((bsz, s, NSA_HEADS * hd), BF16),
        scratch_shapes=[pltpu.VMEM((nb, tq), F32), pltpu.VMEM((LANES, tq), F32)],
        compiler_params=_params(("parallel", "parallel", "arbitrary")),
        name="nsa_attn",
    )(z3, z3, cmp, cmp, z3, z3, z3, z3, mcs_t)


def _merge_kernel(og_ref, on_ref, wg_ref, wn_ref, ga_ref, gb_ref, o_ref, wg_scr, wn_scr):
    @pl.when(pl.program_id(1) == 0)
    def _():
        wg_scr[...] = wg_ref[...].astype(BF16)
        wn_scr[...] = wn_ref[...].astype(BF16)

    a = _dot(og_ref[...], wg_scr[...])
    b = _dot(on_ref[...], wn_scr[...])
    ga = _sigmoid(ga_ref[...].astype(F32))
    gb = _sigmoid(gb_ref[...].astype(F32))
    o_ref[...] = (ga * a + gb * b).astype(o_ref.dtype)


def _merge(o_gla, o_nsa, w_pg, w_pn, z2, tm=1024, tn=512):
    m = o_gla.shape[0]
    return pl.pallas_call(
        _merge_kernel,
        grid=(D_MODEL // tn, m // tm),
        in_specs=[
            pl.BlockSpec((tm, GLA_DV), lambda j, i: (i, 0)),
            pl.BlockSpec((tm, NSA_HEADS * NSA_HD), lambda j, i: (i, 0)),
            pl.BlockSpec((GLA_DV, tn), lambda j, i: (0, j)),
            pl.BlockSpec((NSA_HEADS * NSA_HD, tn), lambda j, i: (0, j)),
            pl.BlockSpec((tm, tn), lambda j, i: (i, Z_MG // tn + j)),
            pl.BlockSpec((tm, tn), lambda j, i: (i, (Z_MG + D_MODEL) // tn + j)),
        ],
        out_specs=pl.BlockSpec((tm, tn), lambda j, i: (i, j)),
        out_shape=jax.ShapeDtypeStruct((m, D_MODEL), BF16),
        scratch_shapes=[pltpu.VMEM((GLA_DV, tn), BF16), pltpu.VMEM((NSA_HEADS * NSA_HD, tn), BF16)],
        compiler_params=_params(("parallel", "arbitrary")),
        name="merge_proj",
    )(o_gla, o_nsa, w_pg, w_pn, z2, z2)


def _out_proj_kernel(mg_ref, w_ref, x_ref, g_ref, x1_ref, h_ref):
    x1 = x_ref[...] + _dot(mg_ref[...], w_ref[...])
    x1_ref[...] = x1
    ms = jnp.mean(x1 * x1, axis=-1, keepdims=True)
    h_ref[...] = (x1 * lax.rsqrt(ms + RMS_EPS) * g_ref[...]).astype(h_ref.dtype)


def _out_proj(merged, w_out, x2, g_ffn, tm=512):
    m = merged.shape[0]
    return pl.pallas_call(
        _out_proj_kernel,
        grid=(m // tm,),
        in_specs=[
            pl.BlockSpec((tm, D_MODEL), lambda i: (i, 0)),
            pl.BlockSpec((D_MODEL, D_MODEL), lambda i: (0, 0)),
            pl.BlockSpec((tm, D_MODEL), lambda i: (i, 0)),
            pl.BlockSpec((1, D_MODEL), lambda i: (0, 0)),
        ],
        out_specs=[
            pl.BlockSpec((tm, D_MODEL), lambda i: (i, 0)),
            pl.BlockSpec((tm, D_MODEL), lambda i: (i, 0)),
        ],
        out_shape=[jax.ShapeDtypeStruct((m, D_MODEL), F32),
                   jax.ShapeDtypeStruct((m, D_MODEL), BF16)],
        compiler_params=_params(("parallel",)),
        name="out_proj",
    )(merged, w_out, x2, g_ffn)


def _ffn_up_kernel(h_ref, wg_ref, wu_ref, o_ref, wg_scr, wu_scr):
    @pl.when(pl.program_id(1) == 0)
    def _():
        wg_scr[...] = wg_ref[...].astype(BF16)
        wu_scr[...] = wu_ref[...].astype(BF16)

    h = h_ref[...]
    g = _dot(h, wg_scr[...])
    u = _dot(h, wu_scr[...])
    o_ref[...] = (g * _sigmoid(g) * u).astype(o_ref.dtype)


def _ffn_up(h2, w_gate, w_up, tm=1024, tf=512):
    m = h2.shape[0]
    return pl.pallas_call(
        _ffn_up_kernel,
        grid=(D_FF // tf, m // tm),
        in_specs=[
            pl.BlockSpec((tm, D_MODEL), lambda j, i: (i, 0)),
            pl.BlockSpec((D_MODEL, tf), lambda j, i: (0, j)),
            pl.BlockSpec((D_MODEL, tf), lambda j, i: (0, j)),
        ],
        out_specs=pl.BlockSpec((tm, tf), lambda j, i: (i, j)),
        out_shape=jax.ShapeDtypeStruct((m, D_FF), BF16),
        scratch_shapes=[pltpu.VMEM((D_MODEL, tf), BF16), pltpu.VMEM((D_MODEL, tf), BF16)],
        compiler_params=_params(("parallel", "arbitrary")),
        name="ffn_up",
    )(h2, w_gate, w_up)


def _ffn_down_kernel(a_ref, w_ref, x1_ref, g_ref, o_ref, acc_ref, *, final_norm):
    k = pl.program_id(1)

    @pl.when(k == 0)
    def _():
        acc_ref[...] = x1_ref[...]

    acc_ref[...] += _dot(a_ref[...], w_ref[...])

    @pl.when(k == pl.num_programs(1) - 1)
    def _():
        y = acc_ref[...]
        if final_norm:
            ms = jnp.mean(y * y, axis=-1, keepdims=True)
            y = y * lax.rsqrt(ms + RMS_EPS) * g_ref[...]
        o_ref[...] = y


def _ffn_down(act, w_down, x1, g_final, final_norm, tm=512, tk=1408):
    m = act.shape[0]
    return pl.pallas_call(
        functools.partial(_ffn_down_kernel, final_norm=final_norm),
        grid=(m // tm, D_FF // tk),
        in_specs=[
            pl.BlockSpec((tm, tk), lambda i, k: (i, k)),
            pl.BlockSpec((tk, D_MODEL), lambda i, k: (k, 0)),
            pl.BlockSpec((tm, D_MODEL), lambda i, k: (i, 0)),
            pl.BlockSpec((1, D_MODEL), lambda i, k: (0, 0)),
        ],
        out_specs=pl.BlockSpec((tm, D_MODEL), lambda i, k: (i, 0)),
        out_shape=jax.ShapeDtypeStruct((m, D_MODEL), F32),
        scratch_shapes=[pltpu.VMEM((tm, D_MODEL), F32)],
        compiler_params=_params(("parallel", "arbitrary")),
        name="ffn_down",
    )(act, w_down, x1, g_final)


def _pack_w_in(w):
    o = np.cumsum((0,) + IN_SPLIT_SIZES)
    main = [w[:, o[0]:o[4]], w[:, o[5]:o[7]], w[:, o[8]:o[9]]]
    small = [w[:, o[4]:o[5]], w[:, o[7]:o[8]]]
    pad = jnp.zeros((w.shape[0], LANES - GLA_GATE_RANK - 3 * NSA_HEADS), w.dtype)
    return jnp.concatenate(main + small + [pad], axis=1).astype(BF16)


def _cmp_to_slc_weights_t(seq):
    n_cmp = seq // CMP_STRIDE - 1
    n_slc = seq // SLC_BLOCK
    c0 = np.arange(n_cmp)[:, None] * CMP_STRIDE
    s0 = np.arange(n_slc)[None, :] * SLC_BLOCK
    ov = np.clip(np.minimum(c0 + CMP_LEN, s0 + SLC_BLOCK) - np.maximum(c0, s0), 0, None)
    m_cs = np.zeros((n_cmp + 1, n_slc), np.float32)
    m_cs[:n_cmp] = ov / CMP_LEN
    return jnp.asarray(m_cs.T, BF16)


def kernel(x, g_mix, w_in, w_gla_gate, b_gla_gate, g_gla_out, pe_cmp_k, w_cmp_k1, w_cmp_k2,
           pe_cmp_v, w_cmp_v1, w_cmp_v2, w_proj_gla, w_proj_nsa, w_out, g_ffn,
           w_ffn_gate, w_ffn_up, w_ffn_down, g_final):
    bsz, seq, d = x.shape
    m = bsz * seq
    depth = w_in.shape[0]
    mcs_t = _cmp_to_slc_weights_t(seq)
    xc = x.reshape(m, d)
    for l in range(depth):
        z2 = _in_proj(xc, g_mix[l][None], _pack_w_in(w_in[l]))
        z3 = z2.reshape(bsz, seq, Z_WIDTH)

        w_gate_pad = jnp.zeros((LANES, GLA_DK), F32).at[SMALL_GLR:SMALL_GLR + GLA_GATE_RANK].set(w_gla_gate[l])
        o_gla = _gla(z3, w_gate_pad.astype(BF16), b_gla_gate[l][None], g_gla_out[l][None])

        cmp = _compress(z3, jnp.stack([pe_cmp_k[l], pe_cmp_v[l]]),
                        jnp.stack([w_cmp_k1[l], w_cmp_v1[l]]), jnp.stack([w_cmp_k2[l], w_cmp_v2[l]]))
        o_nsa = _nsa(z3, cmp, mcs_t)

        merged = _merge(o_gla.reshape(m, GLA_DV), o_nsa.reshape(m, NSA_HEADS * NSA_HD),
                        w_proj_gla[l], w_proj_nsa[l], z2)
        x1, h2 = _out_proj(merged, w_out[l].astype(BF16), xc, g_ffn[l][None])
        act = _ffn_up(h2, w_ffn_gate[l], w_ffn_up[l])
        xc = _ffn_down(act, w_ffn_down[l].astype(BF16), x1, g_final[None],
                       final_norm=(l == depth - 1))
    return xc.reshape(bsz, seq, d)
```

```python
import functools

import numpy as np
import jax
import jax.numpy as jnp
from jax import lax
from jax.experimental import pallas as pl
from jax.experimental.pallas import tpu as pltpu

F32 = jnp.float32
BF16 = jnp.bfloat16

D_MODEL = 2048
GLA_HEADS = 4
GLA_DK = D_MODEL // 2
GLA_DV = D_MODEL
GLA_HK = GLA_DK // GLA_HEADS
GLA_HV = GLA_DV // GLA_HEADS
GLA_GATE_RANK = 16
GLA_TAU = 16.0
GLA_CHUNK = 64

NSA_HEADS = 16
NSA_KV_HEADS = 4
NSA_GROUP = NSA_HEADS // NSA_KV_HEADS
NSA_HD = D_MODEL // NSA_HEADS
CMP_STRIDE = 16
CMP_LEN = 2 * CMP_STRIDE
CMP_HIDDEN = 256
SLC_BLOCK = 64
SLC_TOPN = 16
WINDOW = 512
D_FF = ((8 * D_MODEL // 3 + 255) // 256) * 256

IN_SPLIT_SIZES = (GLA_DK, GLA_DK, GLA_DV, GLA_DV, GLA_GATE_RANK,
                  NSA_HEADS * NSA_HD, 6 * NSA_KV_HEADS * NSA_HD, 3 * NSA_HEADS,
                  2 * D_MODEL)

NEG_INF = -1e30
RMS_EPS = 1e-6

Z_GQ = 0
Z_GK = Z_GQ + GLA_DK
Z_GV = Z_GK + GLA_DK
Z_GR = Z_GV + GLA_DV
Z_NQ = Z_GR + GLA_DV
Z_NKV = Z_NQ + NSA_HEADS * NSA_HD
Z_MG = Z_NKV + 6 * NSA_KV_HEADS * NSA_HD
Z_SMALL = Z_MG + 2 * D_MODEL
LANES = 128
SMALL_GLR = 0
SMALL_NGATE = GLA_GATE_RANK

VMEM_LIMIT = 56 * 1024 * 1024


def _dot(a, b):
    return jnp.dot(a, b, preferred_element_type=F32)


def _dot_nt(a, b):
    return lax.dot_general(a, b, (((1,), (1,)), ((), ())), preferred_element_type=F32)


def _dot_tn(a, b):
    return lax.dot_general(a, b, (((0,), (0,)), ((), ())), preferred_element_type=F32)


def _sigmoid(x):
    return 1.0 / (1.0 + jnp.exp(-x))


def _split2(x):
    hi = x.astype(BF16)
    lo = (x - hi.astype(F32)).astype(BF16)
    return hi, lo


def _split3(x):
    hi = x.astype(BF16)
    r1 = x - hi.astype(F32)
    mid = r1.astype(BF16)
    lo = (r1 - mid.astype(F32)).astype(BF16)
    return hi, mid, lo


def _params(sem):
    return pltpu.CompilerParams(dimension_semantics=sem, vmem_limit_bytes=VMEM_LIMIT)


def _norm_kernel(x_ref, g_ref, ws_ref, h_ref, zs_ref):
    xf = x_ref[...]
    ms = jnp.mean(xf * xf, axis=-1, keepdims=True)
    h = (xf * lax.rsqrt(ms + RMS_EPS) * g_ref[...]).astype(BF16)
    h_ref[...] = h
    zs_ref[...] = _dot(h, ws_ref[...])


def _norm_small(x2, g, w_small, tm=512):
    m = x2.shape[0]
    return pl.pallas_call(
        _norm_kernel,
        grid=(m // tm,),
        in_specs=[
            pl.BlockSpec((tm, D_MODEL), lambda i: (i, 0)),
            pl.BlockSpec((1, D_MODEL), lambda i: (0, 0)),
            pl.BlockSpec((D_MODEL, LANES), lambda i: (0, 0)),
        ],
        out_specs=[pl.BlockSpec((tm, D_MODEL), lambda i: (i, 0)),
                   pl.BlockSpec((tm, LANES), lambda i: (i, 0))],
        out_shape=[jax.ShapeDtypeStruct((m, D_MODEL), BF16),
                   jax.ShapeDtypeStruct((m, LANES), F32)],
        compiler_params=_params(("parallel",)),
        name="norm_small",
    )(x2, g, w_small)


IN_TN = 1024
IN_SHIFT_B = GLA_GATE_RANK
IN_SHIFT_C = GLA_GATE_RANK + 3 * NSA_HEADS
IN_BLOCKS_A = Z_NQ // IN_TN
IN_BLOCKS_AB = Z_MG // IN_TN
IN_ROWS = 256


def _in_proj_kernel(h_ref, wm_ref, we_ref, z_ref, w_scr):
    j = pl.program_id(0)

    def build(shift):
        def body(r, carry):
            rows = pl.ds(pl.multiple_of(r * IN_ROWS, IN_ROWS), IN_ROWS)
            main = wm_ref[rows, :]
            if shift:
                full = jnp.concatenate([main, we_ref[rows, :]], axis=1)
                main = pltpu.roll(full, IN_TN + LANES - shift, 1)[:, :IN_TN]
            w_scr[rows, :] = main.astype(BF16)
            return carry
        lax.fori_loop(0, D_MODEL // IN_ROWS, body, 0)

    first = pl.program_id(1) == 0

    @pl.when(first & (j < IN_BLOCKS_A))
    def _():
        build(0)

    @pl.when(first & (j >= IN_BLOCKS_A) & (j < IN_BLOCKS_AB))
    def _():
        build(IN_SHIFT_B)

    @pl.when(first & (j >= IN_BLOCKS_AB))
    def _():
        build(IN_SHIFT_C)

    z_ref[...] = _dot(h_ref[...], w_scr[...]).astype(z_ref.dtype)


def _in_proj(h, w_in, layer, tm=1024):
    m = h.shape[0]
    tn = IN_TN
    return pl.pallas_call(
        _in_proj_kernel,
        grid=(Z_SMALL // tn, m // tm),
        in_specs=[
            pl.BlockSpec((tm, D_MODEL), lambda j, i: (i, 0)),
            pl.BlockSpec((None, D_MODEL, tn), lambda j, i: (layer, 0, j)),
            pl.BlockSpec((None, D_MODEL, LANES), lambda j, i: (layer, 0, (j + 1) * (tn // LANES))),
        ],
        out_specs=pl.BlockSpec((tm, tn), lambda j, i: (i, j)),
        out_shape=jax.ShapeDtypeStruct((m, Z_SMALL), BF16),
        scratch_shapes=[pltpu.VMEM((D_MODEL, tn), BF16)],
        compiler_params=_params(("parallel", "arbitrary")),
        name="in_proj",
    )(h, w_in, w_in)


def _gla_kernel(q_ref, k_ref, v_ref, r_ref, sm_ref, wg_ref, bg_ref, gn_ref, o_ref,
                state_ref, *, n_chunks):
    @pl.when(pl.program_id(2) == 0)
    def _():
        state_ref[...] = jnp.zeros_like(state_ref)

    C = GLA_CHUNK
    ts = n_chunks * C
    row = lax.broadcasted_iota(jnp.int32, (ts, ts), 0)
    col = lax.broadcasted_iota(jnp.int32, (ts, ts), 1)
    tri = jnp.where((row >= col) & ((row // C) == (col // C)), 1.0, 0.0).astype(BF16)
    crow = lax.broadcasted_iota(jnp.int32, (C, C), 0)
    ccol = lax.broadcasted_iota(jnp.int32, (C, C), 1)
    causal = crow >= ccol

    u = _dot(sm_ref[...].astype(BF16), wg_ref[...]) + bg_ref[...]
    log_a = (jnp.minimum(u, 0.0) - jnp.log(1.0 + jnp.exp(-jnp.abs(u)))) / GLA_TAU
    hi, lo = _split2(log_a)
    b = _dot(tri, hi) + _dot(tri, lo)
    q = q_ref[...].astype(F32) * (GLA_HK ** -0.5)
    k = k_ref[...].astype(F32)
    q_dec = (q * jnp.exp(b)).astype(BF16)
    k_dec = (k * jnp.exp(-b)).astype(BF16)
    gn = gn_ref[...]

    state = state_ref[...]
    for c in range(n_chunks):
        rows = slice(c * C, (c + 1) * C)
        b_c = b[rows]
        b_last = b_c[C - 1:C, :]
        k_end = (k[rows] * jnp.exp(b_last - b_c)).astype(BF16)
        v_c = v_ref[rows, :]
        attn = jnp.where(causal, _dot_nt(q_dec[rows], k_dec[rows]), 0.0).astype(BF16)
        o = _dot(attn, v_c) + _dot_nt(q_dec[rows], state.astype(BF16))
        state = state * jnp.exp(b_last) + _dot_tn(v_c, k_end)
        ms = jnp.mean(o * o, axis=-1, keepdims=True)
        y = o * lax.rsqrt(ms + RMS_EPS) * gn
        r = r_ref[rows, :].astype(F32)
        o_ref[rows, :] = (y * (r * _sigmoid(r))).astype(o_ref.dtype)
    state_ref[...] = state


def _gla(z3, zs3, w_gate_pad, b_gate, g_out, ts=512):
    bsz, s, _ = z3.shape
    qb, vb = GLA_HK, GLA_HV
    return pl.pallas_call(
        functools.partial(_gla_kernel, n_chunks=ts // GLA_CHUNK),
        grid=(bsz, GLA_HEADS, s // ts),
        in_specs=[
            pl.BlockSpec((None, ts, qb), lambda b, h, i: (b, i, Z_GQ // qb + h)),
            pl.BlockSpec((None, ts, qb), lambda b, h, i: (b, i, Z_GK // qb + h)),
            pl.BlockSpec((None, ts, vb), lambda b, h, i: (b, i, Z_GV // vb + h)),
            pl.BlockSpec((None, ts, vb), lambda b, h, i: (b, i, Z_GR // vb + h)),
            pl.BlockSpec((None, ts, LANES), lambda b, h, i: (b, i, 0)),
            pl.BlockSpec((LANES, qb), lambda b, h, i: (0, h)),
            pl.BlockSpec((1, qb), lambda b, h, i: (0, h)),
            pl.BlockSpec((1, vb), lambda b, h, i: (0, 0)),
        ],
        out_specs=pl.BlockSpec((None, ts, vb), lambda b, h, i: (b, i, h)),
        out_shape=jax.ShapeDtypeStruct((bsz, s, GLA_DV), BF16),
        scratch_shapes=[pltpu.VMEM((GLA_HV, GLA_HK), F32)],
        compiler_params=_params(("parallel", "parallel", "arbitrary")),
        name="gla",
    )(z3, z3, z3, z3, zs3, w_gate_pad, b_gate, g_out)


def _cmp_kernel(t_ref, pe_ref, w1_ref, w2_ref, o_ref, tf_scr, lo_scr, hi_scr):
    hd = NSA_HD
    half = CMP_STRIDE * hd
    nchunk = lo_scr.shape[0]
    tf_scr[...] = t_ref[...].astype(F32)
    for r in range(CMP_STRIDE):
        t_r = tf_scr[pl.ds(r, nchunk, stride=CMP_STRIDE), :]
        lo_scr[:, r * hd:(r + 1) * hd] = (t_r + pe_ref[r:r + 1, :]).astype(BF16)
        hi_scr[:, r * hd:(r + 1) * hd] = (t_r + pe_ref[CMP_STRIDE + r:CMP_STRIDE + r + 1, :]).astype(BF16)
    a = _dot(lo_scr[...], w1_ref[:half, :].astype(BF16))
    bm = _dot(hi_scr[...], w1_ref[half:, :].astype(BF16))
    pre = a + pltpu.roll(bm, nchunk - 1, 0)
    hid = (pre * _sigmoid(pre)).astype(BF16)
    o_ref[...] = _dot(hid, w2_ref[...].astype(BF16)).astype(o_ref.dtype)


def _compress(z3, pe, w1, w2):
    bsz, s, _ = z3.shape
    hkv, hd = NSA_KV_HEADS, NSA_HD
    nchunk = s // CMP_STRIDE
    width = CMP_STRIDE * hd
    return pl.pallas_call(
        _cmp_kernel,
        grid=(2, bsz, hkv),
        in_specs=[
            pl.BlockSpec((None, s, hd), lambda kv, b, h: (b, 0, Z_NKV // hd + kv * hkv + h)),
            pl.BlockSpec((None, CMP_LEN, hd), lambda kv, b, h: (kv, 0, 0)),
            pl.BlockSpec((None, 2 * width, CMP_HIDDEN), lambda kv, b, h: (kv, 0, 0)),
            pl.BlockSpec((None, CMP_HIDDEN, hd), lambda kv, b, h: (kv, 0, 0)),
        ],
        out_specs=pl.BlockSpec((None, None, None, nchunk, hd), lambda kv, b, h: (kv, b, h, 0, 0)),
        out_shape=jax.ShapeDtypeStruct((2, bsz, hkv, nchunk, hd), BF16),
        scratch_shapes=[pltpu.VMEM((s, hd), F32),
                        pltpu.VMEM((nchunk, width), BF16),
                        pltpu.VMEM((nchunk, width), BF16)],
        compiler_params=_params(("parallel", "parallel", "parallel")),
        name="nsa_compress",
    )(z3, pe, w1, w2)


NSA_TQ = 256
NSA_TK = NSA_TQ
LOG2E = 1.4426950408889634


def _nsa_kernel(q_ref, sm_ref, kc_ref, vc_ref, ks_ref, vs_ref, kw_ref, vw_ref, mcs_ref,
                o_ref, sel_scr, gate_scr, p_scr):
    G, TQ, TK, HD = NSA_GROUP, NSA_TQ, NSA_TK, NSA_HD
    R = G * TQ
    hk = pl.program_id(1)
    qi = pl.program_id(2)
    qs = qi * TQ

    def lanes4(x):
        return jnp.concatenate([x] * G, axis=1)

    qall = q_ref[...]
    q4 = jnp.concatenate([qall[:, g * HD:(g + 1) * HD] for g in range(G)], axis=0)
    q4 = (q4.astype(F32) * (HD ** -0.5 * LOG2E)).astype(BF16)

    nc = kc_ref.shape[0]
    tq_c = qs + lax.broadcasted_iota(jnp.int32, (nc, TQ), 1)
    cmp_end = lax.broadcasted_iota(jnp.int32, (nc, TQ), 0) * CMP_STRIDE + (CMP_LEN - 1)
    valid_c = lanes4(jnp.where(cmp_end <= tq_c, 1.0, 0.0)) > 0.5
    s_c = jnp.where(valid_c, _dot_nt(kc_ref[...], q4), NEG_INF)
    e_c = jnp.where(valid_c, jnp.exp2(s_c - jnp.max(s_c, axis=0, keepdims=True)), 0.0)
    l_c = jnp.sum(e_c, axis=0, keepdims=True)
    p_c = e_c / jnp.where(l_c > 0.0, l_c, 1.0)
    o_c = _dot_tn(vc_ref[...], p_c.astype(BF16))

    p_sum = p_c[:, 0:TQ]
    for g in range(1, G):
        p_sum = p_sum + p_c[:, g * TQ:(g + 1) * TQ]
    mcs = mcs_ref[...]
    nb = mcs.shape[0]
    imp = sum(_dot(mcs, part) for part in _split3(p_sum))
    blk = lax.broadcasted_iota(jnp.int32, (nb, TQ), 0)
    tq_b = qs + lax.broadcasted_iota(jnp.int32, (nb, TQ), 1)
    cur = tq_b // SLC_BLOCK
    forced = (blk == 0) | (blk == cur) | (blk == cur - 1)
    imp = jnp.where(forced, jnp.inf, imp)
    imp = jnp.where(blk * SLC_BLOCK <= tq_b, imp, -jnp.inf)
    rank = jnp.zeros((nb, TQ), F32)
    for kb in range(nb):
        rk = imp[kb:kb + 1, :]
        ahead = jnp.where(rk > imp, 1.0, jnp.where(rk == imp, jnp.where(blk > kb, 1.0, 0.0), 0.0))
        rank = rank + ahead
    sel_scr[...] = jnp.where(rank < float(min(SLC_TOPN, nb)), 0.0, NEG_INF)

    blocks_per_tile = TK // SLC_BLOCK

    def sel_bias(tile):
        return jnp.concatenate(
            [jnp.broadcast_to(sel_scr[pl.ds(tile * blocks_per_tile + i, 1), :], (SLC_BLOCK, TQ))
             for i in range(blocks_per_tile)], axis=0)

    krow = lax.broadcasted_iota(jnp.int32, (TK, TQ), 0)
    qcol = lax.broadcasted_iota(jnp.int32, (TK, TQ), 1)
    bias_d = jnp.where(krow <= qcol, sel_bias(qi), NEG_INF)
    s_d = _dot_nt(ks_ref[pl.ds(pl.multiple_of(qs, TK), TK), :], q4) + lanes4(bias_d)
    m_d = jnp.max(s_d, axis=0, keepdims=True)
    p_d = jnp.exp2(s_d - m_d)
    l_d = jnp.sum(p_d, axis=0, keepdims=True)
    p_scr[...] = p_d.astype(BF16)

    def kv_step(j, carry):
        m, l, acc, voff = carry
        pv = _dot_tn(vs_ref[pl.ds(pl.multiple_of(voff, TK), TK), :], p_scr[...])
        koff = pl.multiple_of(j * TK, TK)
        s = _dot_nt(ks_ref[pl.ds(koff, TK), :], q4) + lanes4(sel_bias(j))
        m_new = jnp.maximum(m, jnp.max(s, axis=0, keepdims=True))
        alpha = jnp.exp2(m - m_new)
        p = jnp.exp2(s - m_new)
        l = alpha * l + jnp.sum(p, axis=0, keepdims=True)
        acc = alpha * (acc + pv)
        p_scr[...] = p.astype(BF16)
        return m_new, l, acc, koff

    _, l_s, acc_s, voff = lax.fori_loop(0, qi, kv_step, (m_d, l_d, jnp.zeros((HD, R), F32), qs))
    acc_s = acc_s + _dot_tn(vs_ref[pl.ds(pl.multiple_of(voff, TK), TK), :], p_scr[...])
    o_s = acc_s / l_s

    span = WINDOW + TQ
    w0 = pl.multiple_of(jnp.maximum(qs - WINDOW, 0), TQ)
    kpos = w0 + lax.broadcasted_iota(jnp.int32, (span, TQ), 0)
    tq_w = qs + lax.broadcasted_iota(jnp.int32, (span, TQ), 1)
    bias_w = jnp.where(kpos <= tq_w, jnp.where(kpos > tq_w - WINDOW, 0.0, NEG_INF), NEG_INF)
    s_w = _dot_nt(kw_ref[pl.ds(w0, span), :], q4) + lanes4(bias_w)
    e_w = jnp.exp2(s_w - jnp.max(s_w, axis=0, keepdims=True))
    o_w = _dot_tn(vw_ref[pl.ds(w0, span), :], e_w.astype(BF16)) / jnp.sum(e_w, axis=0, keepdims=True)

    gate_scr[...] = _sigmoid(sm_ref[...]).T

    def gate_row(branch):
        base = SMALL_NGATE + branch * NSA_HEADS + hk * G
        return jnp.concatenate([gate_scr[pl.ds(base + g, 1), :] for g in range(G)], axis=1)

    o = gate_row(0) * o_c + gate_row(1) * o_s + gate_row(2) * o_w
    for g in range(G):
        o_ref[:, g * HD:(g + 1) * HD] = o[:, g * TQ:(g + 1) * TQ].T.astype(o_ref.dtype)


def _nsa(z3, zs3, cmp, mcs_t):
    bsz, s, _ = z3.shape
    hkv, hd, tq = NSA_KV_HEADS, NSA_HD, NSA_TQ
    qw = NSA_GROUP * hd
    nc = cmp.shape[3]
    nb = mcs_t.shape[0]
    kv_col = lambda i: (Z_NKV + i * hkv * hd) // hd

    def kv_spec(i):
        return pl.BlockSpec((None, s, hd), lambda b, h, q: (b, 0, kv_col(i) + h))

    return pl.pallas_call(
        _nsa_kernel,
        grid=(bsz, hkv, s // tq),
        in_specs=[
            pl.BlockSpec((None, tq, qw), lambda b, h, q: (b, q, Z_NQ // qw + h)),
            pl.BlockSpec((None, tq, LANES), lambda b, h, q: (b, q, 0)),
            pl.BlockSpec((None, None, None, nc, hd), lambda b, h, q: (0, b, h, 0, 0)),
            pl.BlockSpec((None, None, None, nc, hd), lambda b, h, q: (1, b, h, 0, 0)),
            kv_spec(2), kv_spec(3), kv_spec(4), kv_spec(5),
            pl.BlockSpec(mcs_t.shape, lambda b, h, q: (0, 0)),
        ],
        out_specs=pl.BlockSpec((None, tq, qw), lambda b, h, q: (b, q, h)),
        out_shape=jax.ShapeDtypeStruct((bsz, s, NSA_HEADS * hd), BF16),
        scratch_shapes=[pltpu.VMEM((nb, tq), F32), pltpu.VMEM((LANES, tq), F32),
                        pltpu.VMEM((NSA_TK, NSA_GROUP * tq), BF16)],
        compiler_params=_params(("parallel", "parallel", "arbitrary")),
        name="nsa_attn",
    )(z3, zs3, cmp, cmp, z3, z3, z3, z3, mcs_t)


def _merge_kernel(og_ref, on_ref, wg_ref, wn_ref, ga_ref, gb_ref, o_ref, wg_scr, wn_scr):
    @pl.when(pl.program_id(1) == 0)
    def _():
        wg_scr[...] = wg_ref[...].astype(BF16)
        wn_scr[...] = wn_ref[...].astype(BF16)

    a = _dot(og_ref[...], wg_scr[...])
    b = _dot(on_ref[...], wn_scr[...])
    ga = _sigmoid(ga_ref[...].astype(F32))
    gb = _sigmoid(gb_ref[...].astype(F32))
    o_ref[...] = (ga * a + gb * b).astype(o_ref.dtype)


def _merge(o_gla, o_nsa, w_pg, w_pn, z2, tm=1024, tn=512):
    m = o_gla.shape[0]
    return pl.pallas_call(
        _merge_kernel,
        grid=(D_MODEL // tn, m // tm),
        in_specs=[
            pl.BlockSpec((tm, GLA_DV), lambda j, i: (i, 0)),
            pl.BlockSpec((tm, NSA_HEADS * NSA_HD), lambda j, i: (i, 0)),
            pl.BlockSpec((GLA_DV, tn), lambda j, i: (0, j)),
            pl.BlockSpec((NSA_HEADS * NSA_HD, tn), lambda j, i: (0, j)),
            pl.BlockSpec((tm, tn), lambda j, i: (i, Z_MG // tn + j)),
            pl.BlockSpec((tm, tn), lambda j, i: (i, (Z_MG + D_MODEL) // tn + j)),
        ],
        out_specs=pl.BlockSpec((tm, tn), lambda j, i: (i, j)),
        out_shape=jax.ShapeDtypeStruct((m, D_MODEL), BF16),
        scratch_shapes=[pltpu.VMEM((GLA_DV, tn), BF16), pltpu.VMEM((NSA_HEADS * NSA_HD, tn), BF16)],
        compiler_params=_params(("parallel", "arbitrary")),
        name="merge_proj",
    )(o_gla, o_nsa, w_pg, w_pn, z2, z2)


def _out_proj_kernel(mg_ref, w_ref, x_ref, g_ref, x1_ref, h_ref):
    x1 = x_ref[...] + _dot(mg_ref[...], w_ref[...])
    x1_ref[...] = x1
    ms = jnp.mean(x1 * x1, axis=-1, keepdims=True)
    h_ref[...] = (x1 * lax.rsqrt(ms + RMS_EPS) * g_ref[...]).astype(h_ref.dtype)


def _out_proj(merged, w_out, x2, g_ffn, tm=512):
    m = merged.shape[0]
    return pl.pallas_call(
        _out_proj_kernel,
        grid=(m // tm,),
        in_specs=[
            pl.BlockSpec((tm, D_MODEL), lambda i: (i, 0)),
            pl.BlockSpec((D_MODEL, D_MODEL), lambda i: (0, 0)),
            pl.BlockSpec((tm, D_MODEL), lambda i: (i, 0)),
            pl.BlockSpec((1, D_MODEL), lambda i: (0, 0)),
        ],
        out_specs=[
            pl.BlockSpec((tm, D_MODEL), lambda i: (i, 0)),
            pl.BlockSpec((tm, D_MODEL), lambda i: (i, 0)),
        ],
        out_shape=[jax.ShapeDtypeStruct((m, D_MODEL), F32),
                   jax.ShapeDtypeStruct((m, D_MODEL), BF16)],
        compiler_params=_params(("parallel",)),
        name="out_proj",
    )(merged, w_out, x2, g_ffn)


def _ffn_up_kernel(h_ref, wg_ref, wu_ref, o_ref, wg_scr, wu_scr):
    @pl.when(pl.program_id(1) == 0)
    def _():
        wg_scr[...] = wg_ref[...].astype(BF16)
        wu_scr[...] = wu_ref[...].astype(BF16)

    h = h_ref[...]
    g = _dot(h, wg_scr[...])
    u = _dot(h, wu_scr[...])
    o_ref[...] = (g * _sigmoid(g) * u).astype(o_ref.dtype)


def _ffn_up(h2, w_gate, w_up, tm=1024, tf=512):
    m = h2.shape[0]
    return pl.pallas_call(
        _ffn_up_kernel,
        grid=(D_FF // tf, m // tm),
        in_specs=[
            pl.BlockSpec((tm, D_MODEL), lambda j, i: (i, 0)),
            pl.BlockSpec((D_MODEL, tf), lambda j, i: (0, j)),
            pl.BlockSpec((D_MODEL, tf), lambda j, i: (0, j)),
        ],
        out_specs=pl.BlockSpec((tm, tf), lambda j, i: (i, j)),
        out_shape=jax.ShapeDtypeStruct((m, D_FF), BF16),
        scratch_shapes=[pltpu.VMEM((D_MODEL, tf), BF16), pltpu.VMEM((D_MODEL, tf), BF16)],
        compiler_params=_params(("parallel", "arbitrary")),
        name="ffn_up",
    )(h2, w_gate, w_up)


def _ffn_down_kernel(a_ref, w_ref, x1_ref, g_ref, o_ref, acc_ref, *, final_norm):
    k = pl.program_id(1)

    @pl.when(k == 0)
    def _():
        acc_ref[...] = x1_ref[...]

    acc_ref[...] += _dot(a_ref[...], w_ref[...])

    @pl.when(k == pl.num_programs(1) - 1)
    def _():
        y = acc_ref[...]
        if final_norm:
            ms = jnp.mean(y * y, axis=-1, keepdims=True)
            y = y * lax.rsqrt(ms + RMS_EPS) * g_ref[...]
        o_ref[...] = y


def _ffn_down(act, w_down, x1, g_final, final_norm, tm=512, tk=1408):
    m = act.shape[0]
    return pl.pallas_call(
        functools.partial(_ffn_down_kernel, final_norm=final_norm),
        grid=(m // tm, D_FF // tk),
        in_specs=[
            pl.BlockSpec((tm, tk), lambda i, k: (i, k)),
            pl.BlockSpec((tk, D_MODEL), lambda i, k: (k, 0)),
            pl.BlockSpec((tm, D_MODEL), lambda i, k: (i, 0)),
            pl.BlockSpec((1, D_MODEL), lambda i, k: (0, 0)),
        ],
        out_specs=pl.BlockSpec((tm, D_MODEL), lambda i, k: (i, 0)),
        out_shape=jax.ShapeDtypeStruct((m, D_MODEL), F32),
        scratch_shapes=[pltpu.VMEM((tm, D_MODEL), F32)],
        compiler_params=_params(("parallel", "arbitrary")),
        name="ffn_down",
    )(act, w_down, x1, g_final)


def _small_w_in(w):
    o = np.cumsum((0,) + IN_SPLIT_SIZES)
    pad = jnp.zeros((w.shape[0], LANES - GLA_GATE_RANK - 3 * NSA_HEADS), w.dtype)
    return jnp.concatenate([w[:, o[4]:o[5]], w[:, o[7]:o[8]], pad], axis=1).astype(BF16)


def _cmp_to_slc_weights_t(seq):
    n_cmp = seq // CMP_STRIDE - 1
    n_slc = seq // SLC_BLOCK
    c0 = np.arange(n_cmp)[:, None] * CMP_STRIDE
    s0 = np.arange(n_slc)[None, :] * SLC_BLOCK
    ov = np.clip(np.minimum(c0 + CMP_LEN, s0 + SLC_BLOCK) - np.maximum(c0, s0), 0, None)
    m_cs = np.zeros((n_cmp + 1, n_slc), np.float32)
    m_cs[:n_cmp] = ov / CMP_LEN
    return jnp.asarray(m_cs.T, BF16)


def kernel(x, g_mix, w_in, w_gla_gate, b_gla_gate, g_gla_out, pe_cmp_k, w_cmp_k1, w_cmp_k2,
           pe_cmp_v, w_cmp_v1, w_cmp_v2, w_proj_gla, w_proj_nsa, w_out, g_ffn,
           w_ffn_gate, w_ffn_up, w_ffn_down, g_final):
    bsz, seq, d = x.shape
    m = bsz * seq
    depth = w_in.shape[0]
    mcs_t = _cmp_to_slc_weights_t(seq)
    xc = x.reshape(m, d)
    for l in range(depth):
        h, zs = _norm_small(xc, g_mix[l][None], _small_w_in(w_in[l]))
        z2 = _in_proj(h, w_in, l)
        z3 = z2.reshape(bsz, seq, Z_SMALL)
        zs3 = zs.reshape(bsz, seq, LANES)

        w_gate_pad = jnp.zeros((LANES, GLA_DK), F32).at[SMALL_GLR:SMALL_GLR + GLA_GATE_RANK].set(w_gla_gate[l])
        o_gla = _gla(z3, zs3, w_gate_pad.astype(BF16), b_gla_gate[l][None], g_gla_out[l][None])

        cmp = _compress(z3, jnp.stack([pe_cmp_k[l], pe_cmp_v[l]]),
                        jnp.stack([w_cmp_k1[l], w_cmp_v1[l]]), jnp.stack([w_cmp_k2[l], w_cmp_v2[l]]))
        o_nsa = _nsa(z3, zs3, cmp, mcs_t)

        merged = _merge(o_gla.reshape(m, GLA_DV), o_nsa.reshape(m, NSA_HEADS * NSA_HD),
                        w_proj_gla[l], w_proj_nsa[l], z2)
        x1, h2 = _out_proj(merged, w_out[l].astype(BF16), xc, g_ffn[l][None])
        act = _ffn_up(h2, w_ffn_gate[l], w_ffn_up[l])
        xc = _ffn_down(act, w_ffn_down[l].astype(BF16), x1, g_final[None],
                       final_norm=(l == depth - 1))
    return xc.reshape(bsz, seq, d)
```

```python
import functools

import numpy as np
import jax
import jax.numpy as jnp
from jax import lax
from jax.experimental import pallas as pl
from jax.experimental.pallas import tpu as pltpu

F32 = jnp.float32
BF16 = jnp.bfloat16

D_MODEL = 2048
GLA_HEADS = 4
GLA_DK = D_MODEL // 2
GLA_DV = D_MODEL
GLA_HK = GLA_DK // GLA_HEADS
GLA_HV = GLA_DV // GLA_HEADS
GLA_GATE_RANK = 16
GLA_TAU = 16.0
GLA_CHUNK = 64

NSA_HEADS = 16
NSA_KV_HEADS = 4
NSA_GROUP = NSA_HEADS // NSA_KV_HEADS
NSA_HD = D_MODEL // NSA_HEADS
CMP_STRIDE = 16
CMP_LEN = 2 * CMP_STRIDE
CMP_HIDDEN = 256
SLC_BLOCK = 64
SLC_TOPN = 16
WINDOW = 512
D_FF = ((8 * D_MODEL // 3 + 255) // 256) * 256

IN_SPLIT_SIZES = (GLA_DK, GLA_DK, GLA_DV, GLA_DV, GLA_GATE_RANK,
                  NSA_HEADS * NSA_HD, 6 * NSA_KV_HEADS * NSA_HD, 3 * NSA_HEADS,
                  2 * D_MODEL)

NEG_INF = -1e30
RMS_EPS = 1e-6

Z_GQ = 0
Z_GK = Z_GQ + GLA_DK
Z_GV = Z_GK + GLA_DK
Z_GR = Z_GV + GLA_DV
Z_NQ = Z_GR + GLA_DV
Z_NKV = Z_NQ + NSA_HEADS * NSA_HD
Z_MG = Z_NKV + 6 * NSA_KV_HEADS * NSA_HD
Z_SMALL = Z_MG + 2 * D_MODEL
LANES = 128
SMALL_GLR = 0
SMALL_NGATE = GLA_GATE_RANK

VMEM_LIMIT = 56 * 1024 * 1024


def _dot(a, b):
    return jnp.dot(a, b, preferred_element_type=F32)


def _dot_nt(a, b):
    return lax.dot_general(a, b, (((1,), (1,)), ((), ())), preferred_element_type=F32)


def _dot_tn(a, b):
    return lax.dot_general(a, b, (((0,), (0,)), ((), ())), preferred_element_type=F32)


def _sigmoid(x):
    return 1.0 / (1.0 + jnp.exp(-x))


def _split2(x):
    hi = x.astype(BF16)
    lo = (x - hi.astype(F32)).astype(BF16)
    return hi, lo


def _split3(x):
    hi = x.astype(BF16)
    r1 = x - hi.astype(F32)
    mid = r1.astype(BF16)
    lo = (r1 - mid.astype(F32)).astype(BF16)
    return hi, mid, lo


def _params(sem):
    return pltpu.CompilerParams(dimension_semantics=sem, vmem_limit_bytes=VMEM_LIMIT)


_IN_OFFS = np.cumsum((0,) + IN_SPLIT_SIZES)
COL_GLR = int(_IN_OFFS[4])
COL_NGATE = int(_IN_OFFS[7])
assert COL_GLR % LANES == SMALL_GLR and COL_NGATE % LANES == SMALL_NGATE
assert SMALL_GLR + GLA_GATE_RANK == SMALL_NGATE and SMALL_NGATE + 3 * NSA_HEADS <= LANES


def _norm_kernel(x_ref, g_ref, wa_ref, wb_ref, h_ref, zs_ref):
    xf = x_ref[...]
    ms = jnp.mean(xf * xf, axis=-1, keepdims=True)
    h = (xf * lax.rsqrt(ms + RMS_EPS) * g_ref[...]).astype(BF16)
    h_ref[...] = h
    row = lax.broadcasted_iota(jnp.int32, wa_ref.shape, 0)
    w_small = jnp.where(row < SMALL_NGATE, wa_ref[...],
                        jnp.where(row < SMALL_NGATE + 3 * NSA_HEADS, wb_ref[...], 0.0))
    zs_ref[...] = _dot_nt(h, w_small.astype(BF16))


def _norm_small(x2, g, w_t, tm=512):
    m = x2.shape[0]
    return pl.pallas_call(
        _norm_kernel,
        grid=(m // tm,),
        in_specs=[
            pl.BlockSpec((tm, D_MODEL), lambda i: (i, 0)),
            pl.BlockSpec((1, D_MODEL), lambda i: (0, 0)),
            pl.BlockSpec((LANES, D_MODEL), lambda i: (COL_GLR // LANES, 0)),
            pl.BlockSpec((LANES, D_MODEL), lambda i: (COL_NGATE // LANES, 0)),
        ],
        out_specs=[pl.BlockSpec((tm, D_MODEL), lambda i: (i, 0)),
                   pl.BlockSpec((tm, LANES), lambda i: (i, 0))],
        out_shape=[jax.ShapeDtypeStruct((m, D_MODEL), BF16),
                   jax.ShapeDtypeStruct((m, LANES), F32)],
        compiler_params=_params(("parallel",)),
        name="norm_small",
    )(x2, g, w_t, w_t)


IN_TN = 1024
IN_SHIFT_B = GLA_GATE_RANK
IN_SHIFT_C = GLA_GATE_RANK + 3 * NSA_HEADS
IN_BLOCKS_A = Z_NQ // IN_TN
IN_BLOCKS_AB = Z_MG // IN_TN
IN_ROWS = 48
assert (IN_TN - IN_SHIFT_B) % IN_ROWS == 0 and (IN_TN - IN_SHIFT_C) % IN_ROWS == 0


def _in_proj_kernel(h_ref, wm_ref, we_ref, z_ref, w_scr):
    j = pl.program_id(0)

    def build(shift):
        if shift == 0:
            w_scr[...] = wm_ref[...].astype(BF16)
            return

        def body(r, carry):
            dst = pl.ds(pl.multiple_of(r * IN_ROWS, 16), IN_ROWS)
            src = pl.ds(pl.multiple_of(r * IN_ROWS + shift, 8), IN_ROWS)
            w_scr[dst, :] = wm_ref[src, :].astype(BF16)
            return carry
        lax.fori_loop(0, (IN_TN - shift) // IN_ROWS, body, 0)
        w_scr[IN_TN - shift:, :] = we_ref[:shift, :].astype(BF16)

    first = pl.program_id(1) == 0

    @pl.when(first & (j < IN_BLOCKS_A))
    def _():
        build(0)

    @pl.when(first & (j >= IN_BLOCKS_A) & (j < IN_BLOCKS_AB))
    def _():
        build(IN_SHIFT_B)

    @pl.when(first & (j >= IN_BLOCKS_AB))
    def _():
        build(IN_SHIFT_C)

    z_ref[...] = _dot_nt(h_ref[...], w_scr[...]).astype(z_ref.dtype)


def _in_proj(h, w_t, tm=1024):
    m = h.shape[0]
    tn = IN_TN
    return pl.pallas_call(
        _in_proj_kernel,
        grid=(Z_SMALL // tn, m // tm),
        in_specs=[
            pl.BlockSpec((tm, D_MODEL), lambda j, i: (i, 0)),
            pl.BlockSpec((tn, D_MODEL), lambda j, i: (j, 0)),
            pl.BlockSpec((LANES, D_MODEL), lambda j, i: ((j + 1) * (tn // LANES), 0)),
        ],
        out_specs=pl.BlockSpec((tm, tn), lambda j, i: (i, j)),
        out_shape=jax.ShapeDtypeStruct((m, Z_SMALL), BF16),
        scratch_shapes=[pltpu.VMEM((tn, D_MODEL), BF16)],
        compiler_params=_params(("parallel", "arbitrary")),
        name="in_proj",
    )(h, w_t, w_t)


def _gla_kernel(q_ref, k_ref, v_ref, r_ref, sm_ref, wg_ref, bg_ref, gn_ref, o_ref,
                state_ref, *, n_chunks):
    @pl.when(pl.program_id(2) == 0)
    def _():
        state_ref[...] = jnp.zeros_like(state_ref)

    C = GLA_CHUNK
    ts = n_chunks * C
    row = lax.broadcasted_iota(jnp.int32, (ts, ts), 0)
    col = lax.broadcasted_iota(jnp.int32, (ts, ts), 1)
    tri = jnp.where((row >= col) & ((row // C) == (col // C)), 1.0, 0.0).astype(BF16)
    crow = lax.broadcasted_iota(jnp.int32, (C, C), 0)
    ccol = lax.broadcasted_iota(jnp.int32, (C, C), 1)
    causal = crow >= ccol

    u = _dot(sm_ref[...].astype(BF16), wg_ref[...]) + bg_ref[...]
    log_a = (jnp.minimum(u, 0.0) - jnp.log(1.0 + jnp.exp(-jnp.abs(u)))) / GLA_TAU
    hi, lo = _split2(log_a)
    b = _dot(tri, hi) + _dot(tri, lo)
    q = q_ref[...].astype(F32) * (GLA_HK ** -0.5)
    k = k_ref[...].astype(F32)
    q_dec = (q * jnp.exp(b)).astype(BF16)
    k_dec = (k * jnp.exp(-b)).astype(BF16)
    gn = gn_ref[...]

    state = state_ref[...]
    for c in range(n_chunks):
        rows = slice(c * C, (c + 1) * C)
        b_c = b[rows]
        b_last = b_c[C - 1:C, :]
        k_end = (k[rows] * jnp.exp(b_last - b_c)).astype(BF16)
        v_c = v_ref[rows, :]
        attn = jnp.where(causal, _dot_nt(q_dec[rows], k_dec[rows]), 0.0).astype(BF16)
        o = _dot(attn, v_c) + _dot_nt(q_dec[rows], state.astype(BF16))
        state = state * jnp.exp(b_last) + _dot_tn(v_c, k_end)
        ms = jnp.mean(o * o, axis=-1, keepdims=True)
        y = o * lax.rsqrt(ms + RMS_EPS) * gn
        r = r_ref[rows, :].astype(F32)
        o_ref[rows, :] = (y * (r * _sigmoid(r))).astype(o_ref.dtype)
    state_ref[...] = state


def _gla(z3, zs3, w_gate_pad, b_gate, g_out, ts=512):
    bsz, s, _ = z3.shape
    qb, vb = GLA_HK, GLA_HV
    return pl.pallas_call(
        functools.partial(_gla_kernel, n_chunks=ts // GLA_CHUNK),
        grid=(bsz, GLA_HEADS, s // ts),
        in_specs=[
            pl.BlockSpec((None, ts, qb), lambda b, h, i: (b, i, Z_GQ // qb + h)),
            pl.BlockSpec((None, ts, qb), lambda b, h, i: (b, i, Z_GK // qb + h)),
            pl.BlockSpec((None, ts, vb), lambda b, h, i: (b, i, Z_GV // vb + h)),
            pl.BlockSpec((None, ts, vb), lambda b, h, i: (b, i, Z_GR // vb + h)),
            pl.BlockSpec((None, ts, LANES), lambda b, h, i: (b, i, 0)),
            pl.BlockSpec((LANES, qb), lambda b, h, i: (0, h)),
            pl.BlockSpec((1, qb), lambda b, h, i: (0, h)),
            pl.BlockSpec((1, vb), lambda b, h, i: (0, 0)),
        ],
        out_specs=pl.BlockSpec((None, ts, vb), lambda b, h, i: (b, i, h)),
        out_shape=jax.ShapeDtypeStruct((bsz, s, GLA_DV), BF16),
        scratch_shapes=[pltpu.VMEM((GLA_HV, GLA_HK), F32)],
        compiler_params=_params(("parallel", "parallel", "arbitrary")),
        name="gla",
    )(z3, z3, z3, z3, zs3, w_gate_pad, b_gate, g_out)


def _cmp_kernel(t_ref, pe_ref, w1_ref, w2_ref, o_ref, tf_scr, lo_scr, hi_scr):
    hd = NSA_HD
    half = CMP_STRIDE * hd
    nchunk = lo_scr.shape[0]
    tf_scr[...] = t_ref[...].astype(F32)
    for r in range(CMP_STRIDE):
        t_r = tf_scr[pl.ds(r, nchunk, stride=CMP_STRIDE), :]
        lo_scr[:, r * hd:(r + 1) * hd] = (t_r + pe_ref[r:r + 1, :]).astype(BF16)
        hi_scr[:, r * hd:(r + 1) * hd] = (t_r + pe_ref[CMP_STRIDE + r:CMP_STRIDE + r + 1, :]).astype(BF16)
    a = _dot(lo_scr[...], w1_ref[:half, :].astype(BF16))
    bm = _dot(hi_scr[...], w1_ref[half:, :].astype(BF16))
    pre = a + pltpu.roll(bm, nchunk - 1, 0)
    hid = (pre * _sigmoid(pre)).astype(BF16)
    o_ref[...] = _dot(hid, w2_ref[...].astype(BF16)).astype(o_ref.dtype)


def _compress(z3, pe, w1, w2):
    bsz, s, _ = z3.shape
    hkv, hd = NSA_KV_HEADS, NSA_HD
    nchunk = s // CMP_STRIDE
    width = CMP_STRIDE * hd
    return pl.pallas_call(
        _cmp_kernel,
        grid=(2, bsz, hkv),
        in_specs=[
            pl.BlockSpec((None, s, hd), lambda kv, b, h: (b, 0, Z_NKV // hd + kv * hkv + h)),
            pl.BlockSpec((None, CMP_LEN, hd), lambda kv, b, h: (kv, 0, 0)),
            pl.BlockSpec((None, 2 * width, CMP_HIDDEN), lambda kv, b, h: (kv, 0, 0)),
            pl.BlockSpec((None, CMP_HIDDEN, hd), lambda kv, b, h: (kv, 0, 0)),
        ],
        out_specs=pl.BlockSpec((None, None, None, nchunk, hd), lambda kv, b, h: (kv, b, h, 0, 0)),
        out_shape=jax.ShapeDtypeStruct((2, bsz, hkv, nchunk, hd), BF16),
        scratch_shapes=[pltpu.VMEM((s, hd), F32),
                        pltpu.VMEM((nchunk, width), BF16),
                        pltpu.VMEM((nchunk, width), BF16)],
        compiler_params=_params(("parallel", "parallel", "parallel")),
        name="nsa_compress",
    )(z3, pe, w1, w2)


NSA_TQ = 256
NSA_TK = NSA_TQ
LOG2E = 1.4426950408889634


def _nsa_kernel(q_ref, sm_ref, kc_ref, vc_ref, ks_ref, vs_ref, kw_ref, vw_ref, mcs_ref,
                o_ref, sel_scr, gate_scr, p_scr):
    G, TQ, TK, HD = NSA_GROUP, NSA_TQ, NSA_TK, NSA_HD
    R = G * TQ
    hk = pl.program_id(1)
    qi = pl.program_id(2)
    qs = qi * TQ

    def lanes4(x):
        return jnp.concatenate([x] * G, axis=1)

    qall = q_ref[...]
    q4 = jnp.concatenate([qall[:, g * HD:(g + 1) * HD] for g in range(G)], axis=0)
    q4 = (q4.astype(F32) * (HD ** -0.5 * LOG2E)).astype(BF16)

    nc = kc_ref.shape[0]
    tq_c = qs + lax.broadcasted_iota(jnp.int32, (nc, TQ), 1)
    cmp_end = lax.broadcasted_iota(jnp.int32, (nc, TQ), 0) * CMP_STRIDE + (CMP_LEN - 1)
    valid_c = lanes4(jnp.where(cmp_end <= tq_c, 1.0, 0.0)) > 0.5
    s_c = jnp.where(valid_c, _dot_nt(kc_ref[...], q4), NEG_INF)
    e_c = jnp.where(valid_c, jnp.exp2(s_c - jnp.max(s_c, axis=0, keepdims=True)), 0.0)
    l_c = jnp.sum(e_c, axis=0, keepdims=True)
    p_c = e_c / jnp.where(l_c > 0.0, l_c, 1.0)
    o_c = _dot_tn(vc_ref[...], p_c.astype(BF16))

    p_sum = p_c[:, 0:TQ]
    for g in range(1, G):
        p_sum = p_sum + p_c[:, g * TQ:(g + 1) * TQ]
    mcs = mcs_ref[...]
    nb = mcs.shape[0]
    imp = sum(_dot(mcs, part) for part in _split3(p_sum))
    blk = lax.broadcasted_iota(jnp.int32, (nb, TQ), 0)
    tq_b = qs + lax.broadcasted_iota(jnp.int32, (nb, TQ), 1)
    cur = tq_b // SLC_BLOCK
    forced = (blk == 0) | (blk == cur) | (blk == cur - 1)
    imp = jnp.where(forced, jnp.inf, imp)
    imp = jnp.where(blk * SLC_BLOCK <= tq_b, imp, -jnp.inf)
    rank = jnp.zeros((nb, TQ), F32)
    for kb in range(nb):
        rk = imp[kb:kb + 1, :]
        ahead = jnp.where(rk > imp, 1.0, jnp.where(rk == imp, jnp.where(blk > kb, 1.0, 0.0), 0.0))
        rank = rank + ahead
    sel_scr[...] = jnp.where(rank < float(min(SLC_TOPN, nb)), 0.0, NEG_INF)

    blocks_per_tile = TK // SLC_BLOCK

    def sel_bias(tile):
        return jnp.concatenate(
            [jnp.broadcast_to(sel_scr[pl.ds(tile * blocks_per_tile + i, 1), :], (SLC_BLOCK, TQ))
             for i in range(blocks_per_tile)], axis=0)

    krow = lax.broadcasted_iota(jnp.int32, (TK, TQ), 0)
    qcol = lax.broadcasted_iota(jnp.int32, (TK, TQ), 1)
    bias_d = jnp.where(krow <= qcol, sel_bias(qi), NEG_INF)
    s_d = _dot_nt(ks_ref[pl.ds(pl.multiple_of(qs, TK), TK), :], q4) + lanes4(bias_d)
    m_d = jnp.max(s_d, axis=0, keepdims=True)
    p_d = jnp.exp2(s_d - m_d)
    l_d = jnp.sum(p_d, axis=0, keepdims=True)
    p_scr[...] = p_d.astype(BF16)

    def kv_step(j, carry):
        m, l, acc, voff = carry
        pv = _dot_tn(vs_ref[pl.ds(pl.multiple_of(voff, TK), TK), :], p_scr[...])
        koff = pl.multiple_of(j * TK, TK)
        s = _dot_nt(ks_ref[pl.ds(koff, TK), :], q4) + lanes4(sel_bias(j))
        m_new = jnp.maximum(m, jnp.max(s, axis=0, keepdims=True))
        alpha = jnp.exp2(m - m_new)
        p = jnp.exp2(s - m_new)
        l = alpha * l + jnp.sum(p, axis=0, keepdims=True)
        acc = alpha * (acc + pv)
        p_scr[...] = p.astype(BF16)
        return m_new, l, acc, koff

    _, l_s, acc_s, voff = lax.fori_loop(0, qi, kv_step, (m_d, l_d, jnp.zeros((HD, R), F32), qs))
    acc_s = acc_s + _dot_tn(vs_ref[pl.ds(pl.multiple_of(voff, TK), TK), :], p_scr[...])
    o_s = acc_s / l_s

    span = WINDOW + TQ
    w0 = pl.multiple_of(jnp.maximum(qs - WINDOW, 0), TQ)
    kpos = w0 + lax.broadcasted_iota(jnp.int32, (span, TQ), 0)
    tq_w = qs + lax.broadcasted_iota(jnp.int32, (span, TQ), 1)
    bias_w = jnp.where(kpos <= tq_w, jnp.where(kpos > tq_w - WINDOW, 0.0, NEG_INF), NEG_INF)
    s_w = _dot_nt(kw_ref[pl.ds(w0, span), :], q4) + lanes4(bias_w)
    e_w = jnp.exp2(s_w - jnp.max(s_w, axis=0, keepdims=True))
    o_w = _dot_tn(vw_ref[pl.ds(w0, span), :], e_w.astype(BF16)) / jnp.sum(e_w, axis=0, keepdims=True)

    gate_scr[...] = _sigmoid(sm_ref[...]).T

    def gate_row(branch):
        base = SMALL_NGATE + branch * NSA_HEADS + hk * G
        return jnp.concatenate([gate_scr[pl.ds(base + g, 1), :] for g in range(G)], axis=1)

    o = gate_row(0) * o_c + gate_row(1) * o_s + gate_row(2) * o_w
    for g in range(G):
        o_ref[:, g * HD:(g + 1) * HD] = o[:, g * TQ:(g + 1) * TQ].T.astype(o_ref.dtype)


def _nsa(z3, zs3, cmp, mcs_t):
    bsz, s, _ = z3.shape
    hkv, hd, tq = NSA_KV_HEADS, NSA_HD, NSA_TQ
    qw = NSA_GROUP * hd
    nc = cmp.shape[3]
    nb = mcs_t.shape[0]
    kv_col = lambda i: (Z_NKV + i * hkv * hd) // hd

    def kv_spec(i):
        return pl.BlockSpec((None, s, hd), lambda b, h, q: (b, 0, kv_col(i) + h))

    return pl.pallas_call(
        _nsa_kernel,
        grid=(bsz, hkv, s // tq),
        in_specs=[
            pl.BlockSpec((None, tq, qw), lambda b, h, q: (b, q, Z_NQ // qw + h)),
            pl.BlockSpec((None, tq, LANES), lambda b, h, q: (b, q, 0)),
            pl.BlockSpec((None, None, None, nc, hd), lambda b, h, q: (0, b, h, 0, 0)),
            pl.BlockSpec((None, None, None, nc, hd), lambda b, h, q: (1, b, h, 0, 0)),
            kv_spec(2), kv_spec(3), kv_spec(4), kv_spec(5),
            pl.BlockSpec(mcs_t.shape, lambda b, h, q: (0, 0)),
        ],
        out_specs=pl.BlockSpec((None, tq, qw), lambda b, h, q: (b, q, h)),
        out_shape=jax.ShapeDtypeStruct((bsz, s, NSA_HEADS * hd), BF16),
        scratch_shapes=[pltpu.VMEM((nb, tq), F32), pltpu.VMEM((LANES, tq), F32),
                        pltpu.VMEM((NSA_TK, NSA_GROUP * tq), BF16)],
        compiler_params=_params(("parallel", "parallel", "arbitrary")),
        name="nsa_attn",
    )(z3, zs3, cmp, cmp, z3, z3, z3, z3, mcs_t)


def _merge_kernel(og_ref, on_ref, wg_ref, wn_ref, ga_ref, gb_ref, o_ref, wg_scr, wn_scr):
    @pl.when(pl.program_id(1) == 0)
    def _():
        wg_scr[...] = wg_ref[...].astype(BF16)
        wn_scr[...] = wn_ref[...].astype(BF16)

    a = _dot(og_ref[...], wg_scr[...])
    b = _dot(on_ref[...], wn_scr[...])
    ga = _sigmoid(ga_ref[...].astype(F32))
    gb = _sigmoid(gb_ref[...].astype(F32))
    o_ref[...] = (ga * a + gb * b).astype(o_ref.dtype)


def _merge(o_gla, o_nsa, w_pg, w_pn, z2, tm=512, tn=1024):
    m = o_gla.shape[0]
    once = pl.Buffered(1)
    return pl.pallas_call(
        _merge_kernel,
        grid=(D_MODEL // tn, m // tm),
        in_specs=[
            pl.BlockSpec((tm, GLA_DV), lambda j, i: (i, 0)),
            pl.BlockSpec((tm, NSA_HEADS * NSA_HD), lambda j, i: (i, 0)),
            pl.BlockSpec((GLA_DV, tn), lambda j, i: (0, j), pipeline_mode=once),
            pl.BlockSpec((NSA_HEADS * NSA_HD, tn), lambda j, i: (0, j), pipeline_mode=once),
            pl.BlockSpec((tm, tn), lambda j, i: (i, Z_MG // tn + j)),
            pl.BlockSpec((tm, tn), lambda j, i: (i, (Z_MG + D_MODEL) // tn + j)),
        ],
        out_specs=pl.BlockSpec((tm, tn), lambda j, i: (i, j)),
        out_shape=jax.ShapeDtypeStruct((m, D_MODEL), BF16),
        scratch_shapes=[pltpu.VMEM((GLA_DV, tn), BF16), pltpu.VMEM((NSA_HEADS * NSA_HD, tn), BF16)],
        compiler_params=_params(("parallel", "arbitrary")),
        name="merge_proj",
    )(o_gla, o_nsa, w_pg, w_pn, z2, z2)


def _out_proj_kernel(mg_ref, w_ref, x_ref, g_ref, x1_ref, h_ref):
    x1 = x_ref[...] + _dot(mg_ref[...], w_ref[...])
    x1_ref[...] = x1
    ms = jnp.mean(x1 * x1, axis=-1, keepdims=True)
    h_ref[...] = (x1 * lax.rsqrt(ms + RMS_EPS) * g_ref[...]).astype(h_ref.dtype)


def _out_proj(merged, w_out, x2, g_ffn, tm=512):
    m = merged.shape[0]
    return pl.pallas_call(
        _out_proj_kernel,
        grid=(m // tm,),
        in_specs=[
            pl.BlockSpec((tm, D_MODEL), lambda i: (i, 0)),
            pl.BlockSpec((D_MODEL, D_MODEL), lambda i: (0, 0)),
            pl.BlockSpec((tm, D_MODEL), lambda i: (i, 0)),
            pl.BlockSpec((1, D_MODEL), lambda i: (0, 0)),
        ],
        out_specs=[
            pl.BlockSpec((tm, D_MODEL), lambda i: (i, 0)),
            pl.BlockSpec((tm, D_MODEL), lambda i: (i, 0)),
        ],
        out_shape=[jax.ShapeDtypeStruct((m, D_MODEL), F32),
                   jax.ShapeDtypeStruct((m, D_MODEL), BF16)],
        compiler_params=_params(("parallel",)),
        name="out_proj",
    )(merged, w_out, x2, g_ffn)


def _ffn_up_kernel(h_ref, wg_ref, wu_ref, o_ref, wg_scr, wu_scr):
    @pl.when(pl.program_id(1) == 0)
    def _():
        wg_scr[...] = wg_ref[...].astype(BF16)
        wu_scr[...] = wu_ref[...].astype(BF16)

    h = h_ref[...]
    g = _dot(h, wg_scr[...])
    u = _dot(h, wu_scr[...])
    o_ref[...] = (g * _sigmoid(g) * u).astype(o_ref.dtype)


def _ffn_up(h2, w_gate, w_up, tm=2048, tf=512):
    m = h2.shape[0]
    return pl.pallas_call(
        _ffn_up_kernel,
        grid=(D_FF // tf, m // tm),
        in_specs=[
            pl.BlockSpec((tm, D_MODEL), lambda j, i: (i, 0)),
            pl.BlockSpec((D_MODEL, tf), lambda j, i: (0, j)),
            pl.BlockSpec((D_MODEL, tf), lambda j, i: (0, j)),
        ],
        out_specs=pl.BlockSpec((tm, tf), lambda j, i: (i, j)),
        out_shape=jax.ShapeDtypeStruct((m, D_FF), BF16),
        scratch_shapes=[pltpu.VMEM((D_MODEL, tf), BF16), pltpu.VMEM((D_MODEL, tf), BF16)],
        compiler_params=_params(("parallel", "arbitrary")),
        name="ffn_up",
    )(h2, w_gate, w_up)


def _ffn_down_kernel(a_ref, w_ref, x1_ref, g_ref, o_ref, *, final_norm):
    k = pl.program_id(1)

    @pl.when(k == 0)
    def _():
        o_ref[...] = x1_ref[...]

    o_ref[...] += _dot(a_ref[...], w_ref[...])

    if final_norm:
        @pl.when(k == pl.num_programs(1) - 1)
        def _():
            y = o_ref[...]
            ms = jnp.mean(y * y, axis=-1, keepdims=True)
            o_ref[...] = y * lax.rsqrt(ms + RMS_EPS) * g_ref[...]


def _ffn_down(act, w_down, x1, g_final, final_norm, tm=1024, tk=1408):
    m = act.shape[0]
    return pl.pallas_call(
        functools.partial(_ffn_down_kernel, final_norm=final_norm),
        grid=(m // tm, D_FF // tk),
        in_specs=[
            pl.BlockSpec((tm, tk), lambda i, k: (i, k)),
            pl.BlockSpec((tk, D_MODEL), lambda i, k: (k, 0)),
            pl.BlockSpec((tm, D_MODEL), lambda i, k: (i, 0)),
            pl.BlockSpec((1, D_MODEL), lambda i, k: (0, 0)),
        ],
        out_specs=pl.BlockSpec((tm, D_MODEL), lambda i, k: (i, 0)),
        out_shape=jax.ShapeDtypeStruct((m, D_MODEL), F32),
        compiler_params=_params(("parallel", "arbitrary")),
        name="ffn_down",
    )(act, w_down, x1, g_final)


def _cmp_to_slc_weights_t(seq):
    n_cmp = seq // CMP_STRIDE - 1
    n_slc = seq // SLC_BLOCK
    c0 = np.arange(n_cmp)[:, None] * CMP_STRIDE
    s0 = np.arange(n_slc)[None, :] * SLC_BLOCK
    ov = np.clip(np.minimum(c0 + CMP_LEN, s0 + SLC_BLOCK) - np.maximum(c0, s0), 0, None)
    m_cs = np.zeros((n_cmp + 1, n_slc), np.float32)
    m_cs[:n_cmp] = ov / CMP_LEN
    return jnp.asarray(m_cs.T, BF16)


def kernel(x, g_mix, w_in, w_gla_gate, b_gla_gate, g_gla_out, pe_cmp_k, w_cmp_k1, w_cmp_k2,
           pe_cmp_v, w_cmp_v1, w_cmp_v2, w_proj_gla, w_proj_nsa, w_out, g_ffn,
           w_ffn_gate, w_ffn_up, w_ffn_down, g_final):
    bsz, seq, d = x.shape
    m = bsz * seq
    depth = w_in.shape[0]
    mcs_t = _cmp_to_slc_weights_t(seq)
    xc = x.reshape(m, d)
    for l in range(depth):
        w_t = jnp.swapaxes(w_in[l], 0, 1)
        h, zs = _norm_small(xc, g_mix[l][None], w_t)
        z2 = _in_proj(h, w_t)
        z3 = z2.reshape(bsz, seq, Z_SMALL)
        zs3 = zs.reshape(bsz, seq, LANES)

        w_gate_pad = jnp.zeros((LANES, GLA_DK), F32).at[SMALL_GLR:SMALL_GLR + GLA_GATE_RANK].set(w_gla_gate[l])
        o_gla = _gla(z3, zs3, w_gate_pad.astype(BF16), b_gla_gate[l][None], g_gla_out[l][None])

        cmp = _compress(z3, jnp.stack([pe_cmp_k[l], pe_cmp_v[l]]),
                        jnp.stack([w_cmp_k1[l], w_cmp_v1[l]]), jnp.stack([w_cmp_k2[l], w_cmp_v2[l]]))
        o_nsa = _nsa(z3, zs3, cmp, mcs_t)

        merged = _merge(o_gla.reshape(m, GLA_DV), o_nsa.reshape(m, NSA_HEADS * NSA_HD),
                        w_proj_gla[l], w_proj_nsa[l], z2)
        x1, h2 = _out_proj(merged, w_out[l].astype(BF16), xc, g_ffn[l][None])
        act = _ffn_up(h2, w_ffn_gate[l], w_ffn_up[l])
        xc = _ffn_down(act, w_ffn_down[l].astype(BF16), x1, g_final[None],
                       final_norm=(l == depth - 1))
    return xc.reshape(bsz, seq, d)
```

```python
import functools

import numpy as np
import jax
import jax.numpy as jnp
from jax import lax
from jax.experimental import pallas as pl
from jax.experimental.pallas import tpu as pltpu

F32 = jnp.float32
BF16 = jnp.bfloat16

D_MODEL = 2048
GLA_HEADS = 4
GLA_DK = D_MODEL // 2
GLA_DV = D_MODEL
GLA_HK = GLA_DK // GLA_HEADS
GLA_HV = GLA_DV // GLA_HEADS
GLA_GATE_RANK = 16
GLA_TAU = 16.0
GLA_CHUNK = 64

NSA_HEADS = 16
NSA_KV_HEADS = 4
NSA_GROUP = NSA_HEADS // NSA_KV_HEADS
NSA_HD = D_MODEL // NSA_HEADS
CMP_STRIDE = 16
CMP_LEN = 2 * CMP_STRIDE
CMP_HIDDEN = 256
SLC_BLOCK = 64
SLC_TOPN = 16
WINDOW = 512
D_FF = ((8 * D_MODEL // 3 + 255) // 256) * 256

IN_SPLIT_SIZES = (GLA_DK, GLA_DK, GLA_DV, GLA_DV, GLA_GATE_RANK,
                  NSA_HEADS * NSA_HD, 6 * NSA_KV_HEADS * NSA_HD, 3 * NSA_HEADS,
                  2 * D_MODEL)

NEG_INF = -1e30
RMS_EPS = 1e-6

Z_GQ = 0
Z_GK = Z_GQ + GLA_DK
Z_GV = Z_GK + GLA_DK
Z_GR = Z_GV + GLA_DV
Z_NQ = Z_GR + GLA_DV
Z_NKV = Z_NQ + NSA_HEADS * NSA_HD
Z_MG = Z_NKV + 6 * NSA_KV_HEADS * NSA_HD
Z_SMALL = Z_MG + 2 * D_MODEL
LANES = 128
SMALL_GLR = 0
SMALL_NGATE = GLA_GATE_RANK

VMEM_LIMIT = 56 * 1024 * 1024


def _dot(a, b):
    return jnp.dot(a, b, preferred_element_type=F32)


def _dot_nt(a, b):
    return lax.dot_general(a, b, (((1,), (1,)), ((), ())), preferred_element_type=F32)


def _dot_tn(a, b):
    return lax.dot_general(a, b, (((0,), (0,)), ((), ())), preferred_element_type=F32)


def _sigmoid(x):
    return 1.0 / (1.0 + jnp.exp(-x))


def _split2(x):
    hi = x.astype(BF16)
    lo = (x - hi.astype(F32)).astype(BF16)
    return hi, lo


def _split3(x):
    hi = x.astype(BF16)
    r1 = x - hi.astype(F32)
    mid = r1.astype(BF16)
    lo = (r1 - mid.astype(F32)).astype(BF16)
    return hi, mid, lo


def _params(sem, **kw):
    return pltpu.CompilerParams(dimension_semantics=sem, vmem_limit_bytes=VMEM_LIMIT, **kw)


_IN_OFFS = np.cumsum((0,) + IN_SPLIT_SIZES)
COL_GLR = int(_IN_OFFS[4])
COL_NGATE = int(_IN_OFFS[7])
assert COL_GLR % LANES == SMALL_GLR and COL_NGATE % LANES == SMALL_NGATE
assert SMALL_GLR + GLA_GATE_RANK == SMALL_NGATE and SMALL_NGATE + 3 * NSA_HEADS <= LANES


def _norm_kernel(x_ref, g_ref, wa_ref, wb_ref, h_ref, zs_ref):
    xf = x_ref[...]
    ms = jnp.mean(xf * xf, axis=-1, keepdims=True)
    h = (xf * lax.rsqrt(ms + RMS_EPS) * g_ref[...]).astype(BF16)
    h_ref[...] = h
    row = lax.broadcasted_iota(jnp.int32, wa_ref.shape, 0)
    w_small = jnp.where(row < SMALL_NGATE, wa_ref[...],
                        jnp.where(row < SMALL_NGATE + 3 * NSA_HEADS, wb_ref[...], 0.0))
    zs_ref[...] = _dot_nt(h, w_small.astype(BF16))


def _norm_small(x2, g, w_t, tm=512):
    m = x2.shape[0]
    return pl.pallas_call(
        _norm_kernel,
        grid=(m // tm,),
        in_specs=[
            pl.BlockSpec((tm, D_MODEL), lambda i: (i, 0)),
            pl.BlockSpec((1, D_MODEL), lambda i: (0, 0)),
            pl.BlockSpec((LANES, D_MODEL), lambda i: (COL_GLR // LANES, 0)),
            pl.BlockSpec((LANES, D_MODEL), lambda i: (COL_NGATE // LANES, 0)),
        ],
        out_specs=[pl.BlockSpec((tm, D_MODEL), lambda i: (i, 0)),
                   pl.BlockSpec((tm, LANES), lambda i: (i, 0))],
        out_shape=[jax.ShapeDtypeStruct((m, D_MODEL), BF16),
                   jax.ShapeDtypeStruct((m, LANES), F32)],
        compiler_params=_params(("parallel",)),
        name="norm_small",
    )(x2, g, w_t, w_t)


IN_TN = 1024
IN_SHIFT_B = GLA_GATE_RANK
IN_SHIFT_C = GLA_GATE_RANK + 3 * NSA_HEADS
IN_BLOCKS_A = Z_NQ // IN_TN
IN_BLOCKS_AB = Z_MG // IN_TN
IN_ROWS = 48
assert (IN_TN - IN_SHIFT_B) % IN_ROWS == 0 and (IN_TN - IN_SHIFT_C) % IN_ROWS == 0


def _in_proj_kernel(h_ref, wm_ref, we_ref, z_ref, w_scr):
    j = pl.program_id(0)

    def build(shift):
        if shift == 0:
            w_scr[...] = wm_ref[...].astype(BF16)
            return

        def body(r, carry):
            dst = pl.ds(pl.multiple_of(r * IN_ROWS, 16), IN_ROWS)
            src = pl.ds(pl.multiple_of(r * IN_ROWS + shift, 8), IN_ROWS)
            w_scr[dst, :] = wm_ref[src, :].astype(BF16)
            return carry
        lax.fori_loop(0, (IN_TN - shift) // IN_ROWS, body, 0)
        w_scr[IN_TN - shift:, :] = we_ref[:shift, :].astype(BF16)

    first = pl.program_id(1) == 0

    @pl.when(first & (j < IN_BLOCKS_A))
    def _():
        build(0)

    @pl.when(first & (j >= IN_BLOCKS_A) & (j < IN_BLOCKS_AB))
    def _():
        build(IN_SHIFT_B)

    @pl.when(first & (j >= IN_BLOCKS_AB))
    def _():
        build(IN_SHIFT_C)

    z_ref[...] = _dot_nt(h_ref[...], w_scr[...]).astype(z_ref.dtype)


def _in_proj(h, w_t, tm=1024):
    m = h.shape[0]
    tn = IN_TN
    return pl.pallas_call(
        _in_proj_kernel,
        grid=(Z_SMALL // tn, m // tm),
        in_specs=[
            pl.BlockSpec((tm, D_MODEL), lambda j, i: (i, 0)),
            pl.BlockSpec((tn, D_MODEL), lambda j, i: (j, 0)),
            pl.BlockSpec((LANES, D_MODEL), lambda j, i: ((j + 1) * (tn // LANES), 0)),
        ],
        out_specs=pl.BlockSpec((tm, tn), lambda j, i: (i, j)),
        out_shape=jax.ShapeDtypeStruct((m, Z_SMALL), BF16),
        scratch_shapes=[pltpu.VMEM((tn, D_MODEL), BF16)],
        compiler_params=_params(("parallel", "arbitrary")),
        name="in_proj",
    )(h, w_t, w_t)


GLA_HPS = 2


def _gla_kernel(q_ref, k_ref, v_ref, r_ref, sm_ref, wg_ref, bg_ref, gn_ref, o_ref,
                state_ref, *, n_chunks):
    @pl.when(pl.program_id(2) == 0)
    def _():
        state_ref[...] = jnp.zeros_like(state_ref)

    C = GLA_CHUNK
    HK, HV = GLA_HK, GLA_HV
    ts = n_chunks * C
    row = lax.broadcasted_iota(jnp.int32, (ts, ts), 0)
    col = lax.broadcasted_iota(jnp.int32, (ts, ts), 1)
    tri = jnp.where((row >= col) & ((row // C) == (col // C)), 1.0, 0.0).astype(BF16)
    crow = lax.broadcasted_iota(jnp.int32, (C, C), 0)
    ccol = lax.broadcasted_iota(jnp.int32, (C, C), 1)
    causal = crow >= ccol

    u = _dot(sm_ref[...].astype(BF16), wg_ref[...]) + bg_ref[...]
    log_a = (jnp.minimum(u, 0.0) - jnp.log(1.0 + jnp.exp(-jnp.abs(u)))) / GLA_TAU
    hi, lo = _split2(log_a)
    b_all = _dot(tri, hi) + _dot(tri, lo)
    gn = gn_ref[...]

    heads = range(GLA_HPS)
    kcol = [slice(j * HK, (j + 1) * HK) for j in heads]
    vcol = [slice(j * HV, (j + 1) * HV) for j in heads]
    b = [b_all[:, kcol[j]] for j in heads]
    k = [k_ref[:, kcol[j]].astype(F32) for j in heads]
    q_dec = [(q_ref[:, kcol[j]].astype(F32) * (HK ** -0.5) * jnp.exp(b[j])).astype(BF16) for j in heads]
    k_dec = [(k[j] * jnp.exp(-b[j])).astype(BF16) for j in heads]
    state = [state_ref[j] for j in heads]
    for c in range(n_chunks):
        rows = slice(c * C, (c + 1) * C)
        for j in heads:
            b_c = b[j][rows]
            b_last = b_c[C - 1:C, :]
            k_end = (k[j][rows] * jnp.exp(b_last - b_c)).astype(BF16)
            v_c = v_ref[rows, vcol[j]]
            attn = jnp.where(causal, _dot_nt(q_dec[j][rows], k_dec[j][rows]), 0.0).astype(BF16)
            o = _dot(attn, v_c) + _dot_nt(q_dec[j][rows], state[j].astype(BF16))
            state[j] = state[j] * jnp.exp(b_last) + _dot_tn(v_c, k_end)
            ms = jnp.mean(o * o, axis=-1, keepdims=True)
            y = o * lax.rsqrt(ms + RMS_EPS) * gn
            r = r_ref[rows, vcol[j]].astype(F32)
            o_ref[rows, vcol[j]] = (y * (r * _sigmoid(r))).astype(o_ref.dtype)
    for j in heads:
        state_ref[j] = state[j]


def _gla(z3, zs3, w_gate_pad, b_gate, g_out, ts=512):
    bsz, s, _ = z3.shape
    qb, vb = GLA_HPS * GLA_HK, GLA_HPS * GLA_HV
    return pl.pallas_call(
        functools.partial(_gla_kernel, n_chunks=ts // GLA_CHUNK),
        grid=(bsz, GLA_HEADS // GLA_HPS, s // ts),
        in_specs=[
            pl.BlockSpec((None, ts, qb), lambda b, h, i: (b, i, Z_GQ // qb + h)),
            pl.BlockSpec((None, ts, qb), lambda b, h, i: (b, i, Z_GK // qb + h)),
            pl.BlockSpec((None, ts, vb), lambda b, h, i: (b, i, Z_GV // vb + h)),
            pl.BlockSpec((None, ts, vb), lambda b, h, i: (b, i, Z_GR // vb + h)),
            pl.BlockSpec((None, ts, LANES), lambda b, h, i: (b, i, 0)),
            pl.BlockSpec((LANES, qb), lambda b, h, i: (0, h)),
            pl.BlockSpec((1, qb), lambda b, h, i: (0, h)),
            pl.BlockSpec((1, GLA_HV), lambda b, h, i: (0, 0)),
        ],
        out_specs=pl.BlockSpec((None, ts, vb), lambda b, h, i: (b, i, h)),
        out_shape=jax.ShapeDtypeStruct((bsz, s, GLA_DV), BF16),
        scratch_shapes=[pltpu.VMEM((GLA_HPS, GLA_HV, GLA_HK), F32)],
        compiler_params=_params(("parallel", "parallel", "arbitrary")),
        name="gla",
    )(z3, z3, z3, z3, zs3, w_gate_pad, b_gate, g_out)


def _cmp_kernel(t_ref, pe_ref, w1_ref, w2_ref, o_ref, tf_scr, lo_scr, hi_scr):
    hd = NSA_HD
    half = CMP_STRIDE * hd
    nchunk = lo_scr.shape[0]
    tf_scr[...] = t_ref[...].astype(F32)
    for r in range(CMP_STRIDE):
        t_r = tf_scr[pl.ds(r, nchunk, stride=CMP_STRIDE), :]
        lo_scr[:, r * hd:(r + 1) * hd] = (t_r + pe_ref[r:r + 1, :]).astype(BF16)
        hi_scr[:, r * hd:(r + 1) * hd] = (t_r + pe_ref[CMP_STRIDE + r:CMP_STRIDE + r + 1, :]).astype(BF16)
    a = _dot(lo_scr[...], w1_ref[:half, :].astype(BF16))
    bm = _dot(hi_scr[...], w1_ref[half:, :].astype(BF16))
    pre = a + pltpu.roll(bm, nchunk - 1, 0)
    hid = (pre * _sigmoid(pre)).astype(BF16)
    o_ref[...] = _dot(hid, w2_ref[...].astype(BF16)).astype(o_ref.dtype)


def _compress(z3, pe, w1, w2):
    bsz, s, _ = z3.shape
    hkv, hd = NSA_KV_HEADS, NSA_HD
    nchunk = s // CMP_STRIDE
    width = CMP_STRIDE * hd
    return pl.pallas_call(
        _cmp_kernel,
        grid=(2, bsz, hkv),
        in_specs=[
            pl.BlockSpec((None, s, hd), lambda kv, b, h: (b, 0, Z_NKV // hd + kv * hkv + h)),
            pl.BlockSpec((None, CMP_LEN, hd), lambda kv, b, h: (kv, 0, 0)),
            pl.BlockSpec((None, 2 * width, CMP_HIDDEN), lambda kv, b, h: (kv, 0, 0)),
            pl.BlockSpec((None, CMP_HIDDEN, hd), lambda kv, b, h: (kv, 0, 0)),
        ],
        out_specs=pl.BlockSpec((None, None, None, nchunk, hd), lambda kv, b, h: (kv, b, h, 0, 0)),
        out_shape=jax.ShapeDtypeStruct((2, bsz, hkv, nchunk, hd), BF16),
        scratch_shapes=[pltpu.VMEM((s, hd), F32),
                        pltpu.VMEM((nchunk, width), BF16),
                        pltpu.VMEM((nchunk, width), BF16)],
        compiler_params=_params(("parallel", "parallel", "parallel")),
        name="nsa_compress",
    )(z3, pe, w1, w2)


NSA_TQ = 256
NSA_TK = NSA_TQ
LOG2E = 1.4426950408889634


NSA_HPS = 2


def _nsa_kernel(q_ref, sm_ref, kc_ref, vc_ref, ks_ref, vs_ref, kw_ref, vw_ref, mcs_ref,
                o_ref, sel_scr, gate_scr, p_scr):
    G, TQ, TK, HD = NSA_GROUP, NSA_TQ, NSA_TK, NSA_HD
    R = G * TQ
    heads = range(NSA_HPS)
    hk0 = pl.program_id(1) * NSA_HPS
    qi = pl.program_id(2)
    qs = qi * TQ

    def lanes4(x):
        return jnp.concatenate([x] * G, axis=1)

    def hcol(j):
        return slice(j * HD, (j + 1) * HD)

    q4 = []
    for j in heads:
        qj = jnp.concatenate([q_ref[:, (j * G + g) * HD:(j * G + g + 1) * HD] for g in range(G)], axis=0)
        q4.append((qj.astype(F32) * (HD ** -0.5 * LOG2E)).astype(BF16))

    nc = kc_ref.shape[1]
    tq_c = qs + lax.broadcasted_iota(jnp.int32, (nc, TQ), 1)
    cmp_end = lax.broadcasted_iota(jnp.int32, (nc, TQ), 0) * CMP_STRIDE + (CMP_LEN - 1)
    valid_c = lanes4(jnp.where(cmp_end <= tq_c, 1.0, 0.0)) > 0.5
    mcs = mcs_ref[...]
    nb = mcs.shape[0]
    blk = lax.broadcasted_iota(jnp.int32, (nb, TQ), 0)
    tq_b = qs + lax.broadcasted_iota(jnp.int32, (nb, TQ), 1)
    cur = tq_b // SLC_BLOCK
    forced = (blk == 0) | (blk == cur) | (blk == cur - 1)
    visible = blk * SLC_BLOCK <= tq_b
    o_c = []
    for j in heads:
        s_c = jnp.where(valid_c, _dot_nt(kc_ref[j], q4[j]), NEG_INF)
        e_c = jnp.where(valid_c, jnp.exp2(s_c - jnp.max(s_c, axis=0, keepdims=True)), 0.0)
        l_c = jnp.sum(e_c, axis=0, keepdims=True)
        p_c = e_c / jnp.where(l_c > 0.0, l_c, 1.0)
        o_c.append(_dot_tn(vc_ref[j], p_c.astype(BF16)))

        p_sum = p_c[:, 0:TQ]
        for g in range(1, G):
            p_sum = p_sum + p_c[:, g * TQ:(g + 1) * TQ]
        imp = sum(_dot(mcs, part) for part in _split3(p_sum))
        imp = jnp.where(forced, jnp.inf, imp)
        imp = jnp.where(visible, imp, -jnp.inf)
        rank = jnp.zeros((nb, TQ), F32)
        for kb in range(nb):
            rk = imp[kb:kb + 1, :]
            ahead = jnp.where(rk > imp, 1.0, jnp.where(rk == imp, jnp.where(blk > kb, 1.0, 0.0), 0.0))
            rank = rank + ahead
        sel_scr[j] = jnp.where(rank < float(min(SLC_TOPN, nb)), 0.0, NEG_INF)

    blocks_per_tile = TK // SLC_BLOCK

    def sel_bias(j, tile):
        return jnp.concatenate(
            [jnp.broadcast_to(sel_scr[j, pl.ds(tile * blocks_per_tile + i, 1), :], (SLC_BLOCK, TQ))
             for i in range(blocks_per_tile)], axis=0)

    krow = lax.broadcasted_iota(jnp.int32, (TK, TQ), 0)
    qcol = lax.broadcasted_iota(jnp.int32, (TK, TQ), 1)
    diag = pl.ds(pl.multiple_of(qs, TK), TK)
    m_d, l_d = [], []
    for j in heads:
        bias_d = jnp.where(krow <= qcol, sel_bias(j, qi), NEG_INF)
        s_d = _dot_nt(ks_ref[diag, hcol(j)], q4[j]) + lanes4(bias_d)
        m = jnp.max(s_d, axis=0, keepdims=True)
        p_d = jnp.exp2(s_d - m)
        m_d.append(m)
        l_d.append(jnp.sum(p_d, axis=0, keepdims=True))
        p_scr[j] = p_d.astype(BF16)

    def kv_step(t, carry):
        ms, ls, accs, voff = carry
        prev = pl.ds(pl.multiple_of(voff, TK), TK)
        koff = pl.multiple_of(t * TK, TK)
        ms_n, ls_n, accs_n = [], [], []
        for j in heads:
            pv = _dot_tn(vs_ref[prev, hcol(j)], p_scr[j])
            s = _dot_nt(ks_ref[pl.ds(koff, TK), hcol(j)], q4[j]) + lanes4(sel_bias(j, t))
            m_new = jnp.maximum(ms[j], jnp.max(s, axis=0, keepdims=True))
            alpha = jnp.exp2(ms[j] - m_new)
            p = jnp.exp2(s - m_new)
            ms_n.append(m_new)
            ls_n.append(alpha * ls[j] + jnp.sum(p, axis=0, keepdims=True))
            accs_n.append(alpha * (accs[j] + pv))
            p_scr[j] = p.astype(BF16)
        return tuple(ms_n), tuple(ls_n), tuple(accs_n), koff

    acc0 = tuple(jnp.zeros((HD, R), F32) for _ in heads)
    _, l_s, acc_s, voff = lax.fori_loop(0, qi, kv_step, (tuple(m_d), tuple(l_d), acc0, qs))
    last = pl.ds(pl.multiple_of(voff, TK), TK)
    o_s = [(acc_s[j] + _dot_tn(vs_ref[last, hcol(j)], p_scr[j])) / l_s[j] for j in heads]

    span = WINDOW + TQ
    w0 = pl.multiple_of(jnp.maximum(qs - WINDOW, 0), TQ)
    kpos = w0 + lax.broadcasted_iota(jnp.int32, (span, TQ), 0)
    tq_w = qs + lax.broadcasted_iota(jnp.int32, (span, TQ), 1)
    bias_w = lanes4(jnp.where(kpos <= tq_w, jnp.where(kpos > tq_w - WINDOW, 0.0, NEG_INF), NEG_INF))
    o_w = []
    for j in heads:
        s_w = _dot_nt(kw_ref[pl.ds(w0, span), hcol(j)], q4[j]) + bias_w
        e_w = jnp.exp2(s_w - jnp.max(s_w, axis=0, keepdims=True))
        o_w.append(_dot_tn(vw_ref[pl.ds(w0, span), hcol(j)], e_w.astype(BF16))
                   / jnp.sum(e_w, axis=0, keepdims=True))

    gate_scr[...] = _sigmoid(sm_ref[...]).T
    for j in heads:
        def gate_row(branch):
            base = SMALL_NGATE + branch * NSA_HEADS + (hk0 + j) * G
            return jnp.concatenate([gate_scr[pl.ds(base + g, 1), :] for g in range(G)], axis=1)

        o = gate_row(0) * o_c[j] + gate_row(1) * o_s[j] + gate_row(2) * o_w[j]
        for g in range(G):
            o_ref[:, (j * G + g) * HD:(j * G + g + 1) * HD] = o[:, g * TQ:(g + 1) * TQ].T.astype(o_ref.dtype)


def _nsa(z3, zs3, cmp, mcs_t):
    bsz, s, _ = z3.shape
    hkv, hd, tq, hps = NSA_KV_HEADS, NSA_HD, NSA_TQ, NSA_HPS
    qw = hps * NSA_GROUP * hd
    kw = hps * hd
    nc = cmp.shape[3]
    nb = mcs_t.shape[0]

    def kv_spec(i):
        return pl.BlockSpec((None, s, kw), lambda b, h, q: (b, 0, (Z_NKV + i * hkv * hd) // kw + h))

    return pl.pallas_call(
        _nsa_kernel,
        grid=(bsz, hkv // hps, s // tq),
        in_specs=[
            pl.BlockSpec((None, tq, qw), lambda b, h, q: (b, q, Z_NQ // qw + h)),
            pl.BlockSpec((None, tq, LANES), lambda b, h, q: (b, q, 0)),
            pl.BlockSpec((None, None, hps, nc, hd), lambda b, h, q: (0, b, h, 0, 0)),
            pl.BlockSpec((None, None, hps, nc, hd), lambda b, h, q: (1, b, h, 0, 0)),
            kv_spec(2), kv_spec(3), kv_spec(4), kv_spec(5),
            pl.BlockSpec(mcs_t.shape, lambda b, h, q: (0, 0)),
        ],
        out_specs=pl.BlockSpec((None, tq, qw), lambda b, h, q: (b, q, h)),
        out_shape=jax.ShapeDtypeStruct((bsz, s, NSA_HEADS * hd), BF16),
        scratch_shapes=[pltpu.VMEM((hps, nb, tq), F32), pltpu.VMEM((LANES, tq), F32),
                        pltpu.VMEM((hps, NSA_TK, NSA_GROUP * tq), BF16)],
        compiler_params=_params(("parallel", "parallel", "arbitrary")),
        name="nsa_attn",
    )(z3, zs3, cmp, cmp, z3, z3, z3, z3, mcs_t)


def _merge_kernel(og_ref, on_ref, wg_ref, wn_ref, ga_ref, gb_ref, o_ref, wg_scr, wn_scr):
    @pl.when(pl.program_id(1) == 0)
    def _():
        wg_scr[...] = wg_ref[...].astype(BF16)
        wn_scr[...] = wn_ref[...].astype(BF16)

    a = _dot(og_ref[...], wg_scr[...])
    b = _dot(on_ref[...], wn_scr[...])
    ga = _sigmoid(ga_ref[...].astype(F32))
    gb = _sigmoid(gb_ref[...].astype(F32))
    o_ref[...] = (ga * a + gb * b).astype(o_ref.dtype)


def _merge(o_gla, o_nsa, w_pg, w_pn, z2, tm=512, tn=1024):
    m = o_gla.shape[0]
    once = pl.Buffered(1)
    return pl.pallas_call(
        _merge_kernel,
        grid=(D_MODEL // tn, m // tm),
        in_specs=[
            pl.BlockSpec((tm, GLA_DV), lambda j, i: (i, 0)),
            pl.BlockSpec((tm, NSA_HEADS * NSA_HD), lambda j, i: (i, 0)),
            pl.BlockSpec((GLA_DV, tn), lambda j, i: (0, j), pipeline_mode=once),
            pl.BlockSpec((NSA_HEADS * NSA_HD, tn), lambda j, i: (0, j), pipeline_mode=once),
            pl.BlockSpec((tm, tn), lambda j, i: (i, Z_MG // tn + j)),
            pl.BlockSpec((tm, tn), lambda j, i: (i, (Z_MG + D_MODEL) // tn + j)),
        ],
        out_specs=pl.BlockSpec((tm, tn), lambda j, i: (i, j)),
        out_shape=jax.ShapeDtypeStruct((m, D_MODEL), BF16),
        scratch_shapes=[pltpu.VMEM((GLA_DV, tn), BF16), pltpu.VMEM((NSA_HEADS * NSA_HD, tn), BF16)],
        compiler_params=_params(("parallel", "arbitrary")),
        name="merge_proj",
    )(o_gla, o_nsa, w_pg, w_pn, z2, z2)


def _out_proj_kernel(mg_ref, w_ref, x_ref, g_ref, x1_ref, h_ref):
    x1 = x_ref[...] + _dot(mg_ref[...], w_ref[...])
    x1_ref[...] = x1
    ms = jnp.mean(x1 * x1, axis=-1, keepdims=True)
    h_ref[...] = (x1 * lax.rsqrt(ms + RMS_EPS) * g_ref[...]).astype(h_ref.dtype)


def _out_proj(merged, w_out, x2, g_ffn, tm=512):
    m = merged.shape[0]
    return pl.pallas_call(
        _out_proj_kernel,
        grid=(m // tm,),
        in_specs=[
            pl.BlockSpec((tm, D_MODEL), lambda i: (i, 0)),
            pl.BlockSpec((D_MODEL, D_MODEL), lambda i: (0, 0)),
            pl.BlockSpec((tm, D_MODEL), lambda i: (i, 0)),
            pl.BlockSpec((1, D_MODEL), lambda i: (0, 0)),
        ],
        out_specs=[
            pl.BlockSpec((tm, D_MODEL), lambda i: (i, 0)),
            pl.BlockSpec((tm, D_MODEL), lambda i: (i, 0)),
        ],
        out_shape=[jax.ShapeDtypeStruct((m, D_MODEL), F32),
                   jax.ShapeDtypeStruct((m, D_MODEL), BF16)],
        compiler_params=_params(("parallel",)),
        name="out_proj",
    )(merged, w_out, x2, g_ffn)


def _ffn_up_kernel(h_ref, wg_ref, wu_ref, o_ref, wg_scr, wu_scr):
    @pl.when(pl.program_id(1) == 0)
    def _():
        wg_scr[...] = wg_ref[...].astype(BF16)
        wu_scr[...] = wu_ref[...].astype(BF16)

    h = h_ref[...]
    g = _dot(h, wg_scr[...])
    u = _dot(h, wu_scr[...])
    o_ref[...] = (g * _sigmoid(g) * u).astype(o_ref.dtype)


def _ffn_up(h2, w_gate, w_up, tm=2048, tf=512):
    m = h2.shape[0]
    return pl.pallas_call(
        _ffn_up_kernel,
        grid=(D_FF // tf, m // tm),
        in_specs=[
            pl.BlockSpec((tm, D_MODEL), lambda j, i: (i, 0)),
            pl.BlockSpec((D_MODEL, tf), lambda j, i: (0, j)),
            pl.BlockSpec((D_MODEL, tf), lambda j, i: (0, j)),
        ],
        out_specs=pl.BlockSpec((tm, tf), lambda j, i: (i, j)),
        out_shape=jax.ShapeDtypeStruct((m, D_FF), BF16),
        scratch_shapes=[pltpu.VMEM((D_MODEL, tf), BF16), pltpu.VMEM((D_MODEL, tf), BF16)],
        compiler_params=_params(("parallel", "arbitrary")),
        name="ffn_up",
    )(h2, w_gate, w_up)


def _ffn_down_kernel(a_ref, w_ref, x1_ref, g_ref, o_ref, *, final_norm):
    k = pl.program_id(1)

    @pl.when(k == 0)
    def _():
        o_ref[...] = x1_ref[...]

    o_ref[...] += _dot(a_ref[...], w_ref[...])

    if final_norm:
        @pl.when(k == pl.num_programs(1) - 1)
        def _():
            y = o_ref[...]
            ms = jnp.mean(y * y, axis=-1, keepdims=True)
            o_ref[...] = y * lax.rsqrt(ms + RMS_EPS) * g_ref[...]


def _ffn_down(act, w_down, x1, g_final, final_norm, tm=1024, tk=1408):
    m = act.shape[0]
    return pl.pallas_call(
        functools.partial(_ffn_down_kernel, final_norm=final_norm),
        grid=(m // tm, D_FF // tk),
        in_specs=[
            pl.BlockSpec((tm, tk), lambda i, k: (i, k)),
            pl.BlockSpec((tk, D_MODEL), lambda i, k: (k, 0)),
            pl.BlockSpec((tm, D_MODEL), lambda i, k: (i, 0)),
            pl.BlockSpec((1, D_MODEL), lambda i, k: (0, 0)),
        ],
        out_specs=pl.BlockSpec((tm, D_MODEL), lambda i, k: (i, 0)),
        out_shape=jax.ShapeDtypeStruct((m, D_MODEL), F32),
        compiler_params=_params(("parallel", "arbitrary")),
        name="ffn_down",
    )(act, w_down, x1, g_final)


def _cmp_to_slc_weights_t(seq):
    n_cmp = seq // CMP_STRIDE - 1
    n_slc = seq // SLC_BLOCK
    c0 = np.arange(n_cmp)[:, None] * CMP_STRIDE
    s0 = np.arange(n_slc)[None, :] * SLC_BLOCK
    ov = np.clip(np.minimum(c0 + CMP_LEN, s0 + SLC_BLOCK) - np.maximum(c0, s0), 0, None)
    m_cs = np.zeros((n_cmp + 1, n_slc), np.float32)
    m_cs[:n_cmp] = ov / CMP_LEN
    return jnp.asarray(m_cs.T, BF16)


def kernel(x, g_mix, w_in, w_gla_gate, b_gla_gate, g_gla_out, pe_cmp_k, w_cmp_k1, w_cmp_k2,
           pe_cmp_v, w_cmp_v1, w_cmp_v2, w_proj_gla, w_proj_nsa, w_out, g_ffn,
           w_ffn_gate, w_ffn_up, w_ffn_down, g_final):
    bsz, seq, d = x.shape
    m = bsz * seq
    depth = w_in.shape[0]
    mcs_t = _cmp_to_slc_weights_t(seq)
    xc = x.reshape(m, d)
    for l in range(depth):
        w_t = jnp.swapaxes(w_in[l], 0, 1)
        h, zs = _norm_small(xc, g_mix[l][None], w_t)
        z2 = _in_proj(h, w_t)
        z3 = z2.reshape(bsz, seq, Z_SMALL)
        zs3 = zs.reshape(bsz, seq, LANES)

        w_gate_pad = jnp.zeros((LANES, GLA_DK), F32).at[SMALL_GLR:SMALL_GLR + GLA_GATE_RANK].set(w_gla_gate[l])
        o_gla = _gla(z3, zs3, w_gate_pad.astype(BF16), b_gla_gate[l][None], g_gla_out[l][None])

        cmp = _compress(z3, jnp.stack([pe_cmp_k[l], pe_cmp_v[l]]),
                        jnp.stack([w_cmp_k1[l], w_cmp_v1[l]]), jnp.stack([w_cmp_k2[l], w_cmp_v2[l]]))
        o_nsa = _nsa(z3, zs3, cmp, mcs_t)

        merged = _merge(o_gla.reshape(m, GLA_DV), o_nsa.reshape(m, NSA_HEADS * NSA_HD),
                        w_proj_gla[l], w_proj_nsa[l], z2)
        x1, h2 = _out_proj(merged, w_out[l].astype(BF16), xc, g_ffn[l][None])
        act = _ffn_up(h2, w_ffn_gate[l], w_ffn_up[l])
        xc = _ffn_down(act, w_ffn_down[l].astype(BF16), x1, g_final[None],
                       final_norm=(l == depth - 1))
    return xc.reshape(bsz, seq, d)
```

```python
import functools

import numpy as np
import jax
import jax.numpy as jnp
from jax import lax
from jax.experimental import pallas as pl
from jax.experimental.pallas import tpu as pltpu

F32 = jnp.float32
BF16 = jnp.bfloat16

D_MODEL = 2048
GLA_HEADS = 4
GLA_DK = D_MODEL // 2
GLA_DV = D_MODEL
GLA_HK = GLA_DK // GLA_HEADS
GLA_HV = GLA_DV // GLA_HEADS
GLA_GATE_RANK = 16
GLA_TAU = 16.0
GLA_CHUNK = 64

NSA_HEADS = 16
NSA_KV_HEADS = 4
NSA_GROUP = NSA_HEADS // NSA_KV_HEADS
NSA_HD = D_MODEL // NSA_HEADS
CMP_STRIDE = 16
CMP_LEN = 2 * CMP_STRIDE
CMP_HIDDEN = 256
SLC_BLOCK = 64
SLC_TOPN = 16
WINDOW = 512
D_FF = ((8 * D_MODEL // 3 + 255) // 256) * 256

IN_SPLIT_SIZES = (GLA_DK, GLA_DK, GLA_DV, GLA_DV, GLA_GATE_RANK,
                  NSA_HEADS * NSA_HD, 6 * NSA_KV_HEADS * NSA_HD, 3 * NSA_HEADS,
                  2 * D_MODEL)

NEG_INF = -1e30
RMS_EPS = 1e-6

Z_GQ = 0
Z_GK = Z_GQ + GLA_DK
Z_GV = Z_GK + GLA_DK
Z_GR = Z_GV + GLA_DV
Z_NQ = Z_GR + GLA_DV
Z_NKV = Z_NQ + NSA_HEADS * NSA_HD
Z_MG = Z_NKV + 6 * NSA_KV_HEADS * NSA_HD
Z_SMALL = Z_MG + 2 * D_MODEL
LANES = 128
SMALL_GLR = 0
SMALL_NGATE = GLA_GATE_RANK

VMEM_LIMIT = 56 * 1024 * 1024


def _dot(a, b):
    return jnp.dot(a, b, preferred_element_type=F32)


def _dot_nt(a, b):
    return lax.dot_general(a, b, (((1,), (1,)), ((), ())), preferred_element_type=F32)


def _dot_tn(a, b):
    return lax.dot_general(a, b, (((0,), (0,)), ((), ())), preferred_element_type=F32)


def _sigmoid(x):
    return 1.0 / (1.0 + jnp.exp(-x))


def _split2(x):
    hi = x.astype(BF16)
    lo = (x - hi.astype(F32)).astype(BF16)
    return hi, lo


def _split3(x):
    hi = x.astype(BF16)
    r1 = x - hi.astype(F32)
    mid = r1.astype(BF16)
    lo = (r1 - mid.astype(F32)).astype(BF16)
    return hi, mid, lo


def _params(sem, **kw):
    return pltpu.CompilerParams(dimension_semantics=sem, vmem_limit_bytes=VMEM_LIMIT, **kw)


_IN_OFFS = np.cumsum((0,) + IN_SPLIT_SIZES)
COL_GLR = int(_IN_OFFS[4])
COL_NGATE = int(_IN_OFFS[7])
assert COL_GLR % LANES == SMALL_GLR and COL_NGATE % LANES == SMALL_NGATE
assert SMALL_GLR + GLA_GATE_RANK == SMALL_NGATE and SMALL_NGATE + 3 * NSA_HEADS <= LANES


def _norm_kernel(x_ref, g_ref, wa_ref, wb_ref, h_ref, zs_ref):
    xf = x_ref[...]
    ms = jnp.mean(xf * xf, axis=-1, keepdims=True)
    h = (xf * lax.rsqrt(ms + RMS_EPS) * g_ref[...]).astype(BF16)
    h_ref[...] = h
    row = lax.broadcasted_iota(jnp.int32, wa_ref.shape, 0)
    w_small = jnp.where(row < SMALL_NGATE, wa_ref[...],
                        jnp.where(row < SMALL_NGATE + 3 * NSA_HEADS, wb_ref[...], 0.0))
    zs_ref[...] = _dot_nt(h, w_small.astype(BF16))


def _norm_small(x2, g, w_t, tm=512):
    m = x2.shape[0]
    return pl.pallas_call(
        _norm_kernel,
        grid=(m // tm,),
        in_specs=[
            pl.BlockSpec((tm, D_MODEL), lambda i: (i, 0)),
            pl.BlockSpec((1, D_MODEL), lambda i: (0, 0)),
            pl.BlockSpec((LANES, D_MODEL), lambda i: (COL_GLR // LANES, 0)),
            pl.BlockSpec((LANES, D_MODEL), lambda i: (COL_NGATE // LANES, 0)),
        ],
        out_specs=[pl.BlockSpec((tm, D_MODEL), lambda i: (i, 0)),
                   pl.BlockSpec((tm, LANES), lambda i: (i, 0))],
        out_shape=[jax.ShapeDtypeStruct((m, D_MODEL), BF16),
                   jax.ShapeDtypeStruct((m, LANES), F32)],
        compiler_params=_params(("parallel",)),
        name="norm_small",
    )(x2, g, w_t, w_t)


IN_TN = 1024
IN_SHIFT_B = GLA_GATE_RANK
IN_SHIFT_C = GLA_GATE_RANK + 3 * NSA_HEADS
IN_BLOCKS_A = Z_NQ // IN_TN
IN_BLOCKS_AB = Z_MG // IN_TN
IN_ROWS = 48
assert (IN_TN - IN_SHIFT_B) % IN_ROWS == 0 and (IN_TN - IN_SHIFT_C) % IN_ROWS == 0


def _in_proj_kernel(h_ref, wm_ref, we_ref, z_ref, w_scr):
    j = pl.program_id(0)

    def build(shift):
        if shift == 0:
            w_scr[...] = wm_ref[...].astype(BF16)
            return

        def body(r, carry):
            dst = pl.ds(pl.multiple_of(r * IN_ROWS, 16), IN_ROWS)
            src = pl.ds(pl.multiple_of(r * IN_ROWS + shift, 8), IN_ROWS)
            w_scr[dst, :] = wm_ref[src, :].astype(BF16)
            return carry
        lax.fori_loop(0, (IN_TN - shift) // IN_ROWS, body, 0)
        w_scr[IN_TN - shift:, :] = we_ref[:shift, :].astype(BF16)

    first = pl.program_id(1) == 0

    @pl.when(first & (j < IN_BLOCKS_A))
    def _():
        build(0)

    @pl.when(first & (j >= IN_BLOCKS_A) & (j < IN_BLOCKS_AB))
    def _():
        build(IN_SHIFT_B)

    @pl.when(first & (j >= IN_BLOCKS_AB))
    def _():
        build(IN_SHIFT_C)

    z_ref[...] = _dot_nt(h_ref[...], w_scr[...]).astype(z_ref.dtype)


def _in_proj(h, w_t, tm=1024):
    m = h.shape[0]
    tn = IN_TN
    return pl.pallas_call(
        _in_proj_kernel,
        grid=(Z_SMALL // tn, m // tm),
        in_specs=[
            pl.BlockSpec((tm, D_MODEL), lambda j, i: (i, 0)),
            pl.BlockSpec((tn, D_MODEL), lambda j, i: (j, 0)),
            pl.BlockSpec((LANES, D_MODEL), lambda j, i: ((j + 1) * (tn // LANES), 0)),
        ],
        out_specs=pl.BlockSpec((tm, tn), lambda j, i: (i, j)),
        out_shape=jax.ShapeDtypeStruct((m, Z_SMALL), BF16),
        scratch_shapes=[pltpu.VMEM((tn, D_MODEL), BF16)],
        compiler_params=_params(("parallel", "arbitrary")),
        name="in_proj",
    )(h, w_t, w_t)


GLA_HPS = 4


def _gla_kernel(q_ref, k_ref, v_ref, r_ref, sm_ref, wg_ref, bg_ref, gn_ref, o_ref,
                state_ref, *, n_chunks):
    @pl.when(pl.program_id(2) == 0)
    def _():
        state_ref[...] = jnp.zeros_like(state_ref)

    C = GLA_CHUNK
    HK, HV = GLA_HK, GLA_HV
    ts = n_chunks * C
    row = lax.broadcasted_iota(jnp.int32, (ts, ts), 0)
    col = lax.broadcasted_iota(jnp.int32, (ts, ts), 1)
    tri = jnp.where((row >= col) & ((row // C) == (col // C)), 1.0, 0.0).astype(BF16)
    crow = lax.broadcasted_iota(jnp.int32, (C, C), 0)
    ccol = lax.broadcasted_iota(jnp.int32, (C, C), 1)
    causal = crow >= ccol

    u = _dot(sm_ref[...].astype(BF16), wg_ref[...]) + bg_ref[...]
    log_a = (jnp.minimum(u, 0.0) - jnp.log(1.0 + jnp.exp(-jnp.abs(u)))) / GLA_TAU
    hi, lo = _split2(log_a)
    b_all = _dot(tri, hi) + _dot(tri, lo)
    gn = gn_ref[...]

    heads = range(GLA_HPS)
    kcol = [slice(j * HK, (j + 1) * HK) for j in heads]
    vcol = [slice(j * HV, (j + 1) * HV) for j in heads]
    b = [b_all[:, kcol[j]] for j in heads]
    k = [k_ref[:, kcol[j]].astype(F32) for j in heads]
    q_dec = [(q_ref[:, kcol[j]].astype(F32) * (HK ** -0.5) * jnp.exp(b[j])).astype(BF16) for j in heads]
    k_dec = [(k[j] * jnp.exp(-b[j])).astype(BF16) for j in heads]
    state = [state_ref[j] for j in heads]
    for c in range(n_chunks):
        rows = slice(c * C, (c + 1) * C)
        for j in heads:
            b_c = b[j][rows]
            b_last = b_c[C - 1:C, :]
            k_end = (k[j][rows] * jnp.exp(b_last - b_c)).astype(BF16)
            v_c = v_ref[rows, vcol[j]]
            attn = jnp.where(causal, _dot_nt(q_dec[j][rows], k_dec[j][rows]), 0.0).astype(BF16)
            o = _dot(attn, v_c) + _dot_nt(q_dec[j][rows], state[j].astype(BF16))
            state[j] = state[j] * jnp.exp(b_last) + _dot_tn(v_c, k_end)
            ms = jnp.mean(o * o, axis=-1, keepdims=True)
            y = o * lax.rsqrt(ms + RMS_EPS) * gn
            r = r_ref[rows, vcol[j]].astype(F32)
            o_ref[rows, vcol[j]] = (y * (r * _sigmoid(r))).astype(o_ref.dtype)
    for j in heads:
        state_ref[j] = state[j]


def _gla(z3, zs3, w_gate_pad, b_gate, g_out, ts=512):
    bsz, s, _ = z3.shape
    qb, vb = GLA_HPS * GLA_HK, GLA_HPS * GLA_HV
    return pl.pallas_call(
        functools.partial(_gla_kernel, n_chunks=ts // GLA_CHUNK),
        grid=(bsz, GLA_HEADS // GLA_HPS, s // ts),
        in_specs=[
            pl.BlockSpec((None, ts, qb), lambda b, h, i: (b, i, Z_GQ // qb + h)),
            pl.BlockSpec((None, ts, qb), lambda b, h, i: (b, i, Z_GK // qb + h)),
            pl.BlockSpec((None, ts, vb), lambda b, h, i: (b, i, Z_GV // vb + h)),
            pl.BlockSpec((None, ts, vb), lambda b, h, i: (b, i, Z_GR // vb + h)),
            pl.BlockSpec((None, ts, LANES), lambda b, h, i: (b, i, 0)),
            pl.BlockSpec((LANES, qb), lambda b, h, i: (0, h)),
            pl.BlockSpec((1, qb), lambda b, h, i: (0, h)),
            pl.BlockSpec((1, GLA_HV), lambda b, h, i: (0, 0)),
        ],
        out_specs=pl.BlockSpec((None, ts, vb), lambda b, h, i: (b, i, h)),
        out_shape=jax.ShapeDtypeStruct((bsz, s, GLA_DV), BF16),
        scratch_shapes=[pltpu.VMEM((GLA_HPS, GLA_HV, GLA_HK), F32)],
        compiler_params=_params(("parallel", "parallel", "arbitrary")),
        name="gla",
    )(z3, z3, z3, z3, zs3, w_gate_pad, b_gate, g_out)


def _cmp_kernel(t_ref, pe_ref, w1_ref, w2_ref, o_ref, tf_scr, lo_scr, hi_scr):
    hd = NSA_HD
    half = CMP_STRIDE * hd
    nchunk = lo_scr.shape[0]
    tf_scr[...] = t_ref[...].astype(F32)
    for r in range(CMP_STRIDE):
        t_r = tf_scr[pl.ds(r, nchunk, stride=CMP_STRIDE), :]
        lo_scr[:, r * hd:(r + 1) * hd] = (t_r + pe_ref[r:r + 1, :]).astype(BF16)
        hi_scr[:, r * hd:(r + 1) * hd] = (t_r + pe_ref[CMP_STRIDE + r:CMP_STRIDE + r + 1, :]).astype(BF16)
    a = _dot(lo_scr[...], w1_ref[:half, :].astype(BF16))
    bm = _dot(hi_scr[...], w1_ref[half:, :].astype(BF16))
    pre = a + pltpu.roll(bm, nchunk - 1, 0)
    hid = (pre * _sigmoid(pre)).astype(BF16)
    o_ref[...] = _dot(hid, w2_ref[...].astype(BF16)).astype(o_ref.dtype)


def _compress(z3, pe, w1, w2):
    bsz, s, _ = z3.shape
    hkv, hd = NSA_KV_HEADS, NSA_HD
    nchunk = s // CMP_STRIDE
    width = CMP_STRIDE * hd
    return pl.pallas_call(
        _cmp_kernel,
        grid=(2, bsz, hkv),
        in_specs=[
            pl.BlockSpec((None, s, hd), lambda kv, b, h: (b, 0, Z_NKV // hd + kv * hkv + h)),
            pl.BlockSpec((None, CMP_LEN, hd), lambda kv, b, h: (kv, 0, 0)),
            pl.BlockSpec((None, 2 * width, CMP_HIDDEN), lambda kv, b, h: (kv, 0, 0)),
            pl.BlockSpec((None, CMP_HIDDEN, hd), lambda kv, b, h: (kv, 0, 0)),
        ],
        out_specs=pl.BlockSpec((None, None, None, nchunk, hd), lambda kv, b, h: (kv, b, h, 0, 0)),
        out_shape=jax.ShapeDtypeStruct((2, bsz, hkv, nchunk, hd), BF16),
        scratch_shapes=[pltpu.VMEM((s, hd), F32),
                        pltpu.VMEM((nchunk, width), BF16),
                        pltpu.VMEM((nchunk, width), BF16)],
        compiler_params=_params(("parallel", "parallel", "parallel")),
        name="nsa_compress",
    )(z3, pe, w1, w2)


NSA_TQ = 256
NSA_TK = NSA_TQ
LOG2E = 1.4426950408889634


NSA_HPS = 4


def _nsa_kernel(q_ref, sm_ref, kc_ref, vc_ref, ks_ref, vs_ref, kw_ref, vw_ref, mcs_ref,
                o_ref, sel_scr, gate_scr, p_scr):
    G, TQ, TK, HD = NSA_GROUP, NSA_TQ, NSA_TK, NSA_HD
    R = G * TQ
    heads = range(NSA_HPS)
    hk0 = pl.program_id(1) * NSA_HPS
    qi = pl.program_id(2)
    qs = qi * TQ

    def lanes4(x):
        return jnp.concatenate([x] * G, axis=1)

    def hcol(j):
        return slice(j * HD, (j + 1) * HD)

    q4 = []
    for j in heads:
        qj = jnp.concatenate([q_ref[:, (j * G + g) * HD:(j * G + g + 1) * HD] for g in range(G)], axis=0)
        q4.append((qj.astype(F32) * (HD ** -0.5 * LOG2E)).astype(BF16))

    nc = kc_ref.shape[1]
    tq_c = qs + lax.broadcasted_iota(jnp.int32, (nc, TQ), 1)
    cmp_end = lax.broadcasted_iota(jnp.int32, (nc, TQ), 0) * CMP_STRIDE + (CMP_LEN - 1)
    valid_c = lanes4(jnp.where(cmp_end <= tq_c, 1.0, 0.0)) > 0.5
    mcs = mcs_ref[...]
    nb = mcs.shape[0]
    blk = lax.broadcasted_iota(jnp.int32, (nb, TQ), 0)
    tq_b = qs + lax.broadcasted_iota(jnp.int32, (nb, TQ), 1)
    cur = tq_b // SLC_BLOCK
    forced = (blk == 0) | (blk == cur) | (blk == cur - 1)
    visible = blk * SLC_BLOCK <= tq_b
    o_c = []
    for j in heads:
        s_c = jnp.where(valid_c, _dot_nt(kc_ref[j], q4[j]), NEG_INF)
        e_c = jnp.where(valid_c, jnp.exp2(s_c - jnp.max(s_c, axis=0, keepdims=True)), 0.0)
        l_c = jnp.sum(e_c, axis=0, keepdims=True)
        p_c = e_c / jnp.where(l_c > 0.0, l_c, 1.0)
        o_c.append(_dot_tn(vc_ref[j], p_c.astype(BF16)))

        p_sum = p_c[:, 0:TQ]
        for g in range(1, G):
            p_sum = p_sum + p_c[:, g * TQ:(g + 1) * TQ]
        imp = sum(_dot(mcs, part) for part in _split3(p_sum))
        imp = jnp.where(forced, jnp.inf, imp)
        imp = jnp.where(visible, imp, -jnp.inf)
        rank = jnp.zeros((nb, TQ), F32)
        for kb in range(nb):
            rk = imp[kb:kb + 1, :]
            ahead = jnp.where(rk > imp, 1.0, jnp.where(rk == imp, jnp.where(blk > kb, 1.0, 0.0), 0.0))
            rank = rank + ahead
        sel_scr[j] = jnp.where(rank < float(min(SLC_TOPN, nb)), 0.0, NEG_INF)

    blocks_per_tile = TK // SLC_BLOCK

    def sel_bias(j, tile):
        return jnp.concatenate(
            [jnp.broadcast_to(sel_scr[j, pl.ds(tile * blocks_per_tile + i, 1), :], (SLC_BLOCK, TQ))
             for i in range(blocks_per_tile)], axis=0)

    krow = lax.broadcasted_iota(jnp.int32, (TK, TQ), 0)
    qcol = lax.broadcasted_iota(jnp.int32, (TK, TQ), 1)
    diag = pl.ds(pl.multiple_of(qs, TK), TK)
    m_d, l_d = [], []
    for j in heads:
        bias_d = jnp.where(krow <= qcol, sel_bias(j, qi), NEG_INF)
        s_d = _dot_nt(ks_ref[diag, hcol(j)], q4[j]) + lanes4(bias_d)
        m = jnp.max(s_d, axis=0, keepdims=True)
        p_d = jnp.exp2(s_d - m)
        m_d.append(m)
        l_d.append(jnp.sum(p_d, axis=0, keepdims=True))
        p_scr[j] = p_d.astype(BF16)

    def kv_step(t, carry):
        ms, ls, accs, voff = carry
        prev = pl.ds(pl.multiple_of(voff, TK), TK)
        koff = pl.multiple_of(t * TK, TK)
        ms_n, ls_n, accs_n = [], [], []
        for j in heads:
            pv = _dot_tn(vs_ref[prev, hcol(j)], p_scr[j])
            s = _dot_nt(ks_ref[pl.ds(koff, TK), hcol(j)], q4[j]) + lanes4(sel_bias(j, t))
            m_new = jnp.maximum(ms[j], jnp.max(s, axis=0, keepdims=True))
            alpha = jnp.exp2(ms[j] - m_new)
            p = jnp.exp2(s - m_new)
            ms_n.append(m_new)
            ls_n.append(alpha * ls[j] + jnp.sum(p, axis=0, keepdims=True))
            accs_n.append(alpha * (accs[j] + pv))
            p_scr[j] = p.astype(BF16)
        return tuple(ms_n), tuple(ls_n), tuple(accs_n), koff

    acc0 = tuple(jnp.zeros((HD, R), F32) for _ in heads)
    _, l_s, acc_s, voff = lax.fori_loop(0, qi, kv_step, (tuple(m_d), tuple(l_d), acc0, qs))
    last = pl.ds(pl.multiple_of(voff, TK), TK)
    o_s = [(acc_s[j] + _dot_tn(vs_ref[last, hcol(j)], p_scr[j])) / l_s[j] for j in heads]

    span = WINDOW + TQ
    w0 = pl.multiple_of(jnp.maximum(qs - WINDOW, 0), TQ)
    kpos = w0 + lax.broadcasted_iota(jnp.int32, (span, TQ), 0)
    tq_w = qs + lax.broadcasted_iota(jnp.int32, (span, TQ), 1)
    bias_w = lanes4(jnp.where(kpos <= tq_w, jnp.where(kpos > tq_w - WINDOW, 0.0, NEG_INF), NEG_INF))
    o_w = []
    for j in heads:
        s_w = _dot_nt(kw_ref[pl.ds(w0, span), hcol(j)], q4[j]) + bias_w
        e_w = jnp.exp2(s_w - jnp.max(s_w, axis=0, keepdims=True))
        o_w.append(_dot_tn(vw_ref[pl.ds(w0, span), hcol(j)], e_w.astype(BF16))
                   / jnp.sum(e_w, axis=0, keepdims=True))

    gate_scr[...] = _sigmoid(sm_ref[...]).T
    for j in heads:
        def gate_row(branch):
            base = SMALL_NGATE + branch * NSA_HEADS + (hk0 + j) * G
            return jnp.concatenate([gate_scr[pl.ds(base + g, 1), :] for g in range(G)], axis=1)

        o = gate_row(0) * o_c[j] + gate_row(1) * o_s[j] + gate_row(2) * o_w[j]
        for g in range(G):
            o_ref[:, (j * G + g) * HD:(j * G + g + 1) * HD] = o[:, g * TQ:(g + 1) * TQ].T.astype(o_ref.dtype)


def _nsa(z3, zs3, cmp, mcs_t):
    bsz, s, _ = z3.shape
    hkv, hd, tq, hps = NSA_KV_HEADS, NSA_HD, NSA_TQ, NSA_HPS
    qw = hps * NSA_GROUP * hd
    kw = hps * hd
    nc = cmp.shape[3]
    nb = mcs_t.shape[0]

    def kv_spec(i):
        return pl.BlockSpec((None, s, kw), lambda b, h, q: (b, 0, (Z_NKV + i * hkv * hd) // kw + h))

    return pl.pallas_call(
        _nsa_kernel,
        grid=(bsz, hkv // hps, s // tq),
        in_specs=[
            pl.BlockSpec((None, tq, qw), lambda b, h, q: (b, q, Z_NQ // qw + h)),
            pl.BlockSpec((None, tq, LANES), lambda b, h, q: (b, q, 0)),
            pl.BlockSpec((None, None, hps, nc, hd), lambda b, h, q: (0, b, h, 0, 0)),
            pl.BlockSpec((None, None, hps, nc, hd), lambda b, h, q: (1, b, h, 0, 0)),
            kv_spec(2), kv_spec(3), kv_spec(4), kv_spec(5),
            pl.BlockSpec(mcs_t.shape, lambda b, h, q: (0, 0)),
        ],
        out_specs=pl.BlockSpec((None, tq, qw), lambda b, h, q: (b, q, h)),
        out_shape=jax.ShapeDtypeStruct((bsz, s, NSA_HEADS * hd), BF16),
        scratch_shapes=[pltpu.VMEM((hps, nb, tq), F32), pltpu.VMEM((LANES, tq), F32),
                        pltpu.VMEM((hps, NSA_TK, NSA_GROUP * tq), BF16)],
        compiler_params=_params(("parallel", "parallel", "arbitrary")),
        name="nsa_attn",
    )(z3, zs3, cmp, cmp, z3, z3, z3, z3, mcs_t)


def _merge_kernel(og_ref, on_ref, wg_ref, wn_ref, ga_ref, gb_ref, o_ref, wg_scr, wn_scr):
    @pl.when(pl.program_id(1) == 0)
    def _():
        wg_scr[...] = wg_ref[...].astype(BF16)
        wn_scr[...] = wn_ref[...].astype(BF16)

    a = _dot(og_ref[...], wg_scr[...])
    b = _dot(on_ref[...], wn_scr[...])
    ga = _sigmoid(ga_ref[...].astype(F32))
    gb = _sigmoid(gb_ref[...].astype(F32))
    o_ref[...] = (ga * a + gb * b).astype(o_ref.dtype)


def _merge(o_gla, o_nsa, w_pg, w_pn, z2, tm=512, tn=1024):
    m = o_gla.shape[0]
    once = pl.Buffered(1)
    return pl.pallas_call(
        _merge_kernel,
        grid=(D_MODEL // tn, m // tm),
        in_specs=[
            pl.BlockSpec((tm, GLA_DV), lambda j, i: (i, 0)),
            pl.BlockSpec((tm, NSA_HEADS * NSA_HD), lambda j, i: (i, 0)),
            pl.BlockSpec((GLA_DV, tn), lambda j, i: (0, j), pipeline_mode=once),
            pl.BlockSpec((NSA_HEADS * NSA_HD, tn), lambda j, i: (0, j), pipeline_mode=once),
            pl.BlockSpec((tm, tn), lambda j, i: (i, Z_MG // tn + j)),
            pl.BlockSpec((tm, tn), lambda j, i: (i, (Z_MG + D_MODEL) // tn + j)),
        ],
        out_specs=pl.BlockSpec((tm, tn), lambda j, i: (i, j)),
        out_shape=jax.ShapeDtypeStruct((m, D_MODEL), BF16),
        scratch_shapes=[pltpu.VMEM((GLA_DV, tn), BF16), pltpu.VMEM((NSA_HEADS * NSA_HD, tn), BF16)],
        compiler_params=_params(("parallel", "arbitrary")),
        name="merge_proj",
    )(o_gla, o_nsa, w_pg, w_pn, z2, z2)


def _out_proj_kernel(mg_ref, w_ref, x_ref, g_ref, x1_ref, h_ref):
    x1 = x_ref[...] + _dot(mg_ref[...], w_ref[...])
    x1_ref[...] = x1
    ms = jnp.mean(x1 * x1, axis=-1, keepdims=True)
    h_ref[...] = (x1 * lax.rsqrt(ms + RMS_EPS) * g_ref[...]).astype(h_ref.dtype)


def _out_proj(merged, w_out, x2, g_ffn, tm=512):
    m = merged.shape[0]
    return pl.pallas_call(
        _out_proj_kernel,
        grid=(m // tm,),
        in_specs=[
            pl.BlockSpec((tm, D_MODEL), lambda i: (i, 0)),
            pl.BlockSpec((D_MODEL, D_MODEL), lambda i: (0, 0)),
            pl.BlockSpec((tm, D_MODEL), lambda i: (i, 0)),
            pl.BlockSpec((1, D_MODEL), lambda i: (0, 0)),
        ],
        out_specs=[
            pl.BlockSpec((tm, D_MODEL), lambda i: (i, 0)),
            pl.BlockSpec((tm, D_MODEL), lambda i: (i, 0)),
        ],
        out_shape=[jax.ShapeDtypeStruct((m, D_MODEL), F32),
                   jax.ShapeDtypeStruct((m, D_MODEL), BF16)],
        compiler_params=_params(("parallel",)),
        name="out_proj",
    )(merged, w_out, x2, g_ffn)


def _ffn_up_kernel(h_ref, wg_ref, wu_ref, o_ref, wg_scr, wu_scr):
    @pl.when(pl.program_id(1) == 0)
    def _():
        wg_scr[...] = wg_ref[...].astype(BF16)
        wu_scr[...] = wu_ref[...].astype(BF16)

    h = h_ref[...]
    g = _dot(h, wg_scr[...])
    u = _dot(h, wu_scr[...])
    o_ref[...] = (g * _sigmoid(g) * u).astype(o_ref.dtype)


def _ffn_up(h2, w_gate, w_up, tm=2048, tf=512):
    m = h2.shape[0]
    return pl.pallas_call(
        _ffn_up_kernel,
        grid=(D_FF // tf, m // tm),
        in_specs=[
            pl.BlockSpec((tm, D_MODEL), lambda j, i: (i, 0)),
            pl.BlockSpec((D_MODEL, tf), lambda j, i: (0, j)),
            pl.BlockSpec((D_MODEL, tf), lambda j, i: (0, j)),
        ],
        out_specs=pl.BlockSpec((tm, tf), lambda j, i: (i, j)),
        out_shape=jax.ShapeDtypeStruct((m, D_FF), BF16),
        scratch_shapes=[pltpu.VMEM((D_MODEL, tf), BF16), pltpu.VMEM((D_MODEL, tf), BF16)],
        compiler_params=_params(("parallel", "arbitrary")),
        name="ffn_up",
    )(h2, w_gate, w_up)


def _ffn_down_kernel(a_ref, w_ref, x1_ref, g_ref, o_ref, *, final_norm):
    k = pl.program_id(1)

    @pl.when(k == 0)
    def _():
        o_ref[...] = x1_ref[...]

    o_ref[...] += _dot(a_ref[...], w_ref[...])

    if final_norm:
        @pl.when(k == pl.num_programs(1) - 1)
        def _():
            y = o_ref[...]
            ms = jnp.mean(y * y, axis=-1, keepdims=True)
            o_ref[...] = y * lax.rsqrt(ms + RMS_EPS) * g_ref[...]


def _ffn_down(act, w_down, x1, g_final, final_norm, tm=1024, tk=1408):
    m = act.shape[0]
    return pl.pallas_call(
        functools.partial(_ffn_down_kernel, final_norm=final_norm),
        grid=(m // tm, D_FF // tk),
        in_specs=[
            pl.BlockSpec((tm, tk), lambda i, k: (i, k)),
            pl.BlockSpec((tk, D_MODEL), lambda i, k: (k, 0)),
            pl.BlockSpec((tm, D_MODEL), lambda i, k: (i, 0)),
            pl.BlockSpec((1, D_MODEL), lambda i, k: (0, 0)),
        ],
        out_specs=pl.BlockSpec((tm, D_MODEL), lambda i, k: (i, 0)),
        out_shape=jax.ShapeDtypeStruct((m, D_MODEL), F32),
        compiler_params=_params(("parallel", "arbitrary")),
        name="ffn_down",
    )(act, w_down, x1, g_final)


def _cmp_to_slc_weights_t(seq):
    n_cmp = seq // CMP_STRIDE - 1
    n_slc = seq // SLC_BLOCK
    c0 = np.arange(n_cmp)[:, None] * CMP_STRIDE
    s0 = np.arange(n_slc)[None, :] * SLC_BLOCK
    ov = np.clip(np.minimum(c0 + CMP_LEN, s0 + SLC_BLOCK) - np.maximum(c0, s0), 0, None)
    m_cs = np.zeros((n_cmp + 1, n_slc), np.float32)
    m_cs[:n_cmp] = ov / CMP_LEN
    return jnp.asarray(m_cs.T, BF16)


def kernel(x, g_mix, w_in, w_gla_gate, b_gla_gate, g_gla_out, pe_cmp_k, w_cmp_k1, w_cmp_k2,
           pe_cmp_v, w_cmp_v1, w_cmp_v2, w_proj_gla, w_proj_nsa, w_out, g_ffn,
           w_ffn_gate, w_ffn_up, w_ffn_down, g_final):
    bsz, seq, d = x.shape
    m = bsz * seq
    depth = w_in.shape[0]
    mcs_t = _cmp_to_slc_weights_t(seq)
    xc = x.reshape(m, d)
    for l in range(depth):
        w_t = jnp.swapaxes(w_in[l], 0, 1)
        h, zs = _norm_small(xc, g_mix[l][None], w_t)
        z2 = _in_proj(h, w_t)
        z3 = z2.reshape(bsz, seq, Z_SMALL)
        zs3 = zs.reshape(bsz, seq, LANES)

        w_gate_pad = jnp.zeros((LANES, GLA_DK), F32).at[SMALL_GLR:SMALL_GLR + GLA_GATE_RANK].set(w_gla_gate[l])
        o_gla = _gla(z3, zs3, w_gate_pad.astype(BF16), b_gla_gate[l][None], g_gla_out[l][None])

        cmp = _compress(z3, jnp.stack([pe_cmp_k[l], pe_cmp_v[l]]),
                        jnp.stack([w_cmp_k1[l], w_cmp_v1[l]]), jnp.stack([w_cmp_k2[l], w_cmp_v2[l]]))
        o_nsa = _nsa(z3, zs3, cmp, mcs_t)

        merged = _merge(o_gla.reshape(m, GLA_DV), o_nsa.reshape(m, NSA_HEADS * NSA_HD),
                        w_proj_gla[l], w_proj_nsa[l], z2)
        x1, h2 = _out_proj(merged, w_out[l].astype(BF16), xc, g_ffn[l][None])
        act = _ffn_up(h2, w_ffn_gate[l], w_ffn_up[l])
        xc = _ffn_down(act, w_ffn_down[l].astype(BF16), x1, g_final[None],
                       final_norm=(l == depth - 1))
    return xc.reshape(bsz, seq, d)
```

```python
import functools

import numpy as np
import jax
import jax.numpy as jnp
from jax import lax
from jax.experimental import pallas as pl
from jax.experimental.pallas import tpu as pltpu

F32 = jnp.float32
BF16 = jnp.bfloat16

D_MODEL = 2048
GLA_HEADS = 4
GLA_DK = D_MODEL // 2
GLA_DV = D_MODEL
GLA_HK = GLA_DK // GLA_HEADS
GLA_HV = GLA_DV // GLA_HEADS
GLA_GATE_RANK = 16
GLA_TAU = 16.0
GLA_CHUNK = 64

NSA_HEADS = 16
NSA_KV_HEADS = 4
NSA_GROUP = NSA_HEADS // NSA_KV_HEADS
NSA_HD = D_MODEL // NSA_HEADS
CMP_STRIDE = 16
CMP_LEN = 2 * CMP_STRIDE
CMP_HIDDEN = 256
SLC_BLOCK = 64
SLC_TOPN = 16
WINDOW = 512
D_FF = ((8 * D_MODEL // 3 + 255) // 256) * 256

IN_SPLIT_SIZES = (GLA_DK, GLA_DK, GLA_DV, GLA_DV, GLA_GATE_RANK,
                  NSA_HEADS * NSA_HD, 6 * NSA_KV_HEADS * NSA_HD, 3 * NSA_HEADS,
                  2 * D_MODEL)

NEG_INF = -1e30
RMS_EPS = 1e-6

Z_GQ = 0
Z_GK = Z_GQ + GLA_DK
Z_GV = Z_GK + GLA_DK
Z_GR = Z_GV + GLA_DV
Z_NQ = Z_GR + GLA_DV
Z_NKV = Z_NQ + NSA_HEADS * NSA_HD
Z_MG = Z_NKV + 6 * NSA_KV_HEADS * NSA_HD
Z_SMALL = Z_MG + 2 * D_MODEL
LANES = 128
SMALL_GLR = 0
SMALL_NGATE = GLA_GATE_RANK

VMEM_LIMIT = 56 * 1024 * 1024


def _dot(a, b):
    return jnp.dot(a, b, preferred_element_type=F32)


def _dot_nt(a, b):
    return lax.dot_general(a, b, (((1,), (1,)), ((), ())), preferred_element_type=F32)


def _dot_tn(a, b):
    return lax.dot_general(a, b, (((0,), (0,)), ((), ())), preferred_element_type=F32)


def _sigmoid(x):
    return 1.0 / (1.0 + jnp.exp(-x))


def _split2(x):
    hi = x.astype(BF16)
    lo = (x - hi.astype(F32)).astype(BF16)
    return hi, lo


def _split3(x):
    hi = x.astype(BF16)
    r1 = x - hi.astype(F32)
    mid = r1.astype(BF16)
    lo = (r1 - mid.astype(F32)).astype(BF16)
    return hi, mid, lo


def _params(sem, **kw):
    return pltpu.CompilerParams(dimension_semantics=sem, vmem_limit_bytes=VMEM_LIMIT, **kw)


_IN_OFFS = np.cumsum((0,) + IN_SPLIT_SIZES)
COL_GLR = int(_IN_OFFS[4])
COL_NGATE = int(_IN_OFFS[7])
assert COL_GLR % LANES == SMALL_GLR and COL_NGATE % LANES == SMALL_NGATE
assert SMALL_GLR + GLA_GATE_RANK == SMALL_NGATE and SMALL_NGATE + 3 * NSA_HEADS <= LANES


def _norm_kernel(x_ref, g_ref, wa_ref, wb_ref, h_ref, zs_ref):
    xf = x_ref[...]
    ms = jnp.mean(xf * xf, axis=-1, keepdims=True)
    h = (xf * lax.rsqrt(ms + RMS_EPS) * g_ref[...]).astype(BF16)
    h_ref[...] = h
    row = lax.broadcasted_iota(jnp.int32, wa_ref.shape, 0)
    w_small = jnp.where(row < SMALL_NGATE, wa_ref[...],
                        jnp.where(row < SMALL_NGATE + 3 * NSA_HEADS, wb_ref[...], 0.0))
    zs_ref[...] = _dot_nt(h, w_small.astype(BF16))


def _norm_small(x2, g, w_t, tm=512):
    m = x2.shape[0]
    return pl.pallas_call(
        _norm_kernel,
        grid=(m // tm,),
        in_specs=[
            pl.BlockSpec((tm, D_MODEL), lambda i: (i, 0)),
            pl.BlockSpec((1, D_MODEL), lambda i: (0, 0)),
            pl.BlockSpec((LANES, D_MODEL), lambda i: (COL_GLR // LANES, 0)),
            pl.BlockSpec((LANES, D_MODEL), lambda i: (COL_NGATE // LANES, 0)),
        ],
        out_specs=[pl.BlockSpec((tm, D_MODEL), lambda i: (i, 0)),
                   pl.BlockSpec((tm, LANES), lambda i: (i, 0))],
        out_shape=[jax.ShapeDtypeStruct((m, D_MODEL), BF16),
                   jax.ShapeDtypeStruct((m, LANES), F32)],
        compiler_params=_params(("parallel",)),
        name="norm_small",
    )(x2, g, w_t, w_t)


IN_TN = 1024
IN_SHIFT_B = GLA_GATE_RANK
IN_SHIFT_C = GLA_GATE_RANK + 3 * NSA_HEADS
IN_BLOCKS_A = Z_NQ // IN_TN
IN_BLOCKS_AB = Z_MG // IN_TN
IN_ROWS = 48
assert (IN_TN - IN_SHIFT_B) % IN_ROWS == 0 and (IN_TN - IN_SHIFT_C) % IN_ROWS == 0


def _in_proj_kernel(h_ref, wm_ref, we_ref, z_ref, w_scr):
    j = pl.program_id(0)

    def build(shift):
        if shift == 0:
            w_scr[...] = wm_ref[...].astype(BF16)
            return

        def body(r, carry):
            dst = pl.ds(pl.multiple_of(r * IN_ROWS, 16), IN_ROWS)
            src = pl.ds(pl.multiple_of(r * IN_ROWS + shift, 8), IN_ROWS)
            w_scr[dst, :] = wm_ref[src, :].astype(BF16)
            return carry
        lax.fori_loop(0, (IN_TN - shift) // IN_ROWS, body, 0)
        w_scr[IN_TN - shift:, :] = we_ref[:shift, :].astype(BF16)

    first = pl.program_id(1) == 0

    @pl.when(first & (j < IN_BLOCKS_A))
    def _():
        build(0)

    @pl.when(first & (j >= IN_BLOCKS_A) & (j < IN_BLOCKS_AB))
    def _():
        build(IN_SHIFT_B)

    @pl.when(first & (j >= IN_BLOCKS_AB))
    def _():
        build(IN_SHIFT_C)

    z_ref[...] = _dot_nt(h_ref[...], w_scr[...]).astype(z_ref.dtype)


def _in_proj(h, w_t, tm=1024):
    m = h.shape[0]
    tn = IN_TN
    return pl.pallas_call(
        _in_proj_kernel,
        grid=(Z_SMALL // tn, m // tm),
        in_specs=[
            pl.BlockSpec((tm, D_MODEL), lambda j, i: (i, 0)),
            pl.BlockSpec((tn, D_MODEL), lambda j, i: (j, 0)),
            pl.BlockSpec((LANES, D_MODEL), lambda j, i: ((j + 1) * (tn // LANES), 0)),
        ],
        out_specs=pl.BlockSpec((tm, tn), lambda j, i: (i, j)),
        out_shape=jax.ShapeDtypeStruct((m, Z_SMALL), BF16),
        scratch_shapes=[pltpu.VMEM((tn, D_MODEL), BF16)],
        compiler_params=_params(("parallel", "arbitrary")),
        name="in_proj",
    )(h, w_t, w_t)


GLA_HPS = 4


def _gla_kernel(q_ref, k_ref, v_ref, r_ref, sm_ref, wg_ref, bg_ref, gn_ref, o_ref,
                state_ref, *, n_chunks):
    @pl.when(pl.program_id(2) == 0)
    def _():
        state_ref[...] = jnp.zeros_like(state_ref)

    C = GLA_CHUNK
    HK, HV = GLA_HK, GLA_HV
    ts = n_chunks * C
    row = lax.broadcasted_iota(jnp.int32, (ts, ts), 0)
    col = lax.broadcasted_iota(jnp.int32, (ts, ts), 1)
    tri = jnp.where((row >= col) & ((row // C) == (col // C)), 1.0, 0.0).astype(BF16)
    crow = lax.broadcasted_iota(jnp.int32, (C, C), 0)
    ccol = lax.broadcasted_iota(jnp.int32, (C, C), 1)
    causal = crow >= ccol

    u = _dot(sm_ref[...].astype(BF16), wg_ref[...]) + bg_ref[...]
    log_a = (jnp.minimum(u, 0.0) - jnp.log(1.0 + jnp.exp(-jnp.abs(u)))) / GLA_TAU
    hi, lo = _split2(log_a)
    b_all = _dot(tri, hi) + _dot(tri, lo)
    gn = gn_ref[...]

    heads = range(GLA_HPS)
    kcol = [slice(j * HK, (j + 1) * HK) for j in heads]
    vcol = [slice(j * HV, (j + 1) * HV) for j in heads]
    b = [b_all[:, kcol[j]] for j in heads]
    k = [k_ref[:, kcol[j]].astype(F32) for j in heads]
    q_dec = [(q_ref[:, kcol[j]].astype(F32) * (HK ** -0.5) * jnp.exp(b[j])).astype(BF16) for j in heads]
    k_dec = [(k[j] * jnp.exp(-b[j])).astype(BF16) for j in heads]
    state = [state_ref[j] for j in heads]
    for c in range(n_chunks):
        rows = slice(c * C, (c + 1) * C)
        for j in heads:
            b_c = b[j][rows]
            b_last = b_c[C - 1:C, :]
            k_end = (k[j][rows] * jnp.exp(b_last - b_c)).astype(BF16)
            v_c = v_ref[rows, vcol[j]]
            attn = jnp.where(causal, _dot_nt(q_dec[j][rows], k_dec[j][rows]), 0.0).astype(BF16)
            o = _dot(attn, v_c) + _dot_nt(q_dec[j][rows], state[j].astype(BF16))
            state[j] = state[j] * jnp.exp(b_last) + _dot_tn(v_c, k_end)
            ms = jnp.mean(o * o, axis=-1, keepdims=True)
            y = o * lax.rsqrt(ms + RMS_EPS) * gn
            r = r_ref[rows, vcol[j]].astype(F32)
            o_ref[rows, vcol[j]] = (y * (r * _sigmoid(r))).astype(o_ref.dtype)
    for j in heads:
        state_ref[j] = state[j]


def _gla(z3, zs3, w_gate_pad, b_gate, g_out, ts=512):
    bsz, s, _ = z3.shape
    qb, vb = GLA_HPS * GLA_HK, GLA_HPS * GLA_HV
    return pl.pallas_call(
        functools.partial(_gla_kernel, n_chunks=ts // GLA_CHUNK),
        grid=(bsz, GLA_HEADS // GLA_HPS, s // ts),
        in_specs=[
            pl.BlockSpec((None, ts, qb), lambda b, h, i: (b, i, Z_GQ // qb + h)),
            pl.BlockSpec((None, ts, qb), lambda b, h, i: (b, i, Z_GK // qb + h)),
            pl.BlockSpec((None, ts, vb), lambda b, h, i: (b, i, Z_GV // vb + h)),
            pl.BlockSpec((None, ts, vb), lambda b, h, i: (b, i, Z_GR // vb + h)),
            pl.BlockSpec((None, ts, LANES), lambda b, h, i: (b, i, 0)),
            pl.BlockSpec((LANES, qb), lambda b, h, i: (0, h)),
            pl.BlockSpec((1, qb), lambda b, h, i: (0, h)),
            pl.BlockSpec((1, GLA_HV), lambda b, h, i: (0, 0)),
        ],
        out_specs=pl.BlockSpec((None, ts, vb), lambda b, h, i: (b, i, h)),
        out_shape=jax.ShapeDtypeStruct((bsz, s, GLA_DV), BF16),
        scratch_shapes=[pltpu.VMEM((GLA_HPS, GLA_HV, GLA_HK), F32)],
        compiler_params=_params(("parallel", "parallel", "arbitrary")),
        name="gla",
    )(z3, z3, z3, z3, zs3, w_gate_pad, b_gate, g_out)


def _cmp_kernel(t_ref, pe_ref, w1_ref, w2_ref, o_ref, tf_scr, lo_scr, hi_scr):
    hd = NSA_HD
    half = CMP_STRIDE * hd
    nchunk = lo_scr.shape[0] // NSA_KV_HEADS
    for h in range(NSA_KV_HEADS):
        tf_scr[h] = t_ref[:, h * hd:(h + 1) * hd].astype(F32)
        rows = slice(h * nchunk, (h + 1) * nchunk)
        for r in range(CMP_STRIDE):
            t_r = tf_scr.at[h][pl.ds(r, nchunk, stride=CMP_STRIDE), :]
            lo_scr[rows, r * hd:(r + 1) * hd] = (t_r + pe_ref[r:r + 1, :]).astype(BF16)
            hi_scr[rows, r * hd:(r + 1) * hd] = (t_r + pe_ref[CMP_STRIDE + r:CMP_STRIDE + r + 1, :]).astype(BF16)
    a = _dot(lo_scr[...], w1_ref[:half, :].astype(BF16))
    bm = _dot(hi_scr[...], w1_ref[half:, :].astype(BF16))
    pre = a + pltpu.roll(bm, bm.shape[0] - 1, 0)
    hid = (pre * _sigmoid(pre)).astype(BF16)
    out = _dot(hid, w2_ref[...].astype(BF16)).astype(o_ref.dtype)
    for h in range(NSA_KV_HEADS):
        o_ref[h] = out[h * nchunk:(h + 1) * nchunk]


def _compress(z3, pe, w1, w2):
    bsz, s, _ = z3.shape
    hkv, hd = NSA_KV_HEADS, NSA_HD
    nchunk = s // CMP_STRIDE
    width = CMP_STRIDE * hd
    return pl.pallas_call(
        _cmp_kernel,
        grid=(2, bsz),
        in_specs=[
            pl.BlockSpec((None, s, hkv * hd), lambda kv, b: (b, 0, Z_NKV // (hkv * hd) + kv)),
            pl.BlockSpec((None, CMP_LEN, hd), lambda kv, b: (kv, 0, 0)),
            pl.BlockSpec((None, 2 * width, CMP_HIDDEN), lambda kv, b: (kv, 0, 0)),
            pl.BlockSpec((None, CMP_HIDDEN, hd), lambda kv, b: (kv, 0, 0)),
        ],
        out_specs=pl.BlockSpec((None, None, hkv, nchunk, hd), lambda kv, b: (kv, b, 0, 0, 0)),
        out_shape=jax.ShapeDtypeStruct((2, bsz, hkv, nchunk, hd), BF16),
        scratch_shapes=[pltpu.VMEM((hkv, s, hd), F32),
                        pltpu.VMEM((hkv * nchunk, width), BF16),
                        pltpu.VMEM((hkv * nchunk, width), BF16)],
        compiler_params=_params(("parallel", "parallel")),
        name="nsa_compress",
    )(z3, pe, w1, w2)


NSA_TQ = 256
NSA_TK = NSA_TQ
LOG2E = 1.4426950408889634


NSA_HPS = 4


def _nsa_kernel(q_ref, sm_ref, kc_ref, vc_ref, ks_ref, vs_ref, kw_ref, vw_ref, mcs_ref,
                o_ref, sel_scr, gate_scr, p_scr):
    G, TQ, TK, HD = NSA_GROUP, NSA_TQ, NSA_TK, NSA_HD
    R = G * TQ
    heads = range(NSA_HPS)
    hk0 = pl.program_id(1) * NSA_HPS
    qi = pl.program_id(2)
    qs = qi * TQ

    def lanes4(x):
        return jnp.concatenate([x] * G, axis=1)

    def hcol(j):
        return slice(j * HD, (j + 1) * HD)

    q4 = []
    for j in heads:
        qj = jnp.concatenate([q_ref[:, (j * G + g) * HD:(j * G + g + 1) * HD] for g in range(G)], axis=0)
        q4.append((qj.astype(F32) * (HD ** -0.5 * LOG2E)).astype(BF16))

    nc = kc_ref.shape[1]
    tq_c = qs + lax.broadcasted_iota(jnp.int32, (nc, TQ), 1)
    cmp_end = lax.broadcasted_iota(jnp.int32, (nc, TQ), 0) * CMP_STRIDE + (CMP_LEN - 1)
    valid_c = lanes4(jnp.where(cmp_end <= tq_c, 1.0, 0.0)) > 0.5
    mcs = mcs_ref[...]
    nb = mcs.shape[0]
    blk = lax.broadcasted_iota(jnp.int32, (nb, TQ), 0)
    tq_b = qs + lax.broadcasted_iota(jnp.int32, (nb, TQ), 1)
    cur = tq_b // SLC_BLOCK
    forced = (blk == 0) | (blk == cur) | (blk == cur - 1)
    visible = blk * SLC_BLOCK <= tq_b
    o_c = []
    for j in heads:
        s_c = jnp.where(valid_c, _dot_nt(kc_ref[j], q4[j]), NEG_INF)
        e_c = jnp.where(valid_c, jnp.exp2(s_c - jnp.max(s_c, axis=0, keepdims=True)), 0.0)
        l_c = jnp.sum(e_c, axis=0, keepdims=True)
        p_c = e_c / jnp.where(l_c > 0.0, l_c, 1.0)
        o_c.append(_dot_tn(vc_ref[j], p_c.astype(BF16)))

        p_sum = p_c[:, 0:TQ]
        for g in range(1, G):
            p_sum = p_sum + p_c[:, g * TQ:(g + 1) * TQ]
        imp = sum(_dot(mcs, part) for part in _split3(p_sum))
        imp = jnp.where(forced, jnp.inf, imp)
        imp = jnp.where(visible, imp, -jnp.inf)
        rank = jnp.zeros((nb, TQ), F32)
        for kb in range(nb):
            rk = imp[kb:kb + 1, :]
            ahead = jnp.where(rk > imp, 1.0, jnp.where(rk == imp, jnp.where(blk > kb, 1.0, 0.0), 0.0))
            rank = rank + ahead
        sel_scr[j] = jnp.where(rank < float(min(SLC_TOPN, nb)), 0.0, NEG_INF)

    blocks_per_tile = TK // SLC_BLOCK

    def sel_bias(j, tile):
        return jnp.concatenate(
            [jnp.broadcast_to(sel_scr[j, pl.ds(tile * blocks_per_tile + i, 1), :], (SLC_BLOCK, TQ))
             for i in range(blocks_per_tile)], axis=0)

    krow = lax.broadcasted_iota(jnp.int32, (TK, TQ), 0)
    qcol = lax.broadcasted_iota(jnp.int32, (TK, TQ), 1)
    diag = pl.ds(pl.multiple_of(qs, TK), TK)
    m_d, l_d = [], []
    for j in heads:
        bias_d = jnp.where(krow <= qcol, sel_bias(j, qi), NEG_INF)
        s_d = _dot_nt(ks_ref[diag, hcol(j)], q4[j]) + lanes4(bias_d)
        m = jnp.max(s_d, axis=0, keepdims=True)
        p_d = jnp.exp2(s_d - m)
        m_d.append(m)
        l_d.append(jnp.sum(p_d, axis=0, keepdims=True))
        p_scr[j] = p_d.astype(BF16)

    def kv_step(t, carry):
        ms, ls, accs, voff = carry
        prev = pl.ds(pl.multiple_of(voff, TK), TK)
        koff = pl.multiple_of(t * TK, TK)
        ms_n, ls_n, accs_n = [], [], []
        for j in heads:
            pv = _dot_tn(vs_ref[prev, hcol(j)], p_scr[j])
            s = _dot_nt(ks_ref[pl.ds(koff, TK), hcol(j)], q4[j]) + lanes4(sel_bias(j, t))
            m_new = jnp.maximum(ms[j], jnp.max(s, axis=0, keepdims=True))
            alpha = jnp.exp2(ms[j] - m_new)
            p = jnp.exp2(s - m_new)
            ms_n.append(m_new)
            ls_n.append(alpha * ls[j] + jnp.sum(p, axis=0, keepdims=True))
            accs_n.append(alpha * (accs[j] + pv))
            p_scr[j] = p.astype(BF16)
        return tuple(ms_n), tuple(ls_n), tuple(accs_n), koff

    acc0 = tuple(jnp.zeros((HD, R), F32) for _ in heads)
    _, l_s, acc_s, voff = lax.fori_loop(0, qi, kv_step, (tuple(m_d), tuple(l_d), acc0, qs))
    last = pl.ds(pl.multiple_of(voff, TK), TK)
    o_s = [(acc_s[j] + _dot_tn(vs_ref[last, hcol(j)], p_scr[j])) / l_s[j] for j in heads]

    span = WINDOW + TQ
    w0 = pl.multiple_of(jnp.maximum(qs - WINDOW, 0), TQ)
    kpos = w0 + lax.broadcasted_iota(jnp.int32, (span, TQ), 0)
    tq_w = qs + lax.broadcasted_iota(jnp.int32, (span, TQ), 1)
    bias_w = lanes4(jnp.where(kpos <= tq_w, jnp.where(kpos > tq_w - WINDOW, 0.0, NEG_INF), NEG_INF))
    o_w = []
    for j in heads:
        s_w = _dot_nt(kw_ref[pl.ds(w0, span), hcol(j)], q4[j]) + bias_w
        e_w = jnp.exp2(s_w - jnp.max(s_w, axis=0, keepdims=True))
        o_w.append(_dot_tn(vw_ref[pl.ds(w0, span), hcol(j)], e_w.astype(BF16))
                   / jnp.sum(e_w, axis=0, keepdims=True))

    gate_scr[...] = _sigmoid(sm_ref[...]).T
    for j in heads:
        def gate_row(branch):
            base = SMALL_NGATE + branch * NSA_HEADS + (hk0 + j) * G
            return jnp.concatenate([gate_scr[pl.ds(base + g, 1), :] for g in range(G)], axis=1)

        o = gate_row(0) * o_c[j] + gate_row(1) * o_s[j] + gate_row(2) * o_w[j]
        for g in range(G):
            o_ref[:, (j * G + g) * HD:(j * G + g + 1) * HD] = o[:, g * TQ:(g + 1) * TQ].T.astype(o_ref.dtype)


def _nsa(z3, zs3, cmp, mcs_t):
    bsz, s, _ = z3.shape
    hkv, hd, tq, hps = NSA_KV_HEADS, NSA_HD, NSA_TQ, NSA_HPS
    qw = hps * NSA_GROUP * hd
    kw = hps * hd
    nc = cmp.shape[3]
    nb = mcs_t.shape[0]

    def kv_spec(i):
        return pl.BlockSpec((None, s, kw), lambda b, h, q: (b, 0, (Z_NKV + i * hkv * hd) // kw + h))

    return pl.pallas_call(
        _nsa_kernel,
        grid=(bsz, hkv // hps, s // tq),
        in_specs=[
            pl.BlockSpec((None, tq, qw), lambda b, h, q: (b, q, Z_NQ // qw + h)),
            pl.BlockSpec((None, tq, LANES), lambda b, h, q: (b, q, 0)),
            pl.BlockSpec((None, None, hps, nc, hd), lambda b, h, q: (0, b, h, 0, 0)),
            pl.BlockSpec((None, None, hps, nc, hd), lambda b, h, q: (1, b, h, 0, 0)),
            kv_spec(2), kv_spec(3), kv_spec(4), kv_spec(5),
            pl.BlockSpec(mcs_t.shape, lambda b, h, q: (0, 0)),
        ],
        out_specs=pl.BlockSpec((None, tq, qw), lambda b, h, q: (b, q, h)),
        out_shape=jax.ShapeDtypeStruct((bsz, s, NSA_HEADS * hd), BF16),
        scratch_shapes=[pltpu.VMEM((hps, nb, tq), F32), pltpu.VMEM((LANES, tq), F32),
                        pltpu.VMEM((hps, NSA_TK, NSA_GROUP * tq), BF16)],
        compiler_params=_params(("parallel", "parallel", "arbitrary")),
        name="nsa_attn",
    )(z3, zs3, cmp, cmp, z3, z3, z3, z3, mcs_t)


def _merge_kernel(og_ref, on_ref, wg_ref, wn_ref, ga_ref, gb_ref, wo_ref, o_ref, wo_bf_ref, wg_scr, wn_scr):
    wo_bf_ref[...] = wo_ref[...].astype(BF16)

    @pl.when(pl.program_id(1) == 0)
    def _():
        wg_scr[...] = wg_ref[...].astype(BF16)
        wn_scr[...] = wn_ref[...].astype(BF16)

    a = _dot(og_ref[...], wg_scr[...])
    b = _dot(on_ref[...], wn_scr[...])
    ga = _sigmoid(ga_ref[...].astype(F32))
    gb = _sigmoid(gb_ref[...].astype(F32))
    o_ref[...] = (ga * a + gb * b).astype(o_ref.dtype)


def _merge(o_gla, o_nsa, w_pg, w_pn, z2, w_out, tm=512, tn=1024):
    m = o_gla.shape[0]
    n_i = m // tm
    slab = D_MODEL // ((D_MODEL // tn) * n_i)
    assert slab % 16 == 0 and slab * (D_MODEL // tn) * n_i == D_MODEL
    once = pl.Buffered(1)
    return pl.pallas_call(
        _merge_kernel,
        grid=(D_MODEL // tn, m // tm),
        in_specs=[
            pl.BlockSpec((tm, GLA_DV), lambda j, i: (i, 0)),
            pl.BlockSpec((tm, NSA_HEADS * NSA_HD), lambda j, i: (i, 0)),
            pl.BlockSpec((GLA_DV, tn), lambda j, i: (0, j), pipeline_mode=once),
            pl.BlockSpec((NSA_HEADS * NSA_HD, tn), lambda j, i: (0, j), pipeline_mode=once),
            pl.BlockSpec((tm, tn), lambda j, i: (i, Z_MG // tn + j)),
            pl.BlockSpec((tm, tn), lambda j, i: (i, (Z_MG + D_MODEL) // tn + j)),
            pl.BlockSpec((slab, D_MODEL), lambda j, i: (j * n_i + i, 0)),
        ],
        out_specs=[pl.BlockSpec((tm, tn), lambda j, i: (i, j)),
                   pl.BlockSpec((slab, D_MODEL), lambda j, i: (j * n_i + i, 0))],
        out_shape=[jax.ShapeDtypeStruct((m, D_MODEL), BF16),
                   jax.ShapeDtypeStruct((D_MODEL, D_MODEL), BF16)],
        scratch_shapes=[pltpu.VMEM((GLA_DV, tn), BF16), pltpu.VMEM((NSA_HEADS * NSA_HD, tn), BF16)],
        compiler_params=_params(("parallel", "arbitrary")),
        name="merge_proj",
    )(o_gla, o_nsa, w_pg, w_pn, z2, z2, w_out)


def _out_proj_kernel(mg_ref, w_ref, x_ref, g_ref, x1_ref, h_ref):
    x1 = x_ref[...] + _dot(mg_ref[...], w_ref[...])
    x1_ref[...] = x1
    ms = jnp.mean(x1 * x1, axis=-1, keepdims=True)
    h_ref[...] = (x1 * lax.rsqrt(ms + RMS_EPS) * g_ref[...]).astype(h_ref.dtype)


def _out_proj(merged, w_out, x2, g_ffn, tm=512):
    m = merged.shape[0]
    return pl.pallas_call(
        _out_proj_kernel,
        grid=(m // tm,),
        in_specs=[
            pl.BlockSpec((tm, D_MODEL), lambda i: (i, 0)),
            pl.BlockSpec((D_MODEL, D_MODEL), lambda i: (0, 0)),
            pl.BlockSpec((tm, D_MODEL), lambda i: (i, 0)),
            pl.BlockSpec((1, D_MODEL), lambda i: (0, 0)),
        ],
        out_specs=[
            pl.BlockSpec((tm, D_MODEL), lambda i: (i, 0)),
            pl.BlockSpec((tm, D_MODEL), lambda i: (i, 0)),
        ],
        out_shape=[jax.ShapeDtypeStruct((m, D_MODEL), F32),
                   jax.ShapeDtypeStruct((m, D_MODEL), BF16)],
        compiler_params=_params(("parallel",)),
        name="out_proj",
    )(merged, w_out, x2, g_ffn)


def _ffn_up_kernel(h_ref, wg_ref, wu_ref, wd_ref, o_ref, wd_bf_ref, wg_scr, wu_scr):
    wd_bf_ref[...] = wd_ref[...].astype(BF16)

    @pl.when(pl.program_id(1) == 0)
    def _():
        wg_scr[...] = wg_ref[...].astype(BF16)
        wu_scr[...] = wu_ref[...].astype(BF16)

    h = h_ref[...]
    g = _dot(h, wg_scr[...])
    u = _dot(h, wu_scr[...])
    o_ref[...] = (g * _sigmoid(g) * u).astype(o_ref.dtype)


def _ffn_up(h2, w_gate, w_up, w_down, tm=1024, tf=512):
    m = h2.shape[0]
    n_i = m // tm
    slab = D_FF // ((D_FF // tf) * n_i)
    assert slab % 16 == 0 and slab * (D_FF // tf) * n_i == D_FF
    return pl.pallas_call(
        _ffn_up_kernel,
        grid=(D_FF // tf, m // tm),
        in_specs=[
            pl.BlockSpec((tm, D_MODEL), lambda j, i: (i, 0)),
            pl.BlockSpec((D_MODEL, tf), lambda j, i: (0, j)),
            pl.BlockSpec((D_MODEL, tf), lambda j, i: (0, j)),
            pl.BlockSpec((slab, D_MODEL), lambda j, i: (j * n_i + i, 0)),
        ],
        out_specs=[pl.BlockSpec((tm, tf), lambda j, i: (i, j)),
                   pl.BlockSpec((slab, D_MODEL), lambda j, i: (j * n_i + i, 0))],
        out_shape=[jax.ShapeDtypeStruct((m, D_FF), BF16),
                   jax.ShapeDtypeStruct((D_FF, D_MODEL), BF16)],
        scratch_shapes=[pltpu.VMEM((D_MODEL, tf), BF16), pltpu.VMEM((D_MODEL, tf), BF16)],
        compiler_params=_params(("parallel", "arbitrary")),
        name="ffn_up",
    )(h2, w_gate, w_up, w_down)


def _ffn_down_kernel(a_ref, w_ref, x1_ref, g_ref, o_ref, *, final_norm):
    k = pl.program_id(1)

    @pl.when(k == 0)
    def _():
        o_ref[...] = x1_ref[...]

    o_ref[...] += _dot(a_ref[...], w_ref[...])

    if final_norm:
        @pl.when(k == pl.num_programs(1) - 1)
        def _():
            y = o_ref[...]
            ms = jnp.mean(y * y, axis=-1, keepdims=True)
            o_ref[...] = y * lax.rsqrt(ms + RMS_EPS) * g_ref[...]


def _ffn_down(act, w_down, x1, g_final, final_norm, tm=1024, tk=1408):
    m = act.shape[0]
    return pl.pallas_call(
        functools.partial(_ffn_down_kernel, final_norm=final_norm),
        grid=(m // tm, D_FF // tk),
        in_specs=[
            pl.BlockSpec((tm, tk), lambda i, k: (i, k)),
            pl.BlockSpec((tk, D_MODEL), lambda i, k: (k, 0)),
            pl.BlockSpec((tm, D_MODEL), lambda i, k: (i, 0)),
            pl.BlockSpec((1, D_MODEL), lambda i, k: (0, 0)),
        ],
        out_specs=pl.BlockSpec((tm, D_MODEL), lambda i, k: (i, 0)),
        out_shape=jax.ShapeDtypeStruct((m, D_MODEL), F32),
        compiler_params=_params(("parallel", "arbitrary")),
        name="ffn_down",
    )(act, w_down, x1, g_final)


def _cmp_to_slc_weights_t(seq):
    n_cmp = seq // CMP_STRIDE - 1
    n_slc = seq // SLC_BLOCK
    c0 = np.arange(n_cmp)[:, None] * CMP_STRIDE
    s0 = np.arange(n_slc)[None, :] * SLC_BLOCK
    ov = np.clip(np.minimum(c0 + CMP_LEN, s0 + SLC_BLOCK) - np.maximum(c0, s0), 0, None)
    m_cs = np.zeros((n_cmp + 1, n_slc), np.float32)
    m_cs[:n_cmp] = ov / CMP_LEN
    return jnp.asarray(m_cs.T, BF16)


def kernel(x, g_mix, w_in, w_gla_gate, b_gla_gate, g_gla_out, pe_cmp_k, w_cmp_k1, w_cmp_k2,
           pe_cmp_v, w_cmp_v1, w_cmp_v2, w_proj_gla, w_proj_nsa, w_out, g_ffn,
           w_ffn_gate, w_ffn_up, w_ffn_down, g_final):
    bsz, seq, d = x.shape
    m = bsz * seq
    depth = w_in.shape[0]
    mcs_t = _cmp_to_slc_weights_t(seq)
    xc = x.reshape(m, d)
    for l in range(depth):
        w_t = jnp.swapaxes(w_in[l], 0, 1)
        h, zs = _norm_small(xc, g_mix[l][None], w_t)
        z2 = _in_proj(h, w_t)
        z3 = z2.reshape(bsz, seq, Z_SMALL)
        zs3 = zs.reshape(bsz, seq, LANES)

        w_gate_pad = jnp.zeros((LANES, GLA_DK), F32).at[SMALL_GLR:SMALL_GLR + GLA_GATE_RANK].set(w_gla_gate[l])
        o_gla = _gla(z3, zs3, w_gate_pad.astype(BF16), b_gla_gate[l][None], g_gla_out[l][None])

        cmp = _compress(z3, jnp.stack([pe_cmp_k[l], pe_cmp_v[l]]),
                        jnp.stack([w_cmp_k1[l], w_cmp_v1[l]]), jnp.stack([w_cmp_k2[l], w_cmp_v2[l]]))
        o_nsa = _nsa(z3, zs3, cmp, mcs_t)

        merged, w_out_bf = _merge(o_gla.reshape(m, GLA_DV), o_nsa.reshape(m, NSA_HEADS * NSA_HD),
                                  w_proj_gla[l], w_proj_nsa[l], z2, w_out[l])
        x1, h2 = _out_proj(merged, w_out_bf, xc, g_ffn[l][None])
        act, w_down_bf = _ffn_up(h2, w_ffn_gate[l], w_ffn_up[l], w_ffn_down[l])
        xc = _ffn_down(act, w_down_bf, x1, g_final[None],
                       final_norm=(l == depth - 1))
    return xc.reshape(bsz, seq, d)
```

```python
import functools

import numpy as np
import jax
import jax.numpy as jnp
from jax import lax
from jax.experimental import pallas as pl
from jax.experimental.pallas import tpu as pltpu

F32 = jnp.float32
BF16 = jnp.bfloat16

D_MODEL = 2048
GLA_HEADS = 4
GLA_DK = D_MODEL // 2
GLA_DV = D_MODEL
GLA_HK = GLA_DK // GLA_HEADS
GLA_HV = GLA_DV // GLA_HEADS
GLA_GATE_RANK = 16
GLA_TAU = 16.0
GLA_CHUNK = 64

NSA_HEADS = 16
NSA_KV_HEADS = 4
NSA_GROUP = NSA_HEADS // NSA_KV_HEADS
NSA_HD = D_MODEL // NSA_HEADS
CMP_STRIDE = 16
CMP_LEN = 2 * CMP_STRIDE
CMP_HIDDEN = 256
SLC_BLOCK = 64
SLC_TOPN = 16
WINDOW = 512
D_FF = ((8 * D_MODEL // 3 + 255) // 256) * 256

IN_SPLIT_SIZES = (GLA_DK, GLA_DK, GLA_DV, GLA_DV, GLA_GATE_RANK,
                  NSA_HEADS * NSA_HD, 6 * NSA_KV_HEADS * NSA_HD, 3 * NSA_HEADS,
                  2 * D_MODEL)

NEG_INF = -1e30
RMS_EPS = 1e-6

Z_GQ = 0
Z_GK = Z_GQ + GLA_DK
Z_GV = Z_GK + GLA_DK
Z_GR = Z_GV + GLA_DV
Z_NQ = Z_GR + GLA_DV
Z_NKV = Z_NQ + NSA_HEADS * NSA_HD
Z_MG = Z_NKV + 6 * NSA_KV_HEADS * NSA_HD
Z_SMALL = Z_MG + 2 * D_MODEL
LANES = 128
SMALL_GLR = 0
SMALL_NGATE = GLA_GATE_RANK

VMEM_LIMIT = 56 * 1024 * 1024


def _dot(a, b):
    return jnp.dot(a, b, preferred_element_type=F32)


def _dot_nt(a, b):
    return lax.dot_general(a, b, (((1,), (1,)), ((), ())), preferred_element_type=F32)


def _dot_tn(a, b):
    return lax.dot_general(a, b, (((0,), (0,)), ((), ())), preferred_element_type=F32)


def _sigmoid(x):
    return 1.0 / (1.0 + jnp.exp(-x))


def _split2(x):
    hi = x.astype(BF16)
    lo = (x - hi.astype(F32)).astype(BF16)
    return hi, lo


def _split3(x):
    hi = x.astype(BF16)
    r1 = x - hi.astype(F32)
    mid = r1.astype(BF16)
    lo = (r1 - mid.astype(F32)).astype(BF16)
    return hi, mid, lo


def _params(sem, **kw):
    return pltpu.CompilerParams(dimension_semantics=sem, vmem_limit_bytes=VMEM_LIMIT, **kw)


_IN_OFFS = np.cumsum((0,) + IN_SPLIT_SIZES)
COL_GLR = int(_IN_OFFS[4])
COL_NGATE = int(_IN_OFFS[7])
assert COL_GLR % LANES == SMALL_GLR and COL_NGATE % LANES == SMALL_NGATE
assert SMALL_GLR + GLA_GATE_RANK == SMALL_NGATE and SMALL_NGATE + 3 * NSA_HEADS <= LANES


def _norm_kernel(x_ref, g_ref, wa_ref, wb_ref, h_ref, zs_ref):
    xf = x_ref[...]
    ms = jnp.mean(xf * xf, axis=-1, keepdims=True)
    h = (xf * lax.rsqrt(ms + RMS_EPS) * g_ref[...]).astype(BF16)
    h_ref[...] = h
    row = lax.broadcasted_iota(jnp.int32, wa_ref.shape, 0)
    w_small = jnp.where(row < SMALL_NGATE, wa_ref[...],
                        jnp.where(row < SMALL_NGATE + 3 * NSA_HEADS, wb_ref[...], 0.0))
    zs_ref[...] = _dot_nt(h, w_small.astype(BF16))


def _norm_small(x2, g, w_t, tm=512):
    m = x2.shape[0]
    return pl.pallas_call(
        _norm_kernel,
        grid=(m // tm,),
        in_specs=[
            pl.BlockSpec((tm, D_MODEL), lambda i: (i, 0)),
            pl.BlockSpec((1, D_MODEL), lambda i: (0, 0)),
            pl.BlockSpec((LANES, D_MODEL), lambda i: (COL_GLR // LANES, 0)),
            pl.BlockSpec((LANES, D_MODEL), lambda i: (COL_NGATE // LANES, 0)),
        ],
        out_specs=[pl.BlockSpec((tm, D_MODEL), lambda i: (i, 0)),
                   pl.BlockSpec((tm, LANES), lambda i: (i, 0))],
        out_shape=[jax.ShapeDtypeStruct((m, D_MODEL), BF16),
                   jax.ShapeDtypeStruct((m, LANES), F32)],
        compiler_params=_params(("parallel",)),
        name="norm_small",
    )(x2, g, w_t, w_t)


IN_TN = 1024
IN_SHIFT_B = GLA_GATE_RANK
IN_SHIFT_C = GLA_GATE_RANK + 3 * NSA_HEADS
IN_BLOCKS_A = Z_NQ // IN_TN
IN_BLOCKS_AB = Z_MG // IN_TN
IN_ROWS = 48
assert (IN_TN - IN_SHIFT_B) % IN_ROWS == 0 and (IN_TN - IN_SHIFT_C) % IN_ROWS == 0


def _in_proj_kernel(h_ref, wm_ref, we_ref, z_ref, w_scr):
    j = pl.program_id(0)

    def build(shift):
        if shift == 0:
            w_scr[...] = wm_ref[...].astype(BF16)
            return

        def body(r, carry):
            dst = pl.ds(pl.multiple_of(r * IN_ROWS, 16), IN_ROWS)
            src = pl.ds(pl.multiple_of(r * IN_ROWS + shift, 8), IN_ROWS)
            w_scr[dst, :] = wm_ref[src, :].astype(BF16)
            return carry
        lax.fori_loop(0, (IN_TN - shift) // IN_ROWS, body, 0)
        w_scr[IN_TN - shift:, :] = we_ref[:shift, :].astype(BF16)

    first = pl.program_id(1) == 0

    @pl.when(first & (j < IN_BLOCKS_A))
    def _():
        build(0)

    @pl.when(first & (j >= IN_BLOCKS_A) & (j < IN_BLOCKS_AB))
    def _():
        build(IN_SHIFT_B)

    @pl.when(first & (j >= IN_BLOCKS_AB))
    def _():
        build(IN_SHIFT_C)

    z_ref[...] = _dot_nt(h_ref[...], w_scr[...]).astype(z_ref.dtype)


def _in_proj(h, w_t, tm=1024):
    m = h.shape[0]
    tn = IN_TN
    return pl.pallas_call(
        _in_proj_kernel,
        grid=(Z_SMALL // tn, m // tm),
        in_specs=[
            pl.BlockSpec((tm, D_MODEL), lambda j, i: (i, 0)),
            pl.BlockSpec((tn, D_MODEL), lambda j, i: (j, 0)),
            pl.BlockSpec((LANES, D_MODEL), lambda j, i: ((j + 1) * (tn // LANES), 0)),
        ],
        out_specs=pl.BlockSpec((tm, tn), lambda j, i: (i, j)),
        out_shape=jax.ShapeDtypeStruct((m, Z_SMALL), BF16),
        scratch_shapes=[pltpu.VMEM((tn, D_MODEL), BF16)],
        compiler_params=_params(("parallel", "arbitrary")),
        name="in_proj",
    )(h, w_t, w_t)


GLA_HPS = 4


def _gla_kernel(q_ref, k_ref, v_ref, r_ref, sm_ref, wg_ref, bg_ref, gn_ref, o_ref,
                state_ref, *, n_chunks):
    @pl.when(pl.program_id(2) == 0)
    def _():
        state_ref[...] = jnp.zeros_like(state_ref)

    C = GLA_CHUNK
    HK, HV = GLA_HK, GLA_HV
    ts = n_chunks * C
    row = lax.broadcasted_iota(jnp.int32, (ts, ts), 0)
    col = lax.broadcasted_iota(jnp.int32, (ts, ts), 1)
    tri = jnp.where((row >= col) & ((row // C) == (col // C)), 1.0, 0.0).astype(BF16)
    crow = lax.broadcasted_iota(jnp.int32, (C, C), 0)
    ccol = lax.broadcasted_iota(jnp.int32, (C, C), 1)
    causal = crow >= ccol

    u = _dot(sm_ref[...].astype(BF16), wg_ref[...]) + bg_ref[...]
    log_a = (jnp.minimum(u, 0.0) - jnp.log(1.0 + jnp.exp(-jnp.abs(u)))) / GLA_TAU
    hi, lo = _split2(log_a)
    b_all = _dot(tri, hi) + _dot(tri, lo)
    gn = gn_ref[...]

    heads = range(GLA_HPS)
    kcol = [slice(j * HK, (j + 1) * HK) for j in heads]
    vcol = [slice(j * HV, (j + 1) * HV) for j in heads]
    b = [b_all[:, kcol[j]] for j in heads]
    k = [k_ref[:, kcol[j]].astype(F32) for j in heads]
    q_dec = [(q_ref[:, kcol[j]].astype(F32) * (HK ** -0.5) * jnp.exp(b[j])).astype(BF16) for j in heads]
    k_dec = [(k[j] * jnp.exp(-b[j])).astype(BF16) for j in heads]
    state = [state_ref[j] for j in heads]
    for c in range(n_chunks):
        rows = slice(c * C, (c + 1) * C)
        for j in heads:
            b_c = b[j][rows]
            b_last = b_c[C - 1:C, :]
            k_end = (k[j][rows] * jnp.exp(b_last - b_c)).astype(BF16)
            v_c = v_ref[rows, vcol[j]]
            attn = jnp.where(causal, _dot_nt(q_dec[j][rows], k_dec[j][rows]), 0.0).astype(BF16)
            o = _dot(attn, v_c) + _dot_nt(q_dec[j][rows], state[j].astype(BF16))
            state[j] = state[j] * jnp.exp(b_last) + _dot_tn(v_c, k_end)
            ms = jnp.mean(o * o, axis=-1, keepdims=True)
            y = o * lax.rsqrt(ms + RMS_EPS) * gn
            r = r_ref[rows, vcol[j]].astype(F32)
            o_ref[rows, vcol[j]] = (y * (r * _sigmoid(r))).astype(o_ref.dtype)
    for j in heads:
        state_ref[j] = state[j]


def _gla(z3, zs3, w_gate_pad, b_gate, g_out, ts=512):
    bsz, s, _ = z3.shape
    qb, vb = GLA_HPS * GLA_HK, GLA_HPS * GLA_HV
    return pl.pallas_call(
        functools.partial(_gla_kernel, n_chunks=ts // GLA_CHUNK),
        grid=(bsz, GLA_HEADS // GLA_HPS, s // ts),
        in_specs=[
            pl.BlockSpec((None, ts, qb), lambda b, h, i: (b, i, Z_GQ // qb + h)),
            pl.BlockSpec((None, ts, qb), lambda b, h, i: (b, i, Z_GK // qb + h)),
            pl.BlockSpec((None, ts, vb), lambda b, h, i: (b, i, Z_GV // vb + h)),
            pl.BlockSpec((None, ts, vb), lambda b, h, i: (b, i, Z_GR // vb + h)),
            pl.BlockSpec((None, ts, LANES), lambda b, h, i: (b, i, 0)),
            pl.BlockSpec((LANES, qb), lambda b, h, i: (0, h)),
            pl.BlockSpec((1, qb), lambda b, h, i: (0, h)),
            pl.BlockSpec((1, GLA_HV), lambda b, h, i: (0, 0)),
        ],
        out_specs=pl.BlockSpec((None, ts, vb), lambda b, h, i: (b, i, h)),
        out_shape=jax.ShapeDtypeStruct((bsz, s, GLA_DV), BF16),
        scratch_shapes=[pltpu.VMEM((GLA_HPS, GLA_HV, GLA_HK), F32)],
        compiler_params=_params(("parallel", "parallel", "arbitrary")),
        name="gla",
    )(z3, z3, z3, z3, zs3, w_gate_pad, b_gate, g_out)


def _cmp_kernel(t_ref, pe_ref, w1_ref, w2_ref, o_ref, tf_scr, lo_scr, hi_scr):
    hd = NSA_HD
    half = CMP_STRIDE * hd
    nchunk = lo_scr.shape[0] // NSA_KV_HEADS
    for h in range(NSA_KV_HEADS):
        tf_scr[h] = t_ref[:, h * hd:(h + 1) * hd].astype(F32)
        rows = slice(h * nchunk, (h + 1) * nchunk)
        for r in range(CMP_STRIDE):
            t_r = tf_scr.at[h][pl.ds(r, nchunk, stride=CMP_STRIDE), :]
            lo_scr[rows, r * hd:(r + 1) * hd] = (t_r + pe_ref[r:r + 1, :]).astype(BF16)
            hi_scr[rows, r * hd:(r + 1) * hd] = (t_r + pe_ref[CMP_STRIDE + r:CMP_STRIDE + r + 1, :]).astype(BF16)
    a = _dot(lo_scr[...], w1_ref[:half, :].astype(BF16))
    bm = _dot(hi_scr[...], w1_ref[half:, :].astype(BF16))
    pre = a + pltpu.roll(bm, bm.shape[0] - 1, 0)
    hid = (pre * _sigmoid(pre)).astype(BF16)
    out = _dot(hid, w2_ref[...].astype(BF16)).astype(o_ref.dtype)
    for h in range(NSA_KV_HEADS):
        o_ref[h] = out[h * nchunk:(h + 1) * nchunk]


def _compress(z3, pe, w1, w2):
    bsz, s, _ = z3.shape
    hkv, hd = NSA_KV_HEADS, NSA_HD
    nchunk = s // CMP_STRIDE
    width = CMP_STRIDE * hd
    return pl.pallas_call(
        _cmp_kernel,
        grid=(2, bsz),
        in_specs=[
            pl.BlockSpec((None, s, hkv * hd), lambda kv, b: (b, 0, Z_NKV // (hkv * hd) + kv)),
            pl.BlockSpec((None, CMP_LEN, hd), lambda kv, b: (kv, 0, 0)),
            pl.BlockSpec((None, 2 * width, CMP_HIDDEN), lambda kv, b: (kv, 0, 0)),
            pl.BlockSpec((None, CMP_HIDDEN, hd), lambda kv, b: (kv, 0, 0)),
        ],
        out_specs=pl.BlockSpec((None, None, hkv, nchunk, hd), lambda kv, b: (kv, b, 0, 0, 0)),
        out_shape=jax.ShapeDtypeStruct((2, bsz, hkv, nchunk, hd), BF16),
        scratch_shapes=[pltpu.VMEM((hkv, s, hd), F32),
                        pltpu.VMEM((hkv * nchunk, width), BF16),
                        pltpu.VMEM((hkv * nchunk, width), BF16)],
        compiler_params=_params(("parallel", "parallel")),
        name="nsa_compress",
    )(z3, pe, w1, w2)


NSA_TQ = 256
NSA_TK = NSA_TQ
LOG2E = 1.4426950408889634


NSA_HPS = 4


def _nsa_kernel(q_ref, sm_ref, kc_ref, vc_ref, ks_ref, vs_ref, kw_ref, vw_ref, mcs_ref,
                o_ref, sel_scr, gate_scr, p_scr, acc_scr):
    G, TQ, TK, HD = NSA_GROUP, NSA_TQ, NSA_TK, NSA_HD
    R = G * TQ
    heads = range(NSA_HPS)
    hk0 = pl.program_id(1) * NSA_HPS
    qi = pl.program_id(2)
    qs = qi * TQ

    def lanes4(x):
        return jnp.concatenate([x] * G, axis=1)

    def hcol(j):
        return slice(j * HD, (j + 1) * HD)

    q4 = []
    for j in heads:
        qj = jnp.concatenate([q_ref[:, (j * G + g) * HD:(j * G + g + 1) * HD] for g in range(G)], axis=0)
        q4.append((qj.astype(F32) * (HD ** -0.5 * LOG2E)).astype(BF16))

    nc = kc_ref.shape[1]
    tq_c = qs + lax.broadcasted_iota(jnp.int32, (nc, TQ), 1)
    cmp_end = lax.broadcasted_iota(jnp.int32, (nc, TQ), 0) * CMP_STRIDE + (CMP_LEN - 1)
    valid_c = lanes4(jnp.where(cmp_end <= tq_c, 1.0, 0.0)) > 0.5
    mcs = mcs_ref[...]
    nb = mcs.shape[0]
    blk = lax.broadcasted_iota(jnp.int32, (nb, TQ), 0)
    tq_b = qs + lax.broadcasted_iota(jnp.int32, (nb, TQ), 1)
    cur = tq_b // SLC_BLOCK
    forced = (blk == 0) | (blk == cur) | (blk == cur - 1)
    visible = blk * SLC_BLOCK <= tq_b
    o_c = []
    for j in heads:
        s_c = jnp.where(valid_c, _dot_nt(kc_ref[j], q4[j]), NEG_INF)
        e_c = jnp.where(valid_c, jnp.exp2(s_c - jnp.max(s_c, axis=0, keepdims=True)), 0.0)
        l_c = jnp.sum(e_c, axis=0, keepdims=True)
        p_c = e_c / jnp.where(l_c > 0.0, l_c, 1.0)
        o_c.append(_dot_tn(vc_ref[j], p_c.astype(BF16)))

        p_sum = p_c[:, 0:TQ]
        for g in range(1, G):
            p_sum = p_sum + p_c[:, g * TQ:(g + 1) * TQ]
        imp = sum(_dot(mcs, part) for part in _split3(p_sum))
        imp = jnp.where(forced, jnp.inf, imp)
        imp = jnp.where(visible, imp, -jnp.inf)
        rank = jnp.zeros((nb, TQ), F32)
        for kb in range(nb):
            rk = imp[kb:kb + 1, :]
            ahead = jnp.where(rk > imp, 1.0, jnp.where(rk == imp, jnp.where(blk > kb, 1.0, 0.0), 0.0))
            rank = rank + ahead
        sel_scr[j] = jnp.where(rank < float(min(SLC_TOPN, nb)), 0.0, NEG_INF)

    blocks_per_tile = TK // SLC_BLOCK

    def sel_bias(j, tile):
        return jnp.concatenate(
            [jnp.broadcast_to(sel_scr[j, pl.ds(tile * blocks_per_tile + i, 1), :], (SLC_BLOCK, TQ))
             for i in range(blocks_per_tile)], axis=0)

    krow = lax.broadcasted_iota(jnp.int32, (TK, TQ), 0)
    qcol = lax.broadcasted_iota(jnp.int32, (TK, TQ), 1)
    diag = pl.ds(pl.multiple_of(qs, TK), TK)
    m_d, l_d = [], []
    for j in heads:
        bias_d = jnp.where(krow <= qcol, sel_bias(j, qi), NEG_INF)
        s_d = _dot_nt(ks_ref[diag, hcol(j)], q4[j]) + lanes4(bias_d)
        m = jnp.max(s_d, axis=0, keepdims=True)
        p_d = jnp.exp2(s_d - m)
        m_d.append(m)
        l_d.append(jnp.sum(p_d, axis=0, keepdims=True))
        p_scr[j] = p_d.astype(BF16)

    def kv_step(t, carry):
        ms, ls, voff = carry
        prev = pl.ds(pl.multiple_of(voff, TK), TK)
        koff = pl.multiple_of(t * TK, TK)
        ms_n, ls_n = [], []
        for j in heads:
            pv = _dot_tn(vs_ref[prev, hcol(j)], p_scr[j])
            s = _dot_nt(ks_ref[pl.ds(koff, TK), hcol(j)], q4[j]) + lanes4(sel_bias(j, t))
            m_new = jnp.maximum(ms[j], jnp.max(s, axis=0, keepdims=True))
            alpha = jnp.exp2(ms[j] - m_new)
            p = jnp.exp2(s - m_new)
            ms_n.append(m_new)
            ls_n.append(alpha * ls[j] + jnp.sum(p, axis=0, keepdims=True))
            acc_scr[j] = alpha * (acc_scr[j] + pv)
            p_scr[j] = p.astype(BF16)
        return tuple(ms_n), tuple(ls_n), koff

    acc_scr[...] = jnp.zeros_like(acc_scr)
    _, l_s, voff = lax.fori_loop(0, qi, kv_step, (tuple(m_d), tuple(l_d), qs))
    last = pl.ds(pl.multiple_of(voff, TK), TK)
    o_s = [(acc_scr[j] + _dot_tn(vs_ref[last, hcol(j)], p_scr[j])) / l_s[j] for j in heads]

    span = WINDOW + TQ
    w0 = pl.multiple_of(jnp.maximum(qs - WINDOW, 0), TQ)
    kpos = w0 + lax.broadcasted_iota(jnp.int32, (span, TQ), 0)
    tq_w = qs + lax.broadcasted_iota(jnp.int32, (span, TQ), 1)
    bias_w = lanes4(jnp.where(kpos <= tq_w, jnp.where(kpos > tq_w - WINDOW, 0.0, NEG_INF), NEG_INF))
    o_w = []
    for j in heads:
        s_w = _dot_nt(kw_ref[pl.ds(w0, span), hcol(j)], q4[j]) + bias_w
        e_w = jnp.exp2(s_w - jnp.max(s_w, axis=0, keepdims=True))
        o_w.append(_dot_tn(vw_ref[pl.ds(w0, span), hcol(j)], e_w.astype(BF16))
                   / jnp.sum(e_w, axis=0, keepdims=True))

    gate_scr[...] = _sigmoid(sm_ref[...]).T
    for j in heads:
        def gate_row(branch):
            base = SMALL_NGATE + branch * NSA_HEADS + (hk0 + j) * G
            return jnp.concatenate([gate_scr[pl.ds(base + g, 1), :] for g in range(G)], axis=1)

        o = gate_row(0) * o_c[j] + gate_row(1) * o_s[j] + gate_row(2) * o_w[j]
        for g in range(G):
            o_ref[:, (j * G + g) * HD:(j * G + g + 1) * HD] = o[:, g * TQ:(g + 1) * TQ].T.astype(o_ref.dtype)


def _nsa(z3, zs3, cmp, mcs_t):
    bsz, s, _ = z3.shape
    hkv, hd, tq, hps = NSA_KV_HEADS, NSA_HD, NSA_TQ, NSA_HPS
    qw = hps * NSA_GROUP * hd
    kw = hps * hd
    nc = cmp.shape[3]
    nb = mcs_t.shape[0]

    def kv_spec(i):
        return pl.BlockSpec((None, s, kw), lambda b, h, q: (b, 0, (Z_NKV + i * hkv * hd) // kw + h))

    return pl.pallas_call(
        _nsa_kernel,
        grid=(bsz, hkv // hps, s // tq),
        in_specs=[
            pl.BlockSpec((None, tq, qw), lambda b, h, q: (b, q, Z_NQ // qw + h)),
            pl.BlockSpec((None, tq, LANES), lambda b, h, q: (b, q, 0)),
            pl.BlockSpec((None, None, hps, nc, hd), lambda b, h, q: (0, b, h, 0, 0)),
            pl.BlockSpec((None, None, hps, nc, hd), lambda b, h, q: (1, b, h, 0, 0)),
            kv_spec(2), kv_spec(3), kv_spec(4), kv_spec(5),
            pl.BlockSpec(mcs_t.shape, lambda b, h, q: (0, 0)),
        ],
        out_specs=pl.BlockSpec((None, tq, qw), lambda b, h, q: (b, q, h)),
        out_shape=jax.ShapeDtypeStruct((bsz, s, NSA_HEADS * hd), BF16),
        scratch_shapes=[pltpu.VMEM((hps, nb, tq), F32), pltpu.VMEM((LANES, tq), F32),
                        pltpu.VMEM((hps, NSA_TK, NSA_GROUP * tq), BF16),
                        pltpu.VMEM((hps, hd, NSA_GROUP * tq), F32)],
        compiler_params=_params(("parallel", "parallel", "arbitrary")),
        name="nsa_attn",
    )(z3, zs3, cmp, cmp, z3, z3, z3, z3, mcs_t)


def _merge_kernel(og_ref, on_ref, wg_ref, wn_ref, ga_ref, gb_ref, wo_ref, o_ref, wo_bf_ref, wg_scr, wn_scr):
    wo_bf_ref[...] = wo_ref[...].astype(BF16)

    @pl.when(pl.program_id(1) == 0)
    def _():
        wg_scr[...] = wg_ref[...].astype(BF16)
        wn_scr[...] = wn_ref[...].astype(BF16)

    a = _dot(og_ref[...], wg_scr[...])
    b = _dot(on_ref[...], wn_scr[...])
    ga = _sigmoid(ga_ref[...].astype(F32))
    gb = _sigmoid(gb_ref[...].astype(F32))
    o_ref[...] = (ga * a + gb * b).astype(o_ref.dtype)


def _merge(o_gla, o_nsa, w_pg, w_pn, z2, w_out, tm=512, tn=1024):
    m = o_gla.shape[0]
    n_i = m // tm
    slab = D_MODEL // ((D_MODEL // tn) * n_i)
    assert slab % 16 == 0 and slab * (D_MODEL // tn) * n_i == D_MODEL
    once = pl.Buffered(1)
    return pl.pallas_call(
        _merge_kernel,
        grid=(D_MODEL // tn, m // tm),
        in_specs=[
            pl.BlockSpec((tm, GLA_DV), lambda j, i: (i, 0)),
            pl.BlockSpec((tm, NSA_HEADS * NSA_HD), lambda j, i: (i, 0)),
            pl.BlockSpec((GLA_DV, tn), lambda j, i: (0, j), pipeline_mode=once),
            pl.BlockSpec((NSA_HEADS * NSA_HD, tn), lambda j, i: (0, j), pipeline_mode=once),
            pl.BlockSpec((tm, tn), lambda j, i: (i, Z_MG // tn + j)),
            pl.BlockSpec((tm, tn), lambda j, i: (i, (Z_MG + D_MODEL) // tn + j)),
            pl.BlockSpec((slab, D_MODEL), lambda j, i: (j * n_i + i, 0)),
        ],
        out_specs=[pl.BlockSpec((tm, tn), lambda j, i: (i, j)),
                   pl.BlockSpec((slab, D_MODEL), lambda j, i: (j * n_i + i, 0))],
        out_shape=[jax.ShapeDtypeStruct((m, D_MODEL), BF16),
                   jax.ShapeDtypeStruct((D_MODEL, D_MODEL), BF16)],
        scratch_shapes=[pltpu.VMEM((GLA_DV, tn), BF16), pltpu.VMEM((NSA_HEADS * NSA_HD, tn), BF16)],
        compiler_params=_params(("parallel", "arbitrary")),
        name="merge_proj",
    )(o_gla, o_nsa, w_pg, w_pn, z2, z2, w_out)


def _out_proj_kernel(mg_ref, w_ref, x_ref, g_ref, x1_ref, h_ref):
    x1 = x_ref[...] + _dot(mg_ref[...], w_ref[...])
    x1_ref[...] = x1
    ms = jnp.mean(x1 * x1, axis=-1, keepdims=True)
    h_ref[...] = (x1 * lax.rsqrt(ms + RMS_EPS) * g_ref[...]).astype(h_ref.dtype)


def _out_proj(merged, w_out, x2, g_ffn, tm=512):
    m = merged.shape[0]
    return pl.pallas_call(
        _out_proj_kernel,
        grid=(m // tm,),
        in_specs=[
            pl.BlockSpec((tm, D_MODEL), lambda i: (i, 0)),
            pl.BlockSpec((D_MODEL, D_MODEL), lambda i: (0, 0)),
            pl.BlockSpec((tm, D_MODEL), lambda i: (i, 0)),
            pl.BlockSpec((1, D_MODEL), lambda i: (0, 0)),
        ],
        out_specs=[
            pl.BlockSpec((tm, D_MODEL), lambda i: (i, 0)),
            pl.BlockSpec((tm, D_MODEL), lambda i: (i, 0)),
        ],
        out_shape=[jax.ShapeDtypeStruct((m, D_MODEL), F32),
                   jax.ShapeDtypeStruct((m, D_MODEL), BF16)],
        compiler_params=_params(("parallel",)),
        name="out_proj",
    )(merged, w_out, x2, g_ffn)


def _ffn_up_kernel(h_ref, wg_ref, wu_ref, wd_ref, o_ref, wd_bf_ref, wg_scr, wu_scr):
    wd_bf_ref[...] = wd_ref[...].astype(BF16)

    @pl.when(pl.program_id(1) == 0)
    def _():
        wg_scr[...] = wg_ref[...].astype(BF16)
        wu_scr[...] = wu_ref[...].astype(BF16)

    h = h_ref[...]
    g = _dot(h, wg_scr[...])
    u = _dot(h, wu_scr[...])
    o_ref[...] = (g * _sigmoid(g) * u).astype(o_ref.dtype)


def _ffn_up(h2, w_gate, w_up, w_down, tm=1024, tf=512):
    m = h2.shape[0]
    n_i = m // tm
    slab = D_FF // ((D_FF // tf) * n_i)
    assert slab % 16 == 0 and slab * (D_FF // tf) * n_i == D_FF
    return pl.pallas_call(
        _ffn_up_kernel,
        grid=(D_FF // tf, m // tm),
        in_specs=[
            pl.BlockSpec((tm, D_MODEL), lambda j, i: (i, 0)),
            pl.BlockSpec((D_MODEL, tf), lambda j, i: (0, j)),
            pl.BlockSpec((D_MODEL, tf), lambda j, i: (0, j)),
            pl.BlockSpec((slab, D_MODEL), lambda j, i: (j * n_i + i, 0)),
        ],
        out_specs=[pl.BlockSpec((tm, tf), lambda j, i: (i, j)),
                   pl.BlockSpec((slab, D_MODEL), lambda j, i: (j * n_i + i, 0))],
        out_shape=[jax.ShapeDtypeStruct((m, D_FF), BF16),
                   jax.ShapeDtypeStruct((D_FF, D_MODEL), BF16)],
        scratch_shapes=[pltpu.VMEM((D_MODEL, tf), BF16), pltpu.VMEM((D_MODEL, tf), BF16)],
        compiler_params=_params(("parallel", "arbitrary")),
        name="ffn_up",
    )(h2, w_gate, w_up, w_down)


def _ffn_down_kernel(a_ref, w_ref, x1_ref, g_ref, o_ref, *, final_norm):
    k = pl.program_id(1)

    @pl.when(k == 0)
    def _():
        o_ref[...] = x1_ref[...]

    o_ref[...] += _dot(a_ref[...], w_ref[...])

    if final_norm:
        @pl.when(k == pl.num_programs(1) - 1)
        def _():
            y = o_ref[...]
            ms = jnp.mean(y * y, axis=-1, keepdims=True)
            o_ref[...] = y * lax.rsqrt(ms + RMS_EPS) * g_ref[...]


def _ffn_down(act, w_down, x1, g_final, final_norm, tm=1024, tk=1408):
    m = act.shape[0]
    return pl.pallas_call(
        functools.partial(_ffn_down_kernel, final_norm=final_norm),
        grid=(m // tm, D_FF // tk),
        in_specs=[
            pl.BlockSpec((tm, tk), lambda i, k: (i, k)),
            pl.BlockSpec((tk, D_MODEL), lambda i, k: (k, 0)),
            pl.BlockSpec((tm, D_MODEL), lambda i, k: (i, 0)),
            pl.BlockSpec((1, D_MODEL), lambda i, k: (0, 0)),
        ],
        out_specs=pl.BlockSpec((tm, D_MODEL), lambda i, k: (i, 0)),
        out_shape=jax.ShapeDtypeStruct((m, D_MODEL), F32),
        compiler_params=_params(("parallel", "arbitrary")),
        name="ffn_down",
    )(act, w_down, x1, g_final)


def _cmp_to_slc_weights_t(seq):
    n_cmp = seq // CMP_STRIDE - 1
    n_slc = seq // SLC_BLOCK
    c0 = np.arange(n_cmp)[:, None] * CMP_STRIDE
    s0 = np.arange(n_slc)[None, :] * SLC_BLOCK
    ov = np.clip(np.minimum(c0 + CMP_LEN, s0 + SLC_BLOCK) - np.maximum(c0, s0), 0, None)
    m_cs = np.zeros((n_cmp + 1, n_slc), np.float32)
    m_cs[:n_cmp] = ov / CMP_LEN
    return jnp.asarray(m_cs.T, BF16)


def kernel(x, g_mix, w_in, w_gla_gate, b_gla_gate, g_gla_out, pe_cmp_k, w_cmp_k1, w_cmp_k2,
           pe_cmp_v, w_cmp_v1, w_cmp_v2, w_proj_gla, w_proj_nsa, w_out, g_ffn,
           w_ffn_gate, w_ffn_up, w_ffn_down, g_final):
    bsz, seq, d = x.shape
    m = bsz * seq
    depth = w_in.shape[0]
    mcs_t = _cmp_to_slc_weights_t(seq)
    xc = x.reshape(m, d)
    for l in range(depth):
        w_t = jnp.swapaxes(w_in[l], 0, 1)
        h, zs = _norm_small(xc, g_mix[l][None], w_t)
        z2 = _in_proj(h, w_t)
        z3 = z2.reshape(bsz, seq, Z_SMALL)
        zs3 = zs.reshape(bsz, seq, LANES)

        w_gate_pad = jnp.zeros((LANES, GLA_DK), F32).at[SMALL_GLR:SMALL_GLR + GLA_GATE_RANK].set(w_gla_gate[l])
        o_gla = _gla(z3, zs3, w_gate_pad.astype(BF16), b_gla_gate[l][None], g_gla_out[l][None])

        cmp = _compress(z3, jnp.stack([pe_cmp_k[l], pe_cmp_v[l]]),
                        jnp.stack([w_cmp_k1[l], w_cmp_v1[l]]), jnp.stack([w_cmp_k2[l], w_cmp_v2[l]]))
        o_nsa = _nsa(z3, zs3, cmp, mcs_t)

        merged, w_out_bf = _merge(o_gla.reshape(m, GLA_DV), o_nsa.reshape(m, NSA_HEADS * NSA_HD),
                                  w_proj_gla[l], w_proj_nsa[l], z2, w_out[l])
        x1, h2 = _out_proj(merged, w_out_bf, xc, g_ffn[l][None])
        act, w_down_bf = _ffn_up(h2, w_ffn_gate[l], w_ffn_up[l], w_ffn_down[l])
        xc = _ffn_down(act, w_down_bf, x1, g_final[None],
                       final_norm=(l == depth - 1))
    return xc.reshape(bsz, seq, d)
```

```python
import functools

import numpy as np
import jax
import jax.numpy as jnp
from jax import lax
from jax.experimental import pallas as pl
from jax.experimental.pallas import tpu as pltpu

F32 = jnp.float32
BF16 = jnp.bfloat16

D_MODEL = 2048
GLA_HEADS = 4
GLA_DK = D_MODEL // 2
GLA_DV = D_MODEL
GLA_HK = GLA_DK // GLA_HEADS
GLA_HV = GLA_DV // GLA_HEADS
GLA_GATE_RANK = 16
GLA_TAU = 16.0
GLA_CHUNK = 64

NSA_HEADS = 16
NSA_KV_HEADS = 4
NSA_GROUP = NSA_HEADS // NSA_KV_HEADS
NSA_HD = D_MODEL // NSA_HEADS
CMP_STRIDE = 16
CMP_LEN = 2 * CMP_STRIDE
CMP_HIDDEN = 256
SLC_BLOCK = 64
SLC_TOPN = 16
WINDOW = 512
D_FF = ((8 * D_MODEL // 3 + 255) // 256) * 256

IN_SPLIT_SIZES = (GLA_DK, GLA_DK, GLA_DV, GLA_DV, GLA_GATE_RANK,
                  NSA_HEADS * NSA_HD, 6 * NSA_KV_HEADS * NSA_HD, 3 * NSA_HEADS,
                  2 * D_MODEL)

NEG_INF = -1e30
RMS_EPS = 1e-6

Z_GQ = 0
Z_GK = Z_GQ + GLA_DK
Z_GV = Z_GK + GLA_DK
Z_GR = Z_GV + GLA_DV
Z_NQ = Z_GR + GLA_DV
Z_NKV = Z_NQ + NSA_HEADS * NSA_HD
Z_MG = Z_NKV + 6 * NSA_KV_HEADS * NSA_HD
Z_SMALL = Z_MG + 2 * D_MODEL
LANES = 128
SMALL_GLR = 0
SMALL_NGATE = GLA_GATE_RANK

VMEM_LIMIT = 56 * 1024 * 1024


def _dot(a, b):
    return jnp.dot(a, b, preferred_element_type=F32)


def _dot_nt(a, b):
    return lax.dot_general(a, b, (((1,), (1,)), ((), ())), preferred_element_type=F32)


def _dot_tn(a, b):
    return lax.dot_general(a, b, (((0,), (0,)), ((), ())), preferred_element_type=F32)


def _sigmoid(x):
    return 1.0 / (1.0 + jnp.exp(-x))


def _split2(x):
    hi = x.astype(BF16)
    lo = (x - hi.astype(F32)).astype(BF16)
    return hi, lo


def _split3(x):
    hi = x.astype(BF16)
    r1 = x - hi.astype(F32)
    mid = r1.astype(BF16)
    lo = (r1 - mid.astype(F32)).astype(BF16)
    return hi, mid, lo


def _params(sem, **kw):
    return pltpu.CompilerParams(dimension_semantics=sem, vmem_limit_bytes=VMEM_LIMIT, **kw)


_IN_OFFS = np.cumsum((0,) + IN_SPLIT_SIZES)
COL_GLR = int(_IN_OFFS[4])
COL_NGATE = int(_IN_OFFS[7])
assert COL_GLR % LANES == SMALL_GLR and COL_NGATE % LANES == SMALL_NGATE
assert SMALL_GLR + GLA_GATE_RANK == SMALL_NGATE and SMALL_NGATE + 3 * NSA_HEADS <= LANES


def _norm_kernel(x_ref, g_ref, wa_ref, wb_ref, h_ref, zs_ref):
    xf = x_ref[...]
    ms = jnp.mean(xf * xf, axis=-1, keepdims=True)
    h = (xf * lax.rsqrt(ms + RMS_EPS) * g_ref[...]).astype(BF16)
    h_ref[...] = h
    row = lax.broadcasted_iota(jnp.int32, wa_ref.shape, 0)
    w_small = jnp.where(row < SMALL_NGATE, wa_ref[...],
                        jnp.where(row < SMALL_NGATE + 3 * NSA_HEADS, wb_ref[...], 0.0))
    zs_ref[...] = _dot_nt(h, w_small.astype(BF16))


def _norm_small(x2, g, w_t, tm=512):
    m = x2.shape[0]
    return pl.pallas_call(
        _norm_kernel,
        grid=(m // tm,),
        in_specs=[
            pl.BlockSpec((tm, D_MODEL), lambda i: (i, 0)),
            pl.BlockSpec((1, D_MODEL), lambda i: (0, 0)),
            pl.BlockSpec((LANES, D_MODEL), lambda i: (COL_GLR // LANES, 0)),
            pl.BlockSpec((LANES, D_MODEL), lambda i: (COL_NGATE // LANES, 0)),
        ],
        out_specs=[pl.BlockSpec((tm, D_MODEL), lambda i: (i, 0)),
                   pl.BlockSpec((tm, LANES), lambda i: (i, 0))],
        out_shape=[jax.ShapeDtypeStruct((m, D_MODEL), BF16),
                   jax.ShapeDtypeStruct((m, LANES), F32)],
        compiler_params=_params(("parallel",)),
        name="norm_small",
    )(x2, g, w_t, w_t)


IN_TN = 1024
IN_SHIFT_B = GLA_GATE_RANK
IN_SHIFT_C = GLA_GATE_RANK + 3 * NSA_HEADS
IN_BLOCKS_A = Z_NQ // IN_TN
IN_BLOCKS_AB = Z_MG // IN_TN
IN_ROWS = 48
assert (IN_TN - IN_SHIFT_B) % IN_ROWS == 0 and (IN_TN - IN_SHIFT_C) % IN_ROWS == 0


def _in_proj_kernel(h_ref, wm_ref, we_ref, z_ref, w_scr):
    j = pl.program_id(0)

    def build(shift):
        if shift == 0:
            w_scr[...] = wm_ref[...].astype(BF16)
            return

        def body(r, carry):
            dst = pl.ds(pl.multiple_of(r * IN_ROWS, 16), IN_ROWS)
            src = pl.ds(pl.multiple_of(r * IN_ROWS + shift, 8), IN_ROWS)
            w_scr[dst, :] = wm_ref[src, :].astype(BF16)
            return carry
        lax.fori_loop(0, (IN_TN - shift) // IN_ROWS, body, 0)
        w_scr[IN_TN - shift:, :] = we_ref[:shift, :].astype(BF16)

    first = pl.program_id(1) == 0

    @pl.when(first & (j < IN_BLOCKS_A))
    def _():
        build(0)

    @pl.when(first & (j >= IN_BLOCKS_A) & (j < IN_BLOCKS_AB))
    def _():
        build(IN_SHIFT_B)

    @pl.when(first & (j >= IN_BLOCKS_AB))
    def _():
        build(IN_SHIFT_C)

    z_ref[...] = _dot_nt(h_ref[...], w_scr[...]).astype(z_ref.dtype)


def _in_proj(h, w_t, tm=1024):
    m = h.shape[0]
    tn = IN_TN
    return pl.pallas_call(
        _in_proj_kernel,
        grid=(Z_SMALL // tn, m // tm),
        in_specs=[
            pl.BlockSpec((tm, D_MODEL), lambda j, i: (i, 0)),
            pl.BlockSpec((tn, D_MODEL), lambda j, i: (j, 0)),
            pl.BlockSpec((LANES, D_MODEL), lambda j, i: ((j + 1) * (tn // LANES), 0)),
        ],
        out_specs=pl.BlockSpec((tm, tn), lambda j, i: (i, j)),
        out_shape=jax.ShapeDtypeStruct((m, Z_SMALL), BF16),
        scratch_shapes=[pltpu.VMEM((tn, D_MODEL), BF16)],
        compiler_params=_params(("parallel", "arbitrary")),
        name="in_proj",
    )(h, w_t, w_t)


GLA_HPS = 4


def _gla_kernel(q_ref, k_ref, v_ref, r_ref, sm_ref, wg_ref, bg_ref, gn_ref, o_ref,
                state_ref, *, n_chunks):
    @pl.when(pl.program_id(2) == 0)
    def _():
        state_ref[...] = jnp.zeros_like(state_ref)

    C = GLA_CHUNK
    HK, HV = GLA_HK, GLA_HV
    ts = n_chunks * C
    row = lax.broadcasted_iota(jnp.int32, (ts, ts), 0)
    col = lax.broadcasted_iota(jnp.int32, (ts, ts), 1)
    tri = jnp.where((row >= col) & ((row // C) == (col // C)), 1.0, 0.0).astype(BF16)
    crow = lax.broadcasted_iota(jnp.int32, (C, C), 0)
    ccol = lax.broadcasted_iota(jnp.int32, (C, C), 1)
    causal = crow >= ccol

    u = _dot(sm_ref[...].astype(BF16), wg_ref[...]) + bg_ref[...]
    log_a = (jnp.minimum(u, 0.0) - jnp.log(1.0 + jnp.exp(-jnp.abs(u)))) / GLA_TAU
    hi, lo = _split2(log_a)
    b_all = _dot(tri, hi) + _dot(tri, lo)
    gn = gn_ref[...]

    heads = range(GLA_HPS)
    kcol = [slice(j * HK, (j + 1) * HK) for j in heads]
    vcol = [slice(j * HV, (j + 1) * HV) for j in heads]
    b = [b_all[:, kcol[j]] for j in heads]
    k = [k_ref[:, kcol[j]].astype(F32) for j in heads]
    q_dec = [(q_ref[:, kcol[j]].astype(F32) * (HK ** -0.5) * jnp.exp(b[j])).astype(BF16) for j in heads]
    k_dec = [(k[j] * jnp.exp(-b[j])).astype(BF16) for j in heads]
    state = [state_ref[j] for j in heads]
    for c in range(n_chunks):
        rows = slice(c * C, (c + 1) * C)
        for j in heads:
            b_c = b[j][rows]
            b_last = b_c[C - 1:C, :]
            k_end = (k[j][rows] * jnp.exp(b_last - b_c)).astype(BF16)
            v_c = v_ref[rows, vcol[j]]
            attn = jnp.where(causal, _dot_nt(q_dec[j][rows], k_dec[j][rows]), 0.0).astype(BF16)
            o = _dot(attn, v_c) + _dot_nt(q_dec[j][rows], state[j].astype(BF16))
            state[j] = state[j] * jnp.exp(b_last) + _dot_tn(v_c, k_end)
            ms = jnp.mean(o * o, axis=-1, keepdims=True)
            y = o * lax.rsqrt(ms + RMS_EPS) * gn
            r = r_ref[rows, vcol[j]].astype(F32)
            o_ref[rows, vcol[j]] = (y * (r * _sigmoid(r))).astype(o_ref.dtype)
    for j in heads:
        state_ref[j] = state[j]


def _gla(z3, zs3, w_gate_pad, b_gate, g_out, ts=512):
    bsz, s, _ = z3.shape
    qb, vb = GLA_HPS * GLA_HK, GLA_HPS * GLA_HV
    return pl.pallas_call(
        functools.partial(_gla_kernel, n_chunks=ts // GLA_CHUNK),
        grid=(bsz, GLA_HEADS // GLA_HPS, s // ts),
        in_specs=[
            pl.BlockSpec((None, ts, qb), lambda b, h, i: (b, i, Z_GQ // qb + h)),
            pl.BlockSpec((None, ts, qb), lambda b, h, i: (b, i, Z_GK // qb + h)),
            pl.BlockSpec((None, ts, vb), lambda b, h, i: (b, i, Z_GV // vb + h)),
            pl.BlockSpec((None, ts, vb), lambda b, h, i: (b, i, Z_GR // vb + h)),
            pl.BlockSpec((None, ts, LANES), lambda b, h, i: (b, i, 0)),
            pl.BlockSpec((LANES, qb), lambda b, h, i: (0, h)),
            pl.BlockSpec((1, qb), lambda b, h, i: (0, h)),
            pl.BlockSpec((1, GLA_HV), lambda b, h, i: (0, 0)),
        ],
        out_specs=pl.BlockSpec((None, ts, vb), lambda b, h, i: (b, i, h)),
        out_shape=jax.ShapeDtypeStruct((bsz, s, GLA_DV), BF16),
        scratch_shapes=[pltpu.VMEM((GLA_HPS, GLA_HV, GLA_HK), F32)],
        compiler_params=_params(("parallel", "parallel", "arbitrary")),
        name="gla",
    )(z3, z3, z3, z3, zs3, w_gate_pad, b_gate, g_out)


def _cmp_kernel(t_ref, pek_ref, pev_ref, w1k_ref, w1v_ref, w2k_ref, w2v_ref, o_ref,
                tf_scr, lo_scr, hi_scr, pe_scr, w1_scr, w2_scr):
    hd = NSA_HD
    half = CMP_STRIDE * hd
    nchunk = lo_scr.shape[0] // NSA_KV_HEADS

    @pl.when(pl.program_id(0) == 0)
    def _():
        pe_scr[...] = pek_ref[...]
        w1_scr[...] = w1k_ref[...].astype(BF16)
        w2_scr[...] = w2k_ref[...].astype(BF16)

    @pl.when(pl.program_id(0) == 1)
    def _():
        pe_scr[...] = pev_ref[...]
        w1_scr[...] = w1v_ref[...].astype(BF16)
        w2_scr[...] = w2v_ref[...].astype(BF16)

    for h in range(NSA_KV_HEADS):
        tf_scr[h] = t_ref[:, h * hd:(h + 1) * hd].astype(F32)
        rows = slice(h * nchunk, (h + 1) * nchunk)
        for r in range(CMP_STRIDE):
            t_r = tf_scr.at[h][pl.ds(r, nchunk, stride=CMP_STRIDE), :]
            lo_scr[rows, r * hd:(r + 1) * hd] = (t_r + pe_scr[r:r + 1, :]).astype(BF16)
            hi_scr[rows, r * hd:(r + 1) * hd] = (t_r + pe_scr[CMP_STRIDE + r:CMP_STRIDE + r + 1, :]).astype(BF16)
    a = _dot(lo_scr[...], w1_scr[:half, :])
    bm = _dot(hi_scr[...], w1_scr[half:, :])
    pre = a + pltpu.roll(bm, bm.shape[0] - 1, 0)
    hid = (pre * _sigmoid(pre)).astype(BF16)
    out = _dot(hid, w2_scr[...]).astype(o_ref.dtype)
    for h in range(NSA_KV_HEADS):
        o_ref[h] = out[h * nchunk:(h + 1) * nchunk]


def _compress(z3, pe_k, pe_v, w1_k, w1_v, w2_k, w2_v):
    bsz, s, _ = z3.shape
    hkv, hd = NSA_KV_HEADS, NSA_HD
    nchunk = s // CMP_STRIDE
    width = CMP_STRIDE * hd
    return pl.pallas_call(
        _cmp_kernel,
        grid=(2, bsz),
        in_specs=[
            pl.BlockSpec((None, s, hkv * hd), lambda kv, b: (b, 0, Z_NKV // (hkv * hd) + kv)),
            pl.BlockSpec((CMP_LEN, hd), lambda kv, b: (0, 0)),
            pl.BlockSpec((CMP_LEN, hd), lambda kv, b: (0, 0)),
            pl.BlockSpec((2 * width, CMP_HIDDEN), lambda kv, b: (0, 0)),
            pl.BlockSpec((2 * width, CMP_HIDDEN), lambda kv, b: (0, 0)),
            pl.BlockSpec((CMP_HIDDEN, hd), lambda kv, b: (0, 0)),
            pl.BlockSpec((CMP_HIDDEN, hd), lambda kv, b: (0, 0)),
        ],
        out_specs=pl.BlockSpec((None, None, hkv, nchunk, hd), lambda kv, b: (kv, b, 0, 0, 0)),
        out_shape=jax.ShapeDtypeStruct((2, bsz, hkv, nchunk, hd), BF16),
        scratch_shapes=[pltpu.VMEM((hkv, s, hd), F32),
                        pltpu.VMEM((hkv * nchunk, width), BF16),
                        pltpu.VMEM((hkv * nchunk, width), BF16),
                        pltpu.VMEM((CMP_LEN, hd), F32),
                        pltpu.VMEM((2 * width, CMP_HIDDEN), BF16),
                        pltpu.VMEM((CMP_HIDDEN, hd), BF16)],
        compiler_params=_params(("parallel", "parallel")),
        name="nsa_compress",
    )(z3, pe_k, pe_v, w1_k, w1_v, w2_k, w2_v)


NSA_TQ = 256
NSA_TK = NSA_TQ
LOG2E = 1.4426950408889634


NSA_HPS = 4


def _nsa_kernel(q_ref, sm_ref, kc_ref, vc_ref, ks_ref, vs_ref, kw_ref, vw_ref, mcs_ref,
                o_ref, sel_scr, gate_scr, p_scr, acc_scr):
    G, TQ, TK, HD = NSA_GROUP, NSA_TQ, NSA_TK, NSA_HD
    R = G * TQ
    heads = range(NSA_HPS)
    hk0 = pl.program_id(1) * NSA_HPS
    qi = pl.program_id(2)
    qs = qi * TQ

    def lanes4(x):
        return jnp.concatenate([x] * G, axis=1)

    def hcol(j):
        return slice(j * HD, (j + 1) * HD)

    q4 = []
    for j in heads:
        qj = jnp.concatenate([q_ref[:, (j * G + g) * HD:(j * G + g + 1) * HD] for g in range(G)], axis=0)
        q4.append((qj.astype(F32) * (HD ** -0.5 * LOG2E)).astype(BF16))

    nc = kc_ref.shape[1]
    tq_c = qs + lax.broadcasted_iota(jnp.int32, (nc, TQ), 1)
    cmp_end = lax.broadcasted_iota(jnp.int32, (nc, TQ), 0) * CMP_STRIDE + (CMP_LEN - 1)
    valid_c = lanes4(jnp.where(cmp_end <= tq_c, 1.0, 0.0)) > 0.5
    mcs = mcs_ref[...]
    nb = mcs.shape[0]
    blk = lax.broadcasted_iota(jnp.int32, (nb, TQ), 0)
    tq_b = qs + lax.broadcasted_iota(jnp.int32, (nb, TQ), 1)
    cur = tq_b // SLC_BLOCK
    forced = (blk == 0) | (blk == cur) | (blk == cur - 1)
    visible = blk * SLC_BLOCK <= tq_b
    o_c, imps = [], []
    for j in heads:
        s_c = jnp.where(valid_c, _dot_nt(kc_ref[j], q4[j]), NEG_INF)
        e_c = jnp.where(valid_c, jnp.exp2(s_c - jnp.max(s_c, axis=0, keepdims=True)), 0.0)
        l_c = jnp.sum(e_c, axis=0, keepdims=True)
        p_c = e_c / jnp.where(l_c > 0.0, l_c, 1.0)
        o_c.append(_dot_tn(vc_ref[j], p_c.astype(BF16)))

        p_sum = p_c[:, 0:TQ]
        for g in range(1, G):
            p_sum = p_sum + p_c[:, g * TQ:(g + 1) * TQ]
        imp = sum(_dot(mcs, part) for part in _split3(p_sum))
        imp = jnp.where(forced, jnp.inf, imp)
        imps.append(jnp.where(visible, imp, -jnp.inf))

    top_n = min(SLC_TOPN, nb)
    few_blocks = qs + TQ <= top_n * SLC_BLOCK

    @pl.when(few_blocks)
    def _():
        sel_scr[...] = jnp.zeros_like(sel_scr)

    @pl.when(jnp.logical_not(few_blocks))
    def _():
        for j in heads:
            imp = imps[j]
            rank = jnp.zeros((nb, TQ), F32)
            for kb in range(nb):
                rk = imp[kb:kb + 1, :]
                ahead = jnp.where(rk > imp, 1.0, jnp.where(rk == imp, jnp.where(blk > kb, 1.0, 0.0), 0.0))
                rank = rank + ahead
            sel_scr[j] = jnp.where(rank < float(top_n), 0.0, NEG_INF)

    blocks_per_tile = TK // SLC_BLOCK

    def sel_bias(j, tile):
        return jnp.concatenate(
            [jnp.broadcast_to(sel_scr[j, pl.ds(tile * blocks_per_tile + i, 1), :], (SLC_BLOCK, TQ))
             for i in range(blocks_per_tile)], axis=0)

    krow = lax.broadcasted_iota(jnp.int32, (TK, TQ), 0)
    qcol = lax.broadcasted_iota(jnp.int32, (TK, TQ), 1)
    diag = pl.ds(pl.multiple_of(qs, TK), TK)
    m_d, l_d = [], []
    for j in heads:
        bias_d = jnp.where(krow <= qcol, sel_bias(j, qi), NEG_INF)
        s_d = _dot_nt(ks_ref[diag, hcol(j)], q4[j]) + lanes4(bias_d)
        m = jnp.max(s_d, axis=0, keepdims=True)
        p_d = jnp.exp2(s_d - m)
        m_d.append(m)
        l_d.append(jnp.sum(p_d, axis=0, keepdims=True))
        p_scr[j] = p_d.astype(BF16)

    def kv_step(t, carry):
        ms, ls, voff = carry
        prev = pl.ds(pl.multiple_of(voff, TK), TK)
        koff = pl.multiple_of(t * TK, TK)
        ms_n, ls_n = [], []
        for j in heads:
            pv = _dot_tn(vs_ref[prev, hcol(j)], p_scr[j])
            s = _dot_nt(ks_ref[pl.ds(koff, TK), hcol(j)], q4[j]) + lanes4(sel_bias(j, t))
            m_new = jnp.maximum(ms[j], jnp.max(s, axis=0, keepdims=True))
            alpha = jnp.exp2(ms[j] - m_new)
            p = jnp.exp2(s - m_new)
            ms_n.append(m_new)
            ls_n.append(alpha * ls[j] + jnp.sum(p, axis=0, keepdims=True))
            acc_scr[j] = alpha * (acc_scr[j] + pv)
            p_scr[j] = p.astype(BF16)
        return tuple(ms_n), tuple(ls_n), koff

    acc_scr[...] = jnp.zeros_like(acc_scr)
    _, l_s, voff = lax.fori_loop(0, qi, kv_step, (tuple(m_d), tuple(l_d), qs))
    last = pl.ds(pl.multiple_of(voff, TK), TK)
    o_s = [(acc_scr[j] + _dot_tn(vs_ref[last, hcol(j)], p_scr[j])) / l_s[j] for j in heads]

    span = WINDOW + TQ
    w0 = pl.multiple_of(jnp.maximum(qs - WINDOW, 0), TQ)
    kpos = w0 + lax.broadcasted_iota(jnp.int32, (span, TQ), 0)
    tq_w = qs + lax.broadcasted_iota(jnp.int32, (span, TQ), 1)
    bias_w = lanes4(jnp.where(kpos <= tq_w, jnp.where(kpos > tq_w - WINDOW, 0.0, NEG_INF), NEG_INF))
    o_w = []
    for j in heads:
        s_w = _dot_nt(kw_ref[pl.ds(w0, span), hcol(j)], q4[j]) + bias_w
        e_w = jnp.exp2(s_w - jnp.max(s_w, axis=0, keepdims=True))
        o_w.append(_dot_tn(vw_ref[pl.ds(w0, span), hcol(j)], e_w.astype(BF16))
                   / jnp.sum(e_w, axis=0, keepdims=True))

    gate_scr[...] = _sigmoid(sm_ref[...]).T
    for j in heads:
        def gate_row(branch):
            base = SMALL_NGATE + branch * NSA_HEADS + (hk0 + j) * G
            return jnp.concatenate([gate_scr[pl.ds(base + g, 1), :] for g in range(G)], axis=1)

        o = gate_row(0) * o_c[j] + gate_row(1) * o_s[j] + gate_row(2) * o_w[j]
        for g in range(G):
            o_ref[:, (j * G + g) * HD:(j * G + g + 1) * HD] = o[:, g * TQ:(g + 1) * TQ].T.astype(o_ref.dtype)


def _nsa(z3, zs3, cmp, mcs_t):
    bsz, s, _ = z3.shape
    hkv, hd, tq, hps = NSA_KV_HEADS, NSA_HD, NSA_TQ, NSA_HPS
    qw = hps * NSA_GROUP * hd
    kw = hps * hd
    nc = cmp.shape[3]
    nb = mcs_t.shape[0]

    def kv_spec(i):
        return pl.BlockSpec((None, s, kw), lambda b, h, q: (b, 0, (Z_NKV + i * hkv * hd) // kw + h))

    return pl.pallas_call(
        _nsa_kernel,
        grid=(bsz, hkv // hps, s // tq),
        in_specs=[
            pl.BlockSpec((None, tq, qw), lambda b, h, q: (b, q, Z_NQ // qw + h)),
            pl.BlockSpec((None, tq, LANES), lambda b, h, q: (b, q, 0)),
            pl.BlockSpec((None, None, hps, nc, hd), lambda b, h, q: (0, b, h, 0, 0)),
            pl.BlockSpec((None, None, hps, nc, hd), lambda b, h, q: (1, b, h, 0, 0)),
            kv_spec(2), kv_spec(3), kv_spec(4), kv_spec(5),
            pl.BlockSpec(mcs_t.shape, lambda b, h, q: (0, 0)),
        ],
        out_specs=pl.BlockSpec((None, tq, qw), lambda b, h, q: (b, q, h)),
        out_shape=jax.ShapeDtypeStruct((bsz, s, NSA_HEADS * hd), BF16),
        scratch_shapes=[pltpu.VMEM((hps, nb, tq), F32), pltpu.VMEM((LANES, tq), F32),
                        pltpu.VMEM((hps, NSA_TK, NSA_GROUP * tq), BF16),
                        pltpu.VMEM((hps, hd, NSA_GROUP * tq), F32)],
        compiler_params=_params(("parallel", "parallel", "arbitrary")),
        name="nsa_attn",
    )(z3, zs3, cmp, cmp, z3, z3, z3, z3, mcs_t)


def _merge_kernel(og_ref, on_ref, wg_ref, wn_ref, ga_ref, gb_ref, wo_ref, o_ref, wo_bf_ref, wg_scr, wn_scr):
    wo_bf_ref[...] = wo_ref[...].astype(BF16)

    @pl.when(pl.program_id(1) == 0)
    def _():
        wg_scr[...] = wg_ref[...].astype(BF16)
        wn_scr[...] = wn_ref[...].astype(BF16)

    a = _dot(og_ref[...], wg_scr[...])
    b = _dot(on_ref[...], wn_scr[...])
    ga = _sigmoid(ga_ref[...].astype(F32))
    gb = _sigmoid(gb_ref[...].astype(F32))
    o_ref[...] = (ga * a + gb * b).astype(o_ref.dtype)


def _merge(o_gla, o_nsa, w_pg, w_pn, z2, w_out, tm=512, tn=1024):
    m = o_gla.shape[0]
    n_i = m // tm
    slab = D_MODEL // ((D_MODEL // tn) * n_i)
    assert slab % 16 == 0 and slab * (D_MODEL // tn) * n_i == D_MODEL
    once = pl.Buffered(1)
    return pl.pallas_call(
        _merge_kernel,
        grid=(D_MODEL // tn, m // tm),
        in_specs=[
            pl.BlockSpec((tm, GLA_DV), lambda j, i: (i, 0)),
            pl.BlockSpec((tm, NSA_HEADS * NSA_HD), lambda j, i: (i, 0)),
            pl.BlockSpec((GLA_DV, tn), lambda j, i: (0, j), pipeline_mode=once),
            pl.BlockSpec((NSA_HEADS * NSA_HD, tn), lambda j, i: (0, j), pipeline_mode=once),
            pl.BlockSpec((tm, tn), lambda j, i: (i, Z_MG // tn + j)),
            pl.BlockSpec((tm, tn), lambda j, i: (i, (Z_MG + D_MODEL) // tn + j)),
            pl.BlockSpec((slab, D_MODEL), lambda j, i: (j * n_i + i, 0)),
        ],
        out_specs=[pl.BlockSpec((tm, tn), lambda j, i: (i, j)),
                   pl.BlockSpec((slab, D_MODEL), lambda j, i: (j * n_i + i, 0))],
        out_shape=[jax.ShapeDtypeStruct((m, D_MODEL), BF16),
                   jax.ShapeDtypeStruct((D_MODEL, D_MODEL), BF16)],
        scratch_shapes=[pltpu.VMEM((GLA_DV, tn), BF16), pltpu.VMEM((NSA_HEADS * NSA_HD, tn), BF16)],
        compiler_params=_params(("parallel", "arbitrary")),
        name="merge_proj",
    )(o_gla, o_nsa, w_pg, w_pn, z2, z2, w_out)


def _out_proj_kernel(mg_ref, w_ref, x_ref, g_ref, x1_ref, h_ref):
    x1 = x_ref[...] + _dot(mg_ref[...], w_ref[...])
    x1_ref[...] = x1
    ms = jnp.mean(x1 * x1, axis=-1, keepdims=True)
    h_ref[...] = (x1 * lax.rsqrt(ms + RMS_EPS) * g_ref[...]).astype(h_ref.dtype)


def _out_proj(merged, w_out, x2, g_ffn, tm=512):
    m = merged.shape[0]
    return pl.pallas_call(
        _out_proj_kernel,
        grid=(m // tm,),
        in_specs=[
            pl.BlockSpec((tm, D_MODEL), lambda i: (i, 0)),
            pl.BlockSpec((D_MODEL, D_MODEL), lambda i: (0, 0)),
            pl.BlockSpec((tm, D_MODEL), lambda i: (i, 0)),
            pl.BlockSpec((1, D_MODEL), lambda i: (0, 0)),
        ],
        out_specs=[
            pl.BlockSpec((tm, D_MODEL), lambda i: (i, 0)),
            pl.BlockSpec((tm, D_MODEL), lambda i: (i, 0)),
        ],
        out_shape=[jax.ShapeDtypeStruct((m, D_MODEL), F32),
                   jax.ShapeDtypeStruct((m, D_MODEL), BF16)],
        compiler_params=_params(("parallel",)),
        name="out_proj",
    )(merged, w_out, x2, g_ffn)


def _ffn_up_kernel(h_ref, wg_ref, wu_ref, wd_ref, o_ref, wd_bf_ref, wg_scr, wu_scr):
    wd_bf_ref[...] = wd_ref[...].astype(BF16)

    @pl.when(pl.program_id(1) == 0)
    def _():
        wg_scr[...] = wg_ref[...].astype(BF16)
        wu_scr[...] = wu_ref[...].astype(BF16)

    h = h_ref[...]
    g = _dot(h, wg_scr[...])
    u = _dot(h, wu_scr[...])
    o_ref[...] = (g * _sigmoid(g) * u).astype(o_ref.dtype)


def _ffn_up(h2, w_gate, w_up, w_down, tm=1024, tf=512):
    m = h2.shape[0]
    n_i = m // tm
    slab = D_FF // ((D_FF // tf) * n_i)
    assert slab % 16 == 0 and slab * (D_FF // tf) * n_i == D_FF
    return pl.pallas_call(
        _ffn_up_kernel,
        grid=(D_FF // tf, m // tm),
        in_specs=[
            pl.BlockSpec((tm, D_MODEL), lambda j, i: (i, 0)),
            pl.BlockSpec((D_MODEL, tf), lambda j, i: (0, j)),
            pl.BlockSpec((D_MODEL, tf), lambda j, i: (0, j)),
            pl.BlockSpec((slab, D_MODEL), lambda j, i: (j * n_i + i, 0)),
        ],
        out_specs=[pl.BlockSpec((tm, tf), lambda j, i: (i, j)),
                   pl.BlockSpec((slab, D_MODEL), lambda j, i: (j * n_i + i, 0))],
        out_shape=[jax.ShapeDtypeStruct((m, D_FF), BF16),
                   jax.ShapeDtypeStruct((D_FF, D_MODEL), BF16)],
        scratch_shapes=[pltpu.VMEM((D_MODEL, tf), BF16), pltpu.VMEM((D_MODEL, tf), BF16)],
        compiler_params=_params(("parallel", "arbitrary")),
        name="ffn_up",
    )(h2, w_gate, w_up, w_down)


def _ffn_down_kernel(a_ref, w_ref, x1_ref, g_ref, o_ref, *, final_norm):
    k = pl.program_id(1)

    @pl.when(k == 0)
    def _():
        o_ref[...] = x1_ref[...]

    o_ref[...] += _dot(a_ref[...], w_ref[...])

    if final_norm:
        @pl.when(k == pl.num_programs(1) - 1)
        def _():
            y = o_ref[...]
            ms = jnp.mean(y * y, axis=-1, keepdims=True)
            o_ref[...] = y * lax.rsqrt(ms + RMS_EPS) * g_ref[...]


def _ffn_down(act, w_down, x1, g_final, final_norm, tm=1024, tk=1408):
    m = act.shape[0]
    return pl.pallas_call(
        functools.partial(_ffn_down_kernel, final_norm=final_norm),
        grid=(m // tm, D_FF // tk),
        in_specs=[
            pl.BlockSpec((tm, tk), lambda i, k: (i, k)),
            pl.BlockSpec((tk, D_MODEL), lambda i, k: (k, 0)),
            pl.BlockSpec((tm, D_MODEL), lambda i, k: (i, 0)),
            pl.BlockSpec((1, D_MODEL), lambda i, k: (0, 0)),
        ],
        out_specs=pl.BlockSpec((tm, D_MODEL), lambda i, k: (i, 0)),
        out_shape=jax.ShapeDtypeStruct((m, D_MODEL), F32),
        compiler_params=_params(("parallel", "arbitrary")),
        name="ffn_down",
    )(act, w_down, x1, g_final)


def _cmp_to_slc_weights_t(seq):
    n_cmp = seq // CMP_STRIDE - 1
    n_slc = seq // SLC_BLOCK
    c0 = np.arange(n_cmp)[:, None] * CMP_STRIDE
    s0 = np.arange(n_slc)[None, :] * SLC_BLOCK
    ov = np.clip(np.minimum(c0 + CMP_LEN, s0 + SLC_BLOCK) - np.maximum(c0, s0), 0, None)
    m_cs = np.zeros((n_cmp + 1, n_slc), np.float32)
    m_cs[:n_cmp] = ov / CMP_LEN
    return jnp.asarray(m_cs.T, BF16)


def kernel(x, g_mix, w_in, w_gla_gate, b_gla_gate, g_gla_out, pe_cmp_k, w_cmp_k1, w_cmp_k2,
           pe_cmp_v, w_cmp_v1, w_cmp_v2, w_proj_gla, w_proj_nsa, w_out, g_ffn,
           w_ffn_gate, w_ffn_up, w_ffn_down, g_final):
    bsz, seq, d = x.shape
    m = bsz * seq
    depth = w_in.shape[0]
    mcs_t = _cmp_to_slc_weights_t(seq)
    xc = x.reshape(m, d)
    for l in range(depth):
        w_t = jnp.swapaxes(w_in[l], 0, 1)
        h, zs = _norm_small(xc, g_mix[l][None], w_t)
        z2 = _in_proj(h, w_t)
        z3 = z2.reshape(bsz, seq, Z_SMALL)
        zs3 = zs.reshape(bsz, seq, LANES)

        w_gate_pad = jnp.zeros((LANES, GLA_DK), F32).at[SMALL_GLR:SMALL_GLR + GLA_GATE_RANK].set(w_gla_gate[l])
        o_gla = _gla(z3, zs3, w_gate_pad.astype(BF16), b_gla_gate[l][None], g_gla_out[l][None])

        cmp = _compress(z3, pe_cmp_k[l], pe_cmp_v[l], w_cmp_k1[l], w_cmp_v1[l], w_cmp_k2[l], w_cmp_v2[l])
        o_nsa = _nsa(z3, zs3, cmp, mcs_t)

        merged, w_out_bf = _merge(o_gla.reshape(m, GLA_DV), o_nsa.reshape(m, NSA_HEADS * NSA_HD),
                                  w_proj_gla[l], w_proj_nsa[l], z2, w_out[l])
        x1, h2 = _out_proj(merged, w_out_bf, xc, g_ffn[l][None])
        act, w_down_bf = _ffn_up(h2, w_ffn_gate[l], w_ffn_up[l], w_ffn_down[l])
        xc = _ffn_down(act, w_down_bf, x1, g_final[None],
                       final_norm=(l == depth - 1))
    return xc.reshape(bsz, seq, d)
```

```python
import functools

import numpy as np
import jax
import jax.numpy as jnp
from jax import lax
from jax.experimental import pallas as pl
from jax.experimental.pallas import tpu as pltpu

F32 = jnp.float32
BF16 = jnp.bfloat16

D_MODEL = 2048
GLA_HEADS = 4
GLA_DK = D_MODEL // 2
GLA_DV = D_MODEL
GLA_HK = GLA_DK // GLA_HEADS
GLA_HV = GLA_DV // GLA_HEADS
GLA_GATE_RANK = 16
GLA_TAU = 16.0
GLA_CHUNK = 64

NSA_HEADS = 16
NSA_KV_HEADS = 4
NSA_GROUP = NSA_HEADS // NSA_KV_HEADS
NSA_HD = D_MODEL // NSA_HEADS
CMP_STRIDE = 16
CMP_LEN = 2 * CMP_STRIDE
CMP_HIDDEN = 256
SLC_BLOCK = 64
SLC_TOPN = 16
WINDOW = 512
D_FF = ((8 * D_MODEL // 3 + 255) // 256) * 256

IN_SPLIT_SIZES = (GLA_DK, GLA_DK, GLA_DV, GLA_DV, GLA_GATE_RANK,
                  NSA_HEADS * NSA_HD, 6 * NSA_KV_HEADS * NSA_HD, 3 * NSA_HEADS,
                  2 * D_MODEL)

NEG_INF = -1e30
RMS_EPS = 1e-6

Z_GQ = 0
Z_GK = Z_GQ + GLA_DK
Z_GV = Z_GK + GLA_DK
Z_GR = Z_GV + GLA_DV
Z_NQ = Z_GR + GLA_DV
Z_NKV = Z_NQ + NSA_HEADS * NSA_HD
Z_MG = Z_NKV + 6 * NSA_KV_HEADS * NSA_HD
Z_SMALL = Z_MG + 2 * D_MODEL
LANES = 128
SMALL_GLR = 0
SMALL_NGATE = GLA_GATE_RANK

VMEM_LIMIT = 56 * 1024 * 1024


def _dot(a, b):
    return jnp.dot(a, b, preferred_element_type=F32)


def _dot_nt(a, b):
    return lax.dot_general(a, b, (((1,), (1,)), ((), ())), preferred_element_type=F32)


def _dot_tn(a, b):
    return lax.dot_general(a, b, (((0,), (0,)), ((), ())), preferred_element_type=F32)


def _sigmoid(x):
    return 1.0 / (1.0 + jnp.exp(-x))


def _split2(x):
    hi = x.astype(BF16)
    lo = (x - hi.astype(F32)).astype(BF16)
    return hi, lo


def _split3(x):
    hi = x.astype(BF16)
    r1 = x - hi.astype(F32)
    mid = r1.astype(BF16)
    lo = (r1 - mid.astype(F32)).astype(BF16)
    return hi, mid, lo


def _params(sem, **kw):
    return pltpu.CompilerParams(dimension_semantics=sem, vmem_limit_bytes=VMEM_LIMIT, **kw)


_IN_OFFS = np.cumsum((0,) + IN_SPLIT_SIZES)
COL_GLR = int(_IN_OFFS[4])
COL_NGATE = int(_IN_OFFS[7])
assert COL_GLR % LANES == SMALL_GLR and COL_NGATE % LANES == SMALL_NGATE
assert SMALL_GLR + GLA_GATE_RANK == SMALL_NGATE and SMALL_NGATE + 3 * NSA_HEADS <= LANES


def _norm_kernel(x_ref, g_ref, wa_ref, wb_ref, h_ref, zs_ref):
    xf = x_ref[...]
    ms = jnp.mean(xf * xf, axis=-1, keepdims=True)
    h = (xf * lax.rsqrt(ms + RMS_EPS) * g_ref[...]).astype(BF16)
    h_ref[...] = h
    row = lax.broadcasted_iota(jnp.int32, wa_ref.shape, 0)
    w_small = jnp.where(row < SMALL_NGATE, wa_ref[...],
                        jnp.where(row < SMALL_NGATE + 3 * NSA_HEADS, wb_ref[...], 0.0))
    zs_ref[...] = _dot_nt(h, w_small.astype(BF16))


def _norm_small(x2, g, w_t, tm=512):
    m = x2.shape[0]
    return pl.pallas_call(
        _norm_kernel,
        grid=(m // tm,),
        in_specs=[
            pl.BlockSpec((tm, D_MODEL), lambda i: (i, 0)),
            pl.BlockSpec((1, D_MODEL), lambda i: (0, 0)),
            pl.BlockSpec((LANES, D_MODEL), lambda i: (COL_GLR // LANES, 0)),
            pl.BlockSpec((LANES, D_MODEL), lambda i: (COL_NGATE // LANES, 0)),
        ],
        out_specs=[pl.BlockSpec((tm, D_MODEL), lambda i: (i, 0)),
                   pl.BlockSpec((tm, LANES), lambda i: (i, 0))],
        out_shape=[jax.ShapeDtypeStruct((m, D_MODEL), BF16),
                   jax.ShapeDtypeStruct((m, LANES), F32)],
        compiler_params=_params(("parallel",)),
        name="norm_small",
    )(x2, g, w_t, w_t)


IN_TN = 1024
IN_SHIFT_B = GLA_GATE_RANK
IN_SHIFT_C = GLA_GATE_RANK + 3 * NSA_HEADS
IN_BLOCKS_A = Z_NQ // IN_TN
IN_BLOCKS_AB = Z_MG // IN_TN
IN_ROWS = 48
assert (IN_TN - IN_SHIFT_B) % IN_ROWS == 0 and (IN_TN - IN_SHIFT_C) % IN_ROWS == 0


def _in_proj_kernel(h_ref, wm_ref, we_ref, z_ref, w_scr):
    j = pl.program_id(0)

    def build(shift):
        if shift == 0:
            w_scr[...] = wm_ref[...].astype(BF16)
            return

        def body(r, carry):
            dst = pl.ds(pl.multiple_of(r * IN_ROWS, 16), IN_ROWS)
            src = pl.ds(pl.multiple_of(r * IN_ROWS + shift, 8), IN_ROWS)
            w_scr[dst, :] = wm_ref[src, :].astype(BF16)
            return carry
        lax.fori_loop(0, (IN_TN - shift) // IN_ROWS, body, 0)
        w_scr[IN_TN - shift:, :] = we_ref[:shift, :].astype(BF16)

    first = pl.program_id(1) == 0

    @pl.when(first & (j < IN_BLOCKS_A))
    def _():
        build(0)

    @pl.when(first & (j >= IN_BLOCKS_A) & (j < IN_BLOCKS_AB))
    def _():
        build(IN_SHIFT_B)

    @pl.when(first & (j >= IN_BLOCKS_AB))
    def _():
        build(IN_SHIFT_C)

    z_ref[...] = _dot_nt(h_ref[...], w_scr[...]).astype(z_ref.dtype)


def _in_proj(h, w_t, tm=1024):
    m = h.shape[0]
    tn = IN_TN
    return pl.pallas_call(
        _in_proj_kernel,
        grid=(Z_SMALL // tn, m // tm),
        in_specs=[
            pl.BlockSpec((tm, D_MODEL), lambda j, i: (i, 0)),
            pl.BlockSpec((tn, D_MODEL), lambda j, i: (j, 0)),
            pl.BlockSpec((LANES, D_MODEL), lambda j, i: ((j + 1) * (tn // LANES), 0)),
        ],
        out_specs=pl.BlockSpec((tm, tn), lambda j, i: (i, j)),
        out_shape=jax.ShapeDtypeStruct((m, Z_SMALL), BF16),
        scratch_shapes=[pltpu.VMEM((tn, D_MODEL), BF16)],
        compiler_params=_params(("parallel", "arbitrary")),
        name="in_proj",
    )(h, w_t, w_t)


GLA_HPS = 4


def _gla_kernel(q_ref, k_ref, v_ref, r_ref, sm_ref, wg_ref, bg_ref, gn_ref, o_ref,
                state_ref, *, n_chunks):
    @pl.when(pl.program_id(2) == 0)
    def _():
        state_ref[...] = jnp.zeros_like(state_ref)

    C = GLA_CHUNK
    HK, HV = GLA_HK, GLA_HV
    ts = n_chunks * C
    row = lax.broadcasted_iota(jnp.int32, (ts, ts), 0)
    col = lax.broadcasted_iota(jnp.int32, (ts, ts), 1)
    tri = jnp.where((row >= col) & ((row // C) == (col // C)), 1.0, 0.0).astype(BF16)
    crow = lax.broadcasted_iota(jnp.int32, (C, C), 0)
    ccol = lax.broadcasted_iota(jnp.int32, (C, C), 1)
    causal = crow >= ccol

    u = _dot(sm_ref[...].astype(BF16), wg_ref[...]) + bg_ref[...]
    log_a = (jnp.minimum(u, 0.0) - jnp.log(1.0 + jnp.exp(-jnp.abs(u)))) / GLA_TAU
    hi, lo = _split2(log_a)
    b_all = _dot(tri, hi) + _dot(tri, lo)
    gn = gn_ref[...]

    heads = range(GLA_HPS)
    kcol = [slice(j * HK, (j + 1) * HK) for j in heads]
    vcol = [slice(j * HV, (j + 1) * HV) for j in heads]
    b = [b_all[:, kcol[j]] for j in heads]
    k = [k_ref[:, kcol[j]].astype(F32) for j in heads]
    q_dec = [(q_ref[:, kcol[j]].astype(F32) * (HK ** -0.5) * jnp.exp(b[j])).astype(BF16) for j in heads]
    k_dec = [(k[j] * jnp.exp(-b[j])).astype(BF16) for j in heads]
    state = [state_ref[j] for j in heads]
    for c in range(n_chunks):
        rows = slice(c * C, (c + 1) * C)
        for j in heads:
            b_c = b[j][rows]
            b_last = b_c[C - 1:C, :]
            k_end = (k[j][rows] * jnp.exp(b_last - b_c)).astype(BF16)
            v_c = v_ref[rows, vcol[j]]
            attn = jnp.where(causal, _dot_nt(q_dec[j][rows], k_dec[j][rows]), 0.0).astype(BF16)
            o = _dot(attn, v_c) + _dot_nt(q_dec[j][rows], state[j].astype(BF16))
            state[j] = state[j] * jnp.exp(b_last) + _dot_tn(v_c, k_end)
            ms = jnp.mean(o * o, axis=-1, keepdims=True)
            y = o * lax.rsqrt(ms + RMS_EPS) * gn
            r = r_ref[rows, vcol[j]].astype(F32)
            o_ref[rows, vcol[j]] = (y * (r * _sigmoid(r))).astype(o_ref.dtype)
    for j in heads:
        state_ref[j] = state[j]


def _gla(z3, zs3, w_gate_pad, b_gate, g_out, ts=512):
    bsz, s, _ = z3.shape
    qb, vb = GLA_HPS * GLA_HK, GLA_HPS * GLA_HV
    return pl.pallas_call(
        functools.partial(_gla_kernel, n_chunks=ts // GLA_CHUNK),
        grid=(bsz, GLA_HEADS // GLA_HPS, s // ts),
        in_specs=[
            pl.BlockSpec((None, ts, qb), lambda b, h, i: (b, i, Z_GQ // qb + h)),
            pl.BlockSpec((None, ts, qb), lambda b, h, i: (b, i, Z_GK // qb + h)),
            pl.BlockSpec((None, ts, vb), lambda b, h, i: (b, i, Z_GV // vb + h)),
            pl.BlockSpec((None, ts, vb), lambda b, h, i: (b, i, Z_GR // vb + h)),
            pl.BlockSpec((None, ts, LANES), lambda b, h, i: (b, i, 0)),
            pl.BlockSpec((LANES, qb), lambda b, h, i: (0, h)),
            pl.BlockSpec((1, qb), lambda b, h, i: (0, h)),
            pl.BlockSpec((1, GLA_HV), lambda b, h, i: (0, 0)),
        ],
        out_specs=pl.BlockSpec((None, ts, vb), lambda b, h, i: (b, i, h)),
        out_shape=jax.ShapeDtypeStruct((bsz, s, GLA_DV), BF16),
        scratch_shapes=[pltpu.VMEM((GLA_HPS, GLA_HV, GLA_HK), F32)],
        compiler_params=_params(("parallel", "parallel", "arbitrary")),
        name="gla",
    )(z3, z3, z3, z3, zs3, w_gate_pad, b_gate, g_out)


def _cmp_kernel(t_ref, pek_ref, pev_ref, w1k_ref, w1v_ref, w2k_ref, w2v_ref, o_ref,
                tf_scr, lo_scr, hi_scr, pe_scr, w1_scr, w2_scr):
    hd = NSA_HD
    half = CMP_STRIDE * hd
    nchunk = lo_scr.shape[0] // NSA_KV_HEADS

    @pl.when(pl.program_id(0) == 0)
    def _():
        pe_scr[...] = pek_ref[...]
        w1_scr[...] = w1k_ref[...].astype(BF16)
        w2_scr[...] = w2k_ref[...].astype(BF16)

    @pl.when(pl.program_id(0) == 1)
    def _():
        pe_scr[...] = pev_ref[...]
        w1_scr[...] = w1v_ref[...].astype(BF16)
        w2_scr[...] = w2v_ref[...].astype(BF16)

    for h in range(NSA_KV_HEADS):
        tf_scr[h] = t_ref[:, h * hd:(h + 1) * hd].astype(F32)
        rows = slice(h * nchunk, (h + 1) * nchunk)
        for r in range(CMP_STRIDE):
            t_r = tf_scr.at[h][pl.ds(r, nchunk, stride=CMP_STRIDE), :]
            lo_scr[rows, r * hd:(r + 1) * hd] = (t_r + pe_scr[r:r + 1, :]).astype(BF16)
            hi_scr[rows, r * hd:(r + 1) * hd] = (t_r + pe_scr[CMP_STRIDE + r:CMP_STRIDE + r + 1, :]).astype(BF16)
    a = _dot(lo_scr[...], w1_scr[:half, :])
    bm = _dot(hi_scr[...], w1_scr[half:, :])
    pre = a + pltpu.roll(bm, bm.shape[0] - 1, 0)
    hid = (pre * _sigmoid(pre)).astype(BF16)
    out = _dot(hid, w2_scr[...]).astype(o_ref.dtype)
    for h in range(NSA_KV_HEADS):
        o_ref[h] = out[h * nchunk:(h + 1) * nchunk]


def _compress(z3, pe_k, pe_v, w1_k, w1_v, w2_k, w2_v):
    bsz, s, _ = z3.shape
    hkv, hd = NSA_KV_HEADS, NSA_HD
    nchunk = s // CMP_STRIDE
    width = CMP_STRIDE * hd
    return pl.pallas_call(
        _cmp_kernel,
        grid=(2, bsz),
        in_specs=[
            pl.BlockSpec((None, s, hkv * hd), lambda kv, b: (b, 0, Z_NKV // (hkv * hd) + kv)),
            pl.BlockSpec((CMP_LEN, hd), lambda kv, b: (0, 0)),
            pl.BlockSpec((CMP_LEN, hd), lambda kv, b: (0, 0)),
            pl.BlockSpec((2 * width, CMP_HIDDEN), lambda kv, b: (0, 0)),
            pl.BlockSpec((2 * width, CMP_HIDDEN), lambda kv, b: (0, 0)),
            pl.BlockSpec((CMP_HIDDEN, hd), lambda kv, b: (0, 0)),
            pl.BlockSpec((CMP_HIDDEN, hd), lambda kv, b: (0, 0)),
        ],
        out_specs=pl.BlockSpec((None, None, hkv, nchunk, hd), lambda kv, b: (kv, b, 0, 0, 0)),
        out_shape=jax.ShapeDtypeStruct((2, bsz, hkv, nchunk, hd), BF16),
        scratch_shapes=[pltpu.VMEM((hkv, s, hd), F32),
                        pltpu.VMEM((hkv * nchunk, width), BF16),
                        pltpu.VMEM((hkv * nchunk, width), BF16),
                        pltpu.VMEM((CMP_LEN, hd), F32),
                        pltpu.VMEM((2 * width, CMP_HIDDEN), BF16),
                        pltpu.VMEM((CMP_HIDDEN, hd), BF16)],
        compiler_params=_params(("parallel", "parallel")),
        name="nsa_compress",
    )(z3, pe_k, pe_v, w1_k, w1_v, w2_k, w2_v)


NSA_TQ = 256
NSA_TK = NSA_TQ
LOG2E = 1.4426950408889634


NSA_HPS = 4


def _nsa_kernel(q_ref, sm_ref, kc_ref, vc_ref, ks_ref, vs_ref, kw_ref, vw_ref, mcs_ref,
                o_ref, sel_scr, gate_scr, p_scr, acc_scr):
    G, TQ, TK, HD = NSA_GROUP, NSA_TQ, NSA_TK, NSA_HD
    R = G * TQ
    heads = range(NSA_HPS)
    hk0 = pl.program_id(1) * NSA_HPS
    qi = pl.program_id(2)
    qs = qi * TQ

    def lanes4(x):
        return jnp.concatenate([x] * G, axis=1)

    def hcol(j):
        return slice(j * HD, (j + 1) * HD)

    q4 = []
    for j in heads:
        qj = jnp.concatenate([q_ref[:, (j * G + g) * HD:(j * G + g + 1) * HD] for g in range(G)], axis=0)
        q4.append((qj.astype(F32) * (HD ** -0.5 * LOG2E)).astype(BF16))

    nc = kc_ref.shape[1]
    tq_c = qs + lax.broadcasted_iota(jnp.int32, (nc, TQ), 1)
    cmp_end = lax.broadcasted_iota(jnp.int32, (nc, TQ), 0) * CMP_STRIDE + (CMP_LEN - 1)
    valid_c = lanes4(jnp.where(cmp_end <= tq_c, 1.0, 0.0)) > 0.5
    mcs = mcs_ref[...]
    nb = mcs.shape[0]
    blk = lax.broadcasted_iota(jnp.int32, (nb, TQ), 0)
    tq_b = qs + lax.broadcasted_iota(jnp.int32, (nb, TQ), 1)
    cur = tq_b // SLC_BLOCK
    forced = (blk == 0) | (blk == cur) | (blk == cur - 1)
    visible = blk * SLC_BLOCK <= tq_b
    o_c = []
    for j in heads:
        s_c = jnp.where(valid_c, _dot_nt(kc_ref[j], q4[j]), NEG_INF)
        e_c = jnp.where(valid_c, jnp.exp2(s_c - jnp.max(s_c, axis=0, keepdims=True)), 0.0)
        l_c = jnp.sum(e_c, axis=0, keepdims=True)
        p_c = e_c / jnp.where(l_c > 0.0, l_c, 1.0)
        o_c.append(_dot_tn(vc_ref[j], p_c.astype(BF16)))

        p_sum = p_c[:, 0:TQ]
        for g in range(1, G):
            p_sum = p_sum + p_c[:, g * TQ:(g + 1) * TQ]
        imp = sum(_dot(mcs, part) for part in _split3(p_sum))
        imp = jnp.where(forced, jnp.inf, imp)
        imp = jnp.where(visible, imp, -jnp.inf)
        rank = jnp.zeros((nb, TQ), F32)
        for kb in range(nb):
            rk = imp[kb:kb + 1, :]
            ahead = jnp.where(rk > imp, 1.0, jnp.where(rk == imp, jnp.where(blk > kb, 1.0, 0.0), 0.0))
            rank = rank + ahead
        sel_scr[j] = jnp.where(rank < float(min(SLC_TOPN, nb)), 0.0, NEG_INF)

    blocks_per_tile = TK // SLC_BLOCK

    def sel_bias(j, tile):
        return jnp.concatenate(
            [jnp.broadcast_to(sel_scr[j, pl.ds(tile * blocks_per_tile + i, 1), :], (SLC_BLOCK, TQ))
             for i in range(blocks_per_tile)], axis=0)

    krow = lax.broadcasted_iota(jnp.int32, (TK, TQ), 0)
    qcol = lax.broadcasted_iota(jnp.int32, (TK, TQ), 1)
    diag = pl.ds(pl.multiple_of(qs, TK), TK)
    m_d, l_d = [], []
    for j in heads:
        bias_d = jnp.where(krow <= qcol, sel_bias(j, qi), NEG_INF)
        s_d = _dot_nt(ks_ref[diag, hcol(j)], q4[j]) + lanes4(bias_d)
        m = jnp.max(s_d, axis=0, keepdims=True)
        p_d = jnp.exp2(s_d - m)
        m_d.append(m)
        l_d.append(jnp.sum(p_d, axis=0, keepdims=True))
        p_scr[j] = p_d.astype(BF16)

    def kv_step(t, carry):
        ms, ls, voff = carry
        prev = pl.ds(pl.multiple_of(voff, TK), TK)
        koff = pl.multiple_of(t * TK, TK)
        ms_n, ls_n = [], []
        for j in heads:
            pv = _dot_tn(vs_ref[prev, hcol(j)], p_scr[j])
            s = _dot_nt(ks_ref[pl.ds(koff, TK), hcol(j)], q4[j]) + lanes4(sel_bias(j, t))
            m_new = jnp.maximum(ms[j], jnp.max(s, axis=0, keepdims=True))
            alpha = jnp.exp2(ms[j] - m_new)
            p = jnp.exp2(s - m_new)
            ms_n.append(m_new)
            ls_n.append(alpha * ls[j] + jnp.sum(p, axis=0, keepdims=True))
            acc_scr[j] = alpha * (acc_scr[j] + pv)
            p_scr[j] = p.astype(BF16)
        return tuple(ms_n), tuple(ls_n), koff

    acc_scr[...] = jnp.zeros_like(acc_scr)
    _, l_s, voff = lax.fori_loop(0, qi, kv_step, (tuple(m_d), tuple(l_d), qs))
    last = pl.ds(pl.multiple_of(voff, TK), TK)
    o_s = [(acc_scr[j] + _dot_tn(vs_ref[last, hcol(j)], p_scr[j])) / l_s[j] for j in heads]

    span = WINDOW + TQ
    w0 = pl.multiple_of(jnp.maximum(qs - WINDOW, 0), TQ)
    kpos = w0 + lax.broadcasted_iota(jnp.int32, (span, TQ), 0)
    tq_w = qs + lax.broadcasted_iota(jnp.int32, (span, TQ), 1)
    bias_w = lanes4(jnp.where(kpos <= tq_w, jnp.where(kpos > tq_w - WINDOW, 0.0, NEG_INF), NEG_INF))
    o_w = []
    for j in heads:
        s_w = _dot_nt(kw_ref[pl.ds(w0, span), hcol(j)], q4[j]) + bias_w
        e_w = jnp.exp2(s_w - jnp.max(s_w, axis=0, keepdims=True))
        o_w.append(_dot_tn(vw_ref[pl.ds(w0, span), hcol(j)], e_w.astype(BF16))
                   / jnp.sum(e_w, axis=0, keepdims=True))

    gate_scr[...] = _sigmoid(sm_ref[...]).T
    for j in heads:
        def gate_row(branch):
            base = SMALL_NGATE + branch * NSA_HEADS + (hk0 + j) * G
            return jnp.concatenate([gate_scr[pl.ds(base + g, 1), :] for g in range(G)], axis=1)

        o = gate_row(0) * o_c[j] + gate_row(1) * o_s[j] + gate_row(2) * o_w[j]
        for g in range(G):
            o_ref[:, (j * G + g) * HD:(j * G + g + 1) * HD] = o[:, g * TQ:(g + 1) * TQ].T.astype(o_ref.dtype)


def _nsa(z3, zs3, cmp, mcs_t):
    bsz, s, _ = z3.shape
    hkv, hd, tq, hps = NSA_KV_HEADS, NSA_HD, NSA_TQ, NSA_HPS
    qw = hps * NSA_GROUP * hd
    kw = hps * hd
    nc = cmp.shape[3]
    nb = mcs_t.shape[0]

    def kv_spec(i):
        return pl.BlockSpec((None, s, kw), lambda b, h, q: (b, 0, (Z_NKV + i * hkv * hd) // kw + h))

    return pl.pallas_call(
        _nsa_kernel,
        grid=(bsz, hkv // hps, s // tq),
        in_specs=[
            pl.BlockSpec((None, tq, qw), lambda b, h, q: (b, q, Z_NQ // qw + h)),
            pl.BlockSpec((None, tq, LANES), lambda b, h, q: (b, q, 0)),
            pl.BlockSpec((None, None, hps, nc, hd), lambda b, h, q: (0, b, h, 0, 0)),
            pl.BlockSpec((None, None, hps, nc, hd), lambda b, h, q: (1, b, h, 0, 0)),
            kv_spec(2), kv_spec(3), kv_spec(4), kv_spec(5),
            pl.BlockSpec(mcs_t.shape, lambda b, h, q: (0, 0)),
        ],
        out_specs=pl.BlockSpec((None, tq, qw), lambda b, h, q: (b, q, h)),
        out_shape=jax.ShapeDtypeStruct((bsz, s, NSA_HEADS * hd), BF16),
        scratch_shapes=[pltpu.VMEM((hps, nb, tq), F32), pltpu.VMEM((LANES, tq), F32),
                        pltpu.VMEM((hps, NSA_TK, NSA_GROUP * tq), BF16),
                        pltpu.VMEM((hps, hd, NSA_GROUP * tq), F32)],
        compiler_params=_params(("parallel", "parallel", "arbitrary")),
        name="nsa_attn",
    )(z3, zs3, cmp, cmp, z3, z3, z3, z3, mcs_t)


def _merge_kernel(og_ref, on_ref, wg_ref, wn_ref, ga_ref, gb_ref, wo_ref, o_ref, wo_bf_ref, wg_scr, wn_scr):
    wo_bf_ref[...] = wo_ref[...].astype(BF16)

    @pl.when(pl.program_id(1) == 0)
    def _():
        wg_scr[...] = wg_ref[...].astype(BF16)
        wn_scr[...] = wn_ref[...].astype(BF16)

    a = _dot(og_ref[...], wg_scr[...])
    b = _dot(on_ref[...], wn_scr[...])
    ga = _sigmoid(ga_ref[...].astype(F32))
    gb = _sigmoid(gb_ref[...].astype(F32))
    o_ref[...] = (ga * a + gb * b).astype(o_ref.dtype)


def _merge(o_gla, o_nsa, w_pg, w_pn, z2, w_out, tm=512, tn=1024):
    m = o_gla.shape[0]
    n_i = m // tm
    slab = D_MODEL // ((D_MODEL // tn) * n_i)
    assert slab % 16 == 0 and slab * (D_MODEL // tn) * n_i == D_MODEL
    once = pl.Buffered(1)
    return pl.pallas_call(
        _merge_kernel,
        grid=(D_MODEL // tn, m // tm),
        in_specs=[
            pl.BlockSpec((tm, GLA_DV), lambda j, i: (i, 0)),
            pl.BlockSpec((tm, NSA_HEADS * NSA_HD), lambda j, i: (i, 0)),
            pl.BlockSpec((GLA_DV, tn), lambda j, i: (0, j), pipeline_mode=once),
            pl.BlockSpec((NSA_HEADS * NSA_HD, tn), lambda j, i: (0, j), pipeline_mode=once),
            pl.BlockSpec((tm, tn), lambda j, i: (i, Z_MG // tn + j)),
            pl.BlockSpec((tm, tn), lambda j, i: (i, (Z_MG + D_MODEL) // tn + j)),
            pl.BlockSpec((slab, D_MODEL), lambda j, i: (j * n_i + i, 0)),
        ],
        out_specs=[pl.BlockSpec((tm, tn), lambda j, i: (i, j)),
                   pl.BlockSpec((slab, D_MODEL), lambda j, i: (j * n_i + i, 0))],
        out_shape=[jax.ShapeDtypeStruct((m, D_MODEL), BF16),
                   jax.ShapeDtypeStruct((D_MODEL, D_MODEL), BF16)],
        scratch_shapes=[pltpu.VMEM((GLA_DV, tn), BF16), pltpu.VMEM((NSA_HEADS * NSA_HD, tn), BF16)],
        compiler_params=_params(("parallel", "arbitrary")),
        name="merge_proj",
    )(o_gla, o_nsa, w_pg, w_pn, z2, z2, w_out)


def _out_proj_kernel(mg_ref, w_ref, x_ref, g_ref, x1_ref, h_ref):
    x1 = x_ref[...] + _dot(mg_ref[...], w_ref[...])
    x1_ref[...] = x1
    ms = jnp.mean(x1 * x1, axis=-1, keepdims=True)
    h_ref[...] = (x1 * lax.rsqrt(ms + RMS_EPS) * g_ref[...]).astype(h_ref.dtype)


def _out_proj(merged, w_out, x2, g_ffn, tm=512):
    m = merged.shape[0]
    return pl.pallas_call(
        _out_proj_kernel,
        grid=(m // tm,),
        in_specs=[
            pl.BlockSpec((tm, D_MODEL), lambda i: (i, 0)),
            pl.BlockSpec((D_MODEL, D_MODEL), lambda i: (0, 0)),
            pl.BlockSpec((tm, D_MODEL), lambda i: (i, 0)),
            pl.BlockSpec((1, D_MODEL), lambda i: (0, 0)),
        ],
        out_specs=[
            pl.BlockSpec((tm, D_MODEL), lambda i: (i, 0)),
            pl.BlockSpec((tm, D_MODEL), lambda i: (i, 0)),
        ],
        out_shape=[jax.ShapeDtypeStruct((m, D_MODEL), F32),
                   jax.ShapeDtypeStruct((m, D_MODEL), BF16)],
        compiler_params=_params(("parallel",)),
        name="out_proj",
    )(merged, w_out, x2, g_ffn)


def _ffn_up_kernel(h_ref, wg_ref, wu_ref, wd_ref, o_ref, wd_bf_ref, wg_scr, wu_scr):
    wd_bf_ref[...] = wd_ref[...].astype(BF16)

    @pl.when(pl.program_id(1) == 0)
    def _():
        wg_scr[...] = wg_ref[...].astype(BF16)
        wu_scr[...] = wu_ref[...].astype(BF16)

    h = h_ref[...]
    g = _dot(h, wg_scr[...])
    u = _dot(h, wu_scr[...])
    o_ref[...] = (g * _sigmoid(g) * u).astype(o_ref.dtype)


def _ffn_up(h2, w_gate, w_up, w_down, tm=1024, tf=512):
    m = h2.shape[0]
    n_i = m // tm
    slab = D_FF // ((D_FF // tf) * n_i)
    assert slab % 16 == 0 and slab * (D_FF // tf) * n_i == D_FF
    return pl.pallas_call(
        _ffn_up_kernel,
        grid=(D_FF // tf, m // tm),
        in_specs=[
            pl.BlockSpec((tm, D_MODEL), lambda j, i: (i, 0)),
            pl.BlockSpec((D_MODEL, tf), lambda j, i: (0, j)),
            pl.BlockSpec((D_MODEL, tf), lambda j, i: (0, j)),
            pl.BlockSpec((slab, D_MODEL), lambda j, i: (j * n_i + i, 0)),
        ],
        out_specs=[pl.BlockSpec((tm, tf), lambda j, i: (i, j)),
                   pl.BlockSpec((slab, D_MODEL), lambda j, i: (j * n_i + i, 0))],
        out_shape=[jax.ShapeDtypeStruct((m, D_FF), BF16),
                   jax.ShapeDtypeStruct((D_FF, D_MODEL), BF16)],
        scratch_shapes=[pltpu.VMEM((D_MODEL, tf), BF16), pltpu.VMEM((D_MODEL, tf), BF16)],
        compiler_params=_params(("parallel", "arbitrary")),
        name="ffn_up",
    )(h2, w_gate, w_up, w_down)


def _ffn_down_kernel(a_ref, w_ref, x1_ref, g_ref, o_ref, *, final_norm):
    k = pl.program_id(1)

    @pl.when(k == 0)
    def _():
        o_ref[...] = x1_ref[...]

    o_ref[...] += _dot(a_ref[...], w_ref[...])

    if final_norm:
        @pl.when(k == pl.num_programs(1) - 1)
        def _():
            y = o_ref[...]
            ms = jnp.mean(y * y, axis=-1, keepdims=True)
            o_ref[...] = y * lax.rsqrt(ms + RMS_EPS) * g_ref[...]


def _ffn_down(act, w_down, x1, g_final, final_norm, tm=1024, tk=1408):
    m = act.shape[0]
    return pl.pallas_call(
        functools.partial(_ffn_down_kernel, final_norm=final_norm),
        grid=(m // tm, D_FF // tk),
        in_specs=[
            pl.BlockSpec((tm, tk), lambda i, k: (i, k)),
            pl.BlockSpec((tk, D_MODEL), lambda i, k: (k, 0)),
            pl.BlockSpec((tm, D_MODEL), lambda i, k: (i, 0)),
            pl.BlockSpec((1, D_MODEL), lambda i, k: (0, 0)),
        ],
        out_specs=pl.BlockSpec((tm, D_MODEL), lambda i, k: (i, 0)),
        out_shape=jax.ShapeDtypeStruct((m, D_MODEL), F32),
        compiler_params=_params(("parallel", "arbitrary")),
        name="ffn_down",
    )(act, w_down, x1, g_final)


def _cmp_to_slc_weights_t(seq):
    n_cmp = seq // CMP_STRIDE - 1
    n_slc = seq // SLC_BLOCK
    c0 = np.arange(n_cmp)[:, None] * CMP_STRIDE
    s0 = np.arange(n_slc)[None, :] * SLC_BLOCK
    ov = np.clip(np.minimum(c0 + CMP_LEN, s0 + SLC_BLOCK) - np.maximum(c0, s0), 0, None)
    m_cs = np.zeros((n_cmp + 1, n_slc), np.float32)
    m_cs[:n_cmp] = ov / CMP_LEN
    return jnp.asarray(m_cs.T, BF16)


def kernel(x, g_mix, w_in, w_gla_gate, b_gla_gate, g_gla_out, pe_cmp_k, w_cmp_k1, w_cmp_k2,
           pe_cmp_v, w_cmp_v1, w_cmp_v2, w_proj_gla, w_proj_nsa, w_out, g_ffn,
           w_ffn_gate, w_ffn_up, w_ffn_down, g_final):
    bsz, seq, d = x.shape
    m = bsz * seq
    depth = w_in.shape[0]
    mcs_t = _cmp_to_slc_weights_t(seq)
    xc = x.reshape(m, d)
    for l in range(depth):
        w_t = jnp.swapaxes(w_in[l], 0, 1)
        h, zs = _norm_small(xc, g_mix[l][None], w_t)
        z2 = _in_proj(h, w_t)
        z3 = z2.reshape(bsz, seq, Z_SMALL)
        zs3 = zs.reshape(bsz, seq, LANES)

        w_gate_pad = jnp.zeros((LANES, GLA_DK), F32).at[SMALL_GLR:SMALL_GLR + GLA_GATE_RANK].set(w_gla_gate[l])
        o_gla = _gla(z3, zs3, w_gate_pad.astype(BF16), b_gla_gate[l][None], g_gla_out[l][None])

        cmp = _compress(z3, pe_cmp_k[l], pe_cmp_v[l], w_cmp_k1[l], w_cmp_v1[l], w_cmp_k2[l], w_cmp_v2[l])
        o_nsa = _nsa(z3, zs3, cmp, mcs_t)

        merged, w_out_bf = _merge(o_gla.reshape(m, GLA_DV), o_nsa.reshape(m, NSA_HEADS * NSA_HD),
                                  w_proj_gla[l], w_proj_nsa[l], z2, w_out[l])
        x1, h2 = _out_proj(merged, w_out_bf, xc, g_ffn[l][None])
        act, w_down_bf = _ffn_up(h2, w_ffn_gate[l], w_ffn_up[l], w_ffn_down[l])
        xc = _ffn_down(act, w_down_bf, x1, g_final[None],
                       final_norm=(l == depth - 1))
    return xc.reshape(bsz, seq, d)
```

```python
import functools

import numpy as np
import jax
import jax.numpy as jnp
from jax import lax
from jax.experimental import pallas as pl
from jax.experimental.pallas import tpu as pltpu

F32 = jnp.float32
BF16 = jnp.bfloat16

D_MODEL = 2048
GLA_HEADS = 4
GLA_DK = D_MODEL // 2
GLA_DV = D_MODEL
GLA_HK = GLA_DK // GLA_HEADS
GLA_HV = GLA_DV // GLA_HEADS
GLA_GATE_RANK = 16
GLA_TAU = 16.0
GLA_CHUNK = 64

NSA_HEADS = 16
NSA_KV_HEADS = 4
NSA_GROUP = NSA_HEADS // NSA_KV_HEADS
NSA_HD = D_MODEL // NSA_HEADS
CMP_STRIDE = 16
CMP_LEN = 2 * CMP_STRIDE
CMP_HIDDEN = 256
SLC_BLOCK = 64
SLC_TOPN = 16
WINDOW = 512
D_FF = ((8 * D_MODEL // 3 + 255) // 256) * 256

IN_SPLIT_SIZES = (GLA_DK, GLA_DK, GLA_DV, GLA_DV, GLA_GATE_RANK,
                  NSA_HEADS * NSA_HD, 6 * NSA_KV_HEADS * NSA_HD, 3 * NSA_HEADS,
                  2 * D_MODEL)

NEG_INF = -1e30
RMS_EPS = 1e-6

Z_GQ = 0
Z_GK = Z_GQ + GLA_DK
Z_GV = Z_GK + GLA_DK
Z_GR = Z_GV + GLA_DV
Z_NQ = Z_GR + GLA_DV
Z_NKV = Z_NQ + NSA_HEADS * NSA_HD
Z_MG = Z_NKV + 6 * NSA_KV_HEADS * NSA_HD
Z_SMALL = Z_MG + 2 * D_MODEL
LANES = 128
SMALL_GLR = 0
SMALL_NGATE = GLA_GATE_RANK

VMEM_LIMIT = 60 * 1024 * 1024


def _dot(a, b):
    return jnp.dot(a, b, preferred_element_type=F32)


def _dot_nt(a, b):
    return lax.dot_general(a, b, (((1,), (1,)), ((), ())), preferred_element_type=F32)


def _dot_tn(a, b):
    return lax.dot_general(a, b, (((0,), (0,)), ((), ())), preferred_element_type=F32)


def _sigmoid(x):
    return 1.0 / (1.0 + jnp.exp(-x))


def _split2(x):
    hi = x.astype(BF16)
    lo = (x - hi.astype(F32)).astype(BF16)
    return hi, lo


def _split3(x):
    hi = x.astype(BF16)
    r1 = x - hi.astype(F32)
    mid = r1.astype(BF16)
    lo = (r1 - mid.astype(F32)).astype(BF16)
    return hi, mid, lo


def _params(sem, **kw):
    return pltpu.CompilerParams(dimension_semantics=sem, vmem_limit_bytes=VMEM_LIMIT, **kw)


_IN_OFFS = np.cumsum((0,) + IN_SPLIT_SIZES)
COL_GLR = int(_IN_OFFS[4])
COL_NGATE = int(_IN_OFFS[7])
assert COL_GLR % LANES == SMALL_GLR and COL_NGATE % LANES == SMALL_NGATE
assert SMALL_GLR + GLA_GATE_RANK == SMALL_NGATE and SMALL_NGATE + 3 * NSA_HEADS <= LANES


def _norm_kernel(x_ref, g_ref, wa_ref, wb_ref, h_ref, zs_ref):
    xf = x_ref[...]
    ms = jnp.mean(xf * xf, axis=-1, keepdims=True)
    h = (xf * lax.rsqrt(ms + RMS_EPS) * g_ref[...]).astype(BF16)
    h_ref[...] = h
    row = lax.broadcasted_iota(jnp.int32, wa_ref.shape, 0)
    w_small = jnp.where(row < SMALL_NGATE, wa_ref[...],
                        jnp.where(row < SMALL_NGATE + 3 * NSA_HEADS, wb_ref[...], 0.0))
    zs_ref[...] = _dot_nt(h, w_small.astype(BF16))


def _norm_small(x2, g, w_t, tm=512):
    m = x2.shape[0]
    return pl.pallas_call(
        _norm_kernel,
        grid=(m // tm,),
        in_specs=[
            pl.BlockSpec((tm, D_MODEL), lambda i: (i, 0)),
            pl.BlockSpec((1, D_MODEL), lambda i: (0, 0)),
            pl.BlockSpec((LANES, D_MODEL), lambda i: (COL_GLR // LANES, 0)),
            pl.BlockSpec((LANES, D_MODEL), lambda i: (COL_NGATE // LANES, 0)),
        ],
        out_specs=[pl.BlockSpec((tm, D_MODEL), lambda i: (i, 0)),
                   pl.BlockSpec((tm, LANES), lambda i: (i, 0))],
        out_shape=[jax.ShapeDtypeStruct((m, D_MODEL), BF16),
                   jax.ShapeDtypeStruct((m, LANES), F32)],
        compiler_params=_params(("parallel",)),
        name="norm_small",
    )(x2, g, w_t, w_t)


IN_TN = 1024
IN_SHIFT_B = GLA_GATE_RANK
IN_SHIFT_C = GLA_GATE_RANK + 3 * NSA_HEADS
IN_BLOCKS_A = Z_NQ // IN_TN
IN_BLOCKS_AB = Z_MG // IN_TN
IN_ROWS = 48
assert (IN_TN - IN_SHIFT_B) % IN_ROWS == 0 and (IN_TN - IN_SHIFT_C) % IN_ROWS == 0


def _in_proj_kernel(h_ref, wm_ref, we_ref, z_ref, w_scr):
    j = pl.program_id(0)

    def build(shift):
        if shift == 0:
            w_scr[...] = wm_ref[...].astype(BF16)
            return

        def body(r, carry):
            dst = pl.ds(pl.multiple_of(r * IN_ROWS, 16), IN_ROWS)
            src = pl.ds(pl.multiple_of(r * IN_ROWS + shift, 8), IN_ROWS)
            w_scr[dst, :] = wm_ref[src, :].astype(BF16)
            return carry
        lax.fori_loop(0, (IN_TN - shift) // IN_ROWS, body, 0)
        w_scr[IN_TN - shift:, :] = we_ref[:shift, :].astype(BF16)

    first = pl.program_id(1) == 0

    @pl.when(first & (j < IN_BLOCKS_A))
    def _():
        build(0)

    @pl.when(first & (j >= IN_BLOCKS_A) & (j < IN_BLOCKS_AB))
    def _():
        build(IN_SHIFT_B)

    @pl.when(first & (j >= IN_BLOCKS_AB))
    def _():
        build(IN_SHIFT_C)

    z_ref[...] = _dot_nt(h_ref[...], w_scr[...]).astype(z_ref.dtype)


def _in_proj(h, w_t, tm=2048):
    m = h.shape[0]
    tn = IN_TN
    return pl.pallas_call(
        _in_proj_kernel,
        grid=(Z_SMALL // tn, m // tm),
        in_specs=[
            pl.BlockSpec((tm, D_MODEL), lambda j, i: (i, 0)),
            pl.BlockSpec((tn, D_MODEL), lambda j, i: (j, 0)),
            pl.BlockSpec((LANES, D_MODEL), lambda j, i: ((j + 1) * (tn // LANES), 0)),
        ],
        out_specs=pl.BlockSpec((tm, tn), lambda j, i: (i, j)),
        out_shape=jax.ShapeDtypeStruct((m, Z_SMALL), BF16),
        scratch_shapes=[pltpu.VMEM((tn, D_MODEL), BF16)],
        compiler_params=_params(("parallel", "arbitrary")),
        name="in_proj",
    )(h, w_t, w_t)


GLA_HPS = 4


def _gla_kernel(q_ref, k_ref, v_ref, r_ref, sm_ref, wg_ref, bg_ref, gn_ref, o_ref,
                state_ref, *, n_chunks):
    @pl.when(pl.program_id(2) == 0)
    def _():
        state_ref[...] = jnp.zeros_like(state_ref)

    C = GLA_CHUNK
    HK, HV = GLA_HK, GLA_HV
    ts = n_chunks * C
    row = lax.broadcasted_iota(jnp.int32, (ts, ts), 0)
    col = lax.broadcasted_iota(jnp.int32, (ts, ts), 1)
    tri = jnp.where((row >= col) & ((row // C) == (col // C)), 1.0, 0.0).astype(BF16)
    crow = lax.broadcasted_iota(jnp.int32, (C, C), 0)
    ccol = lax.broadcasted_iota(jnp.int32, (C, C), 1)
    causal = crow >= ccol

    u = _dot(sm_ref[...].astype(BF16), wg_ref[...]) + bg_ref[...]
    log_a = (jnp.minimum(u, 0.0) - jnp.log(1.0 + jnp.exp(-jnp.abs(u)))) / GLA_TAU
    hi, lo = _split2(log_a)
    b_all = _dot(tri, hi) + _dot(tri, lo)
    gn = gn_ref[...]

    heads = range(GLA_HPS)
    kcol = [slice(j * HK, (j + 1) * HK) for j in heads]
    vcol = [slice(j * HV, (j + 1) * HV) for j in heads]
    b = [b_all[:, kcol[j]] for j in heads]
    k = [k_ref[:, kcol[j]].astype(F32) for j in heads]
    q_dec = [(q_ref[:, kcol[j]].astype(F32) * (HK ** -0.5) * jnp.exp(b[j])).astype(BF16) for j in heads]
    k_dec = [(k[j] * jnp.exp(-b[j])).astype(BF16) for j in heads]
    state = [state_ref[j] for j in heads]
    for c in range(n_chunks):
        rows = slice(c * C, (c + 1) * C)
        for j in heads:
            b_c = b[j][rows]
            b_last = b_c[C - 1:C, :]
            k_end = (k[j][rows] * jnp.exp(b_last - b_c)).astype(BF16)
            v_c = v_ref[rows, vcol[j]]
            attn = jnp.where(causal, _dot_nt(q_dec[j][rows], k_dec[j][rows]), 0.0).astype(BF16)
            o = _dot(attn, v_c) + _dot_nt(q_dec[j][rows], state[j].astype(BF16))
            state[j] = state[j] * jnp.exp(b_last) + _dot_tn(v_c, k_end)
            ms = jnp.mean(o * o, axis=-1, keepdims=True)
            y = o * lax.rsqrt(ms + RMS_EPS) * gn
            r = r_ref[rows, vcol[j]].astype(F32)
            o_ref[rows, vcol[j]] = (y * (r * _sigmoid(r))).astype(o_ref.dtype)
    for j in heads:
        state_ref[j] = state[j]


def _gla(z3, zs3, w_gate_pad, b_gate, g_out, ts=512):
    bsz, s, _ = z3.shape
    qb, vb = GLA_HPS * GLA_HK, GLA_HPS * GLA_HV
    return pl.pallas_call(
        functools.partial(_gla_kernel, n_chunks=ts // GLA_CHUNK),
        grid=(bsz, GLA_HEADS // GLA_HPS, s // ts),
        in_specs=[
            pl.BlockSpec((None, ts, qb), lambda b, h, i: (b, i, Z_GQ // qb + h)),
            pl.BlockSpec((None, ts, qb), lambda b, h, i: (b, i, Z_GK // qb + h)),
            pl.BlockSpec((None, ts, vb), lambda b, h, i: (b, i, Z_GV // vb + h)),
            pl.BlockSpec((None, ts, vb), lambda b, h, i: (b, i, Z_GR // vb + h)),
            pl.BlockSpec((None, ts, LANES), lambda b, h, i: (b, i, 0)),
            pl.BlockSpec((LANES, qb), lambda b, h, i: (0, h)),
            pl.BlockSpec((1, qb), lambda b, h, i: (0, h)),
            pl.BlockSpec((1, GLA_HV), lambda b, h, i: (0, 0)),
        ],
        out_specs=pl.BlockSpec((None, ts, vb), lambda b, h, i: (b, i, h)),
        out_shape=jax.ShapeDtypeStruct((bsz, s, GLA_DV), BF16),
        scratch_shapes=[pltpu.VMEM((GLA_HPS, GLA_HV, GLA_HK), F32)],
        compiler_params=_params(("parallel", "parallel", "arbitrary")),
        name="gla",
    )(z3, z3, z3, z3, zs3, w_gate_pad, b_gate, g_out)


def _cmp_kernel(t_ref, pek_ref, pev_ref, w1k_ref, w1v_ref, w2k_ref, w2v_ref, o_ref,
                tf_scr, lo_scr, hi_scr, pe_scr, w1_scr, w2_scr):
    hd = NSA_HD
    half = CMP_STRIDE * hd
    nchunk = lo_scr.shape[0] // NSA_KV_HEADS

    @pl.when(pl.program_id(0) == 0)
    def _():
        pe_scr[...] = pek_ref[...]
        w1_scr[...] = w1k_ref[...].astype(BF16)
        w2_scr[...] = w2k_ref[...].astype(BF16)

    @pl.when(pl.program_id(0) == 1)
    def _():
        pe_scr[...] = pev_ref[...]
        w1_scr[...] = w1v_ref[...].astype(BF16)
        w2_scr[...] = w2v_ref[...].astype(BF16)

    for h in range(NSA_KV_HEADS):
        tf_scr[h] = t_ref[:, h * hd:(h + 1) * hd].astype(F32)
        rows = slice(h * nchunk, (h + 1) * nchunk)
        for r in range(CMP_STRIDE):
            t_r = tf_scr.at[h][pl.ds(r, nchunk, stride=CMP_STRIDE), :]
            lo_scr[rows, r * hd:(r + 1) * hd] = (t_r + pe_scr[r:r + 1, :]).astype(BF16)
            hi_scr[rows, r * hd:(r + 1) * hd] = (t_r + pe_scr[CMP_STRIDE + r:CMP_STRIDE + r + 1, :]).astype(BF16)
    a = _dot(lo_scr[...], w1_scr[:half, :])
    bm = _dot(hi_scr[...], w1_scr[half:, :])
    pre = a + pltpu.roll(bm, bm.shape[0] - 1, 0)
    hid = (pre * _sigmoid(pre)).astype(BF16)
    out = _dot(hid, w2_scr[...]).astype(o_ref.dtype)
    for h in range(NSA_KV_HEADS):
        o_ref[h] = out[h * nchunk:(h + 1) * nchunk]


def _compress(z3, pe_k, pe_v, w1_k, w1_v, w2_k, w2_v):
    bsz, s, _ = z3.shape
    hkv, hd = NSA_KV_HEADS, NSA_HD
    nchunk = s // CMP_STRIDE
    width = CMP_STRIDE * hd
    return pl.pallas_call(
        _cmp_kernel,
        grid=(2, bsz),
        in_specs=[
            pl.BlockSpec((None, s, hkv * hd), lambda kv, b: (b, 0, Z_NKV // (hkv * hd) + kv)),
            pl.BlockSpec((CMP_LEN, hd), lambda kv, b: (0, 0)),
            pl.BlockSpec((CMP_LEN, hd), lambda kv, b: (0, 0)),
            pl.BlockSpec((2 * width, CMP_HIDDEN), lambda kv, b: (0, 0)),
            pl.BlockSpec((2 * width, CMP_HIDDEN), lambda kv, b: (0, 0)),
            pl.BlockSpec((CMP_HIDDEN, hd), lambda kv, b: (0, 0)),
            pl.BlockSpec((CMP_HIDDEN, hd), lambda kv, b: (0, 0)),
        ],
        out_specs=pl.BlockSpec((None, None, hkv, nchunk, hd), lambda kv, b: (kv, b, 0, 0, 0)),
        out_shape=jax.ShapeDtypeStruct((2, bsz, hkv, nchunk, hd), BF16),
        scratch_shapes=[pltpu.VMEM((hkv, s, hd), F32),
                        pltpu.VMEM((hkv * nchunk, width), BF16),
                        pltpu.VMEM((hkv * nchunk, width), BF16),
                        pltpu.VMEM((CMP_LEN, hd), F32),
                        pltpu.VMEM((2 * width, CMP_HIDDEN), BF16),
                        pltpu.VMEM((CMP_HIDDEN, hd), BF16)],
        compiler_params=_params(("parallel", "parallel")),
        name="nsa_compress",
    )(z3, pe_k, pe_v, w1_k, w1_v, w2_k, w2_v)


NSA_TQ = 256
NSA_TK = NSA_TQ
LOG2E = 1.4426950408889634


NSA_HPS = 4


def _nsa_kernel(q_ref, sm_ref, kc_ref, vc_ref, ks_ref, vs_ref, kw_ref, vw_ref, mcs_ref,
                o_ref, sel_scr, gate_scr, p_scr, acc_scr):
    G, TQ, TK, HD = NSA_GROUP, NSA_TQ, NSA_TK, NSA_HD
    R = G * TQ
    heads = range(NSA_HPS)
    hk0 = pl.program_id(1) * NSA_HPS
    qi = pl.program_id(2)
    qs = qi * TQ

    def lanes4(x):
        return jnp.concatenate([x] * G, axis=1)

    def hcol(j):
        return slice(j * HD, (j + 1) * HD)

    q4 = []
    for j in heads:
        qj = jnp.concatenate([q_ref[:, (j * G + g) * HD:(j * G + g + 1) * HD] for g in range(G)], axis=0)
        q4.append((qj.astype(F32) * (HD ** -0.5 * LOG2E)).astype(BF16))

    nc = kc_ref.shape[1]
    tq_c = qs + lax.broadcasted_iota(jnp.int32, (nc, TQ), 1)
    cmp_end = lax.broadcasted_iota(jnp.int32, (nc, TQ), 0) * CMP_STRIDE + (CMP_LEN - 1)
    valid_c = lanes4(jnp.where(cmp_end <= tq_c, 1.0, 0.0)) > 0.5
    mcs = mcs_ref[...]
    nb = mcs.shape[0]
    blk = lax.broadcasted_iota(jnp.int32, (nb, TQ), 0)
    tq_b = qs + lax.broadcasted_iota(jnp.int32, (nb, TQ), 1)
    cur = tq_b // SLC_BLOCK
    forced = (blk == 0) | (blk == cur) | (blk == cur - 1)
    visible = blk * SLC_BLOCK <= tq_b
    o_c = []
    for j in heads:
        s_c = jnp.where(valid_c, _dot_nt(kc_ref[j], q4[j]), NEG_INF)
        e_c = jnp.where(valid_c, jnp.exp2(s_c - jnp.max(s_c, axis=0, keepdims=True)), 0.0)
        l_c = jnp.sum(e_c, axis=0, keepdims=True)
        p_c = e_c / jnp.where(l_c > 0.0, l_c, 1.0)
        o_c.append(_dot_tn(vc_ref[j], p_c.astype(BF16)))

        p_sum = p_c[:, 0:TQ]
        for g in range(1, G):
            p_sum = p_sum + p_c[:, g * TQ:(g + 1) * TQ]
        imp = sum(_dot(mcs, part) for part in _split3(p_sum))
        imp = jnp.where(forced, jnp.inf, imp)
        imp = jnp.where(visible, imp, -jnp.inf)
        rank = jnp.zeros((nb, TQ), F32)
        for kb in range(nb):
            rk = imp[kb:kb + 1, :]
            ahead = jnp.where(rk > imp, 1.0, jnp.where(rk == imp, jnp.where(blk > kb, 1.0, 0.0), 0.0))
            rank = rank + ahead
        sel_scr[j] = jnp.where(rank < float(min(SLC_TOPN, nb)), 0.0, NEG_INF)

    blocks_per_tile = TK // SLC_BLOCK

    def sel_bias(j, tile):
        return jnp.concatenate(
            [jnp.broadcast_to(sel_scr[j, pl.ds(tile * blocks_per_tile + i, 1), :], (SLC_BLOCK, TQ))
             for i in range(blocks_per_tile)], axis=0)

    krow = lax.broadcasted_iota(jnp.int32, (TK, TQ), 0)
    qcol = lax.broadcasted_iota(jnp.int32, (TK, TQ), 1)
    diag = pl.ds(pl.multiple_of(qs, TK), TK)
    m_d, l_d = [], []
    for j in heads:
        bias_d = jnp.where(krow <= qcol, sel_bias(j, qi), NEG_INF)
        s_d = _dot_nt(ks_ref[diag, hcol(j)], q4[j]) + lanes4(bias_d)
        m = jnp.max(s_d, axis=0, keepdims=True)
        p_d = jnp.exp2(s_d - m)
        m_d.append(m)
        l_d.append(jnp.sum(p_d, axis=0, keepdims=True))
        p_scr[j] = p_d.astype(BF16)

    def kv_step(t, carry):
        ms, ls, voff = carry
        prev = pl.ds(pl.multiple_of(voff, TK), TK)
        koff = pl.multiple_of(t * TK, TK)
        ms_n, ls_n = [], []
        for j in heads:
            pv = _dot_tn(vs_ref[prev, hcol(j)], p_scr[j])
            s = _dot_nt(ks_ref[pl.ds(koff, TK), hcol(j)], q4[j]) + lanes4(sel_bias(j, t))
            m_new = jnp.maximum(ms[j], jnp.max(s, axis=0, keepdims=True))
            alpha = jnp.exp2(ms[j] - m_new)
            p = jnp.exp2(s - m_new)
            ms_n.append(m_new)
            ls_n.append(alpha * ls[j] + jnp.sum(p, axis=0, keepdims=True))
            acc_scr[j] = alpha * (acc_scr[j] + pv)
            p_scr[j] = p.astype(BF16)
        return tuple(ms_n), tuple(ls_n), koff

    acc_scr[...] = jnp.zeros_like(acc_scr)
    _, l_s, voff = lax.fori_loop(0, qi, kv_step, (tuple(m_d), tuple(l_d), qs))
    last = pl.ds(pl.multiple_of(voff, TK), TK)
    o_s = [(acc_scr[j] + _dot_tn(vs_ref[last, hcol(j)], p_scr[j])) / l_s[j] for j in heads]

    span = WINDOW + TQ
    w0 = pl.multiple_of(jnp.maximum(qs - WINDOW, 0), TQ)
    kpos = w0 + lax.broadcasted_iota(jnp.int32, (span, TQ), 0)
    tq_w = qs + lax.broadcasted_iota(jnp.int32, (span, TQ), 1)
    bias_w = lanes4(jnp.where(kpos <= tq_w, jnp.where(kpos > tq_w - WINDOW, 0.0, NEG_INF), NEG_INF))
    o_w = []
    for j in heads:
        s_w = _dot_nt(kw_ref[pl.ds(w0, span), hcol(j)], q4[j]) + bias_w
        e_w = jnp.exp2(s_w - jnp.max(s_w, axis=0, keepdims=True))
        o_w.append(_dot_tn(vw_ref[pl.ds(w0, span), hcol(j)], e_w.astype(BF16))
                   / jnp.sum(e_w, axis=0, keepdims=True))

    gate_scr[...] = _sigmoid(sm_ref[...]).T
    for j in heads:
        def gate_row(branch):
            base = SMALL_NGATE + branch * NSA_HEADS + (hk0 + j) * G
            return jnp.concatenate([gate_scr[pl.ds(base + g, 1), :] for g in range(G)], axis=1)

        o = gate_row(0) * o_c[j] + gate_row(1) * o_s[j] + gate_row(2) * o_w[j]
        for g in range(G):
            o_ref[:, (j * G + g) * HD:(j * G + g + 1) * HD] = o[:, g * TQ:(g + 1) * TQ].T.astype(o_ref.dtype)


def _nsa(z3, zs3, cmp, mcs_t):
    bsz, s, _ = z3.shape
    hkv, hd, tq, hps = NSA_KV_HEADS, NSA_HD, NSA_TQ, NSA_HPS
    qw = hps * NSA_GROUP * hd
    kw = hps * hd
    nc = cmp.shape[3]
    nb = mcs_t.shape[0]

    def kv_spec(i):
        return pl.BlockSpec((None, s, kw), lambda b, h, q: (b, 0, (Z_NKV + i * hkv * hd) // kw + h))

    return pl.pallas_call(
        _nsa_kernel,
        grid=(bsz, hkv // hps, s // tq),
        in_specs=[
            pl.BlockSpec((None, tq, qw), lambda b, h, q: (b, q, Z_NQ // qw + h)),
            pl.BlockSpec((None, tq, LANES), lambda b, h, q: (b, q, 0)),
            pl.BlockSpec((None, None, hps, nc, hd), lambda b, h, q: (0, b, h, 0, 0)),
            pl.BlockSpec((None, None, hps, nc, hd), lambda b, h, q: (1, b, h, 0, 0)),
            kv_spec(2), kv_spec(3), kv_spec(4), kv_spec(5),
            pl.BlockSpec(mcs_t.shape, lambda b, h, q: (0, 0)),
        ],
        out_specs=pl.BlockSpec((None, tq, qw), lambda b, h, q: (b, q, h)),
        out_shape=jax.ShapeDtypeStruct((bsz, s, NSA_HEADS * hd), BF16),
        scratch_shapes=[pltpu.VMEM((hps, nb, tq), F32), pltpu.VMEM((LANES, tq), F32),
                        pltpu.VMEM((hps, NSA_TK, NSA_GROUP * tq), BF16),
                        pltpu.VMEM((hps, hd, NSA_GROUP * tq), F32)],
        compiler_params=_params(("parallel", "parallel", "arbitrary")),
        name="nsa_attn",
    )(z3, zs3, cmp, cmp, z3, z3, z3, z3, mcs_t)


def _merge_kernel(og_ref, on_ref, wg_ref, wn_ref, ga_ref, gb_ref, wo_ref, o_ref, wo_bf_ref, wg_scr, wn_scr):
    wo_bf_ref[...] = wo_ref[...].astype(BF16)

    @pl.when(pl.program_id(1) == 0)
    def _():
        wg_scr[...] = wg_ref[...].astype(BF16)
        wn_scr[...] = wn_ref[...].astype(BF16)

    a = _dot(og_ref[...], wg_scr[...])
    b = _dot(on_ref[...], wn_scr[...])
    ga = _sigmoid(ga_ref[...].astype(F32))
    gb = _sigmoid(gb_ref[...].astype(F32))
    o_ref[...] = (ga * a + gb * b).astype(o_ref.dtype)


def _merge(o_gla, o_nsa, w_pg, w_pn, z2, w_out, tm=512, tn=1024):
    m = o_gla.shape[0]
    n_i = m // tm
    slab = D_MODEL // ((D_MODEL // tn) * n_i)
    assert slab % 16 == 0 and slab * (D_MODEL // tn) * n_i == D_MODEL
    once = pl.Buffered(1)
    return pl.pallas_call(
        _merge_kernel,
        grid=(D_MODEL // tn, m // tm),
        in_specs=[
            pl.BlockSpec((tm, GLA_DV), lambda j, i: (i, 0)),
            pl.BlockSpec((tm, NSA_HEADS * NSA_HD), lambda j, i: (i, 0)),
            pl.BlockSpec((GLA_DV, tn), lambda j, i: (0, j), pipeline_mode=once),
            pl.BlockSpec((NSA_HEADS * NSA_HD, tn), lambda j, i: (0, j), pipeline_mode=once),
            pl.BlockSpec((tm, tn), lambda j, i: (i, Z_MG // tn + j)),
            pl.BlockSpec((tm, tn), lambda j, i: (i, (Z_MG + D_MODEL) // tn + j)),
            pl.BlockSpec((slab, D_MODEL), lambda j, i: (j * n_i + i, 0)),
        ],
        out_specs=[pl.BlockSpec((tm, tn), lambda j, i: (i, j)),
                   pl.BlockSpec((slab, D_MODEL), lambda j, i: (j * n_i + i, 0))],
        out_shape=[jax.ShapeDtypeStruct((m, D_MODEL), BF16),
                   jax.ShapeDtypeStruct((D_MODEL, D_MODEL), BF16)],
        scratch_shapes=[pltpu.VMEM((GLA_DV, tn), BF16), pltpu.VMEM((NSA_HEADS * NSA_HD, tn), BF16)],
        compiler_params=_params(("parallel", "arbitrary")),
        name="merge_proj",
    )(o_gla, o_nsa, w_pg, w_pn, z2, z2, w_out)


def _out_proj_kernel(mg_ref, w_ref, x_ref, g_ref, x1_ref, h_ref):
    x1 = x_ref[...] + _dot(mg_ref[...], w_ref[...])
    x1_ref[...] = x1
    ms = jnp.mean(x1 * x1, axis=-1, keepdims=True)
    h_ref[...] = (x1 * lax.rsqrt(ms + RMS_EPS) * g_ref[...]).astype(h_ref.dtype)


def _out_proj(merged, w_out, x2, g_ffn, tm=512):
    m = merged.shape[0]
    return pl.pallas_call(
        _out_proj_kernel,
        grid=(m // tm,),
        in_specs=[
            pl.BlockSpec((tm, D_MODEL), lambda i: (i, 0)),
            pl.BlockSpec((D_MODEL, D_MODEL), lambda i: (0, 0)),
            pl.BlockSpec((tm, D_MODEL), lambda i: (i, 0)),
            pl.BlockSpec((1, D_MODEL), lambda i: (0, 0)),
        ],
        out_specs=[
            pl.BlockSpec((tm, D_MODEL), lambda i: (i, 0)),
            pl.BlockSpec((tm, D_MODEL), lambda i: (i, 0)),
        ],
        out_shape=[jax.ShapeDtypeStruct((m, D_MODEL), F32),
                   jax.ShapeDtypeStruct((m, D_MODEL), BF16)],
        compiler_params=_params(("parallel",)),
        name="out_proj",
    )(merged, w_out, x2, g_ffn)


def _ffn_up_kernel(h_ref, wg_ref, wu_ref, wd_ref, o_ref, wd_bf_ref, wg_scr, wu_scr):
    wd_bf_ref[...] = wd_ref[...].astype(BF16)

    @pl.when(pl.program_id(1) == 0)
    def _():
        wg_scr[...] = wg_ref[...].astype(BF16)
        wu_scr[...] = wu_ref[...].astype(BF16)

    h = h_ref[...]
    g = _dot(h, wg_scr[...])
    u = _dot(h, wu_scr[...])
    o_ref[...] = (g * _sigmoid(g) * u).astype(o_ref.dtype)


def _ffn_up(h2, w_gate, w_up, w_down, tm=1024, tf=512):
    m = h2.shape[0]
    n_i = m // tm
    slab = D_FF // ((D_FF // tf) * n_i)
    assert slab % 16 == 0 and slab * (D_FF // tf) * n_i == D_FF
    return pl.pallas_call(
        _ffn_up_kernel,
        grid=(D_FF // tf, m // tm),
        in_specs=[
            pl.BlockSpec((tm, D_MODEL), lambda j, i: (i, 0)),
            pl.BlockSpec((D_MODEL, tf), lambda j, i: (0, j)),
            pl.BlockSpec((D_MODEL, tf), lambda j, i: (0, j)),
            pl.BlockSpec((slab, D_MODEL), lambda j, i: (j * n_i + i, 0)),
        ],
        out_specs=[pl.BlockSpec((tm, tf), lambda j, i: (i, j)),
                   pl.BlockSpec((slab, D_MODEL), lambda j, i: (j * n_i + i, 0))],
        out_shape=[jax.ShapeDtypeStruct((m, D_FF), BF16),
                   jax.ShapeDtypeStruct((D_FF, D_MODEL), BF16)],
        scratch_shapes=[pltpu.VMEM((D_MODEL, tf), BF16), pltpu.VMEM((D_MODEL, tf), BF16)],
        compiler_params=_params(("parallel", "arbitrary")),
        name="ffn_up",
    )(h2, w_gate, w_up, w_down)


def _ffn_down_kernel(a_ref, w_ref, x1_ref, g_ref, o_ref, *, final_norm):
    k = pl.program_id(1)

    @pl.when(k == 0)
    def _():
        o_ref[...] = x1_ref[...]

    o_ref[...] += _dot(a_ref[...], w_ref[...])

    if final_norm:
        @pl.when(k == pl.num_programs(1) - 1)
        def _():
            y = o_ref[...]
            ms = jnp.mean(y * y, axis=-1, keepdims=True)
            o_ref[...] = y * lax.rsqrt(ms + RMS_EPS) * g_ref[...]


def _ffn_down(act, w_down, x1, g_final, final_norm, tm=1024, tk=1408):
    m = act.shape[0]
    return pl.pallas_call(
        functools.partial(_ffn_down_kernel, final_norm=final_norm),
        grid=(m // tm, D_FF // tk),
        in_specs=[
            pl.BlockSpec((tm, tk), lambda i, k: (i, k)),
            pl.BlockSpec((tk, D_MODEL), lambda i, k: (k, 0)),
            pl.BlockSpec((tm, D_MODEL), lambda i, k: (i, 0)),
            pl.BlockSpec((1, D_MODEL), lambda i, k: (0, 0)),
        ],
        out_specs=pl.BlockSpec((tm, D_MODEL), lambda i, k: (i, 0)),
        out_shape=jax.ShapeDtypeStruct((m, D_MODEL), F32),
        compiler_params=_params(("parallel", "arbitrary")),
        name="ffn_down",
    )(act, w_down, x1, g_final)


def _cmp_to_slc_weights_t(seq):
    n_cmp = seq // CMP_STRIDE - 1
    n_slc = seq // SLC_BLOCK
    c0 = np.arange(n_cmp)[:, None] * CMP_STRIDE
    s0 = np.arange(n_slc)[None, :] * SLC_BLOCK
    ov = np.clip(np.minimum(c0 + CMP_LEN, s0 + SLC_BLOCK) - np.maximum(c0, s0), 0, None)
    m_cs = np.zeros((n_cmp + 1, n_slc), np.float32)
    m_cs[:n_cmp] = ov / CMP_LEN
    return jnp.asarray(m_cs.T, BF16)


def kernel(x, g_mix, w_in, w_gla_gate, b_gla_gate, g_gla_out, pe_cmp_k, w_cmp_k1, w_cmp_k2,
           pe_cmp_v, w_cmp_v1, w_cmp_v2, w_proj_gla, w_proj_nsa, w_out, g_ffn,
           w_ffn_gate, w_ffn_up, w_ffn_down, g_final):
    bsz, seq, d = x.shape
    m = bsz * seq
    depth = w_in.shape[0]
    mcs_t = _cmp_to_slc_weights_t(seq)
    xc = x.reshape(m, d)
    for l in range(depth):
        w_t = jnp.swapaxes(w_in[l], 0, 1)
        h, zs = _norm_small(xc, g_mix[l][None], w_t)
        z2 = _in_proj(h, w_t)
        z3 = z2.reshape(bsz, seq, Z_SMALL)
        zs3 = zs.reshape(bsz, seq, LANES)

        w_gate_pad = jnp.zeros((LANES, GLA_DK), F32).at[SMALL_GLR:SMALL_GLR + GLA_GATE_RANK].set(w_gla_gate[l])
        o_gla = _gla(z3, zs3, w_gate_pad.astype(BF16), b_gla_gate[l][None], g_gla_out[l][None])

        cmp = _compress(z3, pe_cmp_k[l], pe_cmp_v[l], w_cmp_k1[l], w_cmp_v1[l], w_cmp_k2[l], w_cmp_v2[l])
        o_nsa = _nsa(z3, zs3, cmp, mcs_t)

        merged, w_out_bf = _merge(o_gla.reshape(m, GLA_DV), o_nsa.reshape(m, NSA_HEADS * NSA_HD),
                                  w_proj_gla[l], w_proj_nsa[l], z2, w_out[l])
        x1, h2 = _out_proj(merged, w_out_bf, xc, g_ffn[l][None])
        act, w_down_bf = _ffn_up(h2, w_ffn_gate[l], w_ffn_up[l], w_ffn_down[l])
        xc = _ffn_down(act, w_down_bf, x1, g_final[None],
                       final_norm=(l == depth - 1))
    return xc.reshape(bsz, seq, d)
```

```python
import functools

import numpy as np
import jax
import jax.numpy as jnp
from jax import lax
from jax.experimental import pallas as pl
from jax.experimental.pallas import tpu as pltpu

F32 = jnp.float32
BF16 = jnp.bfloat16

D_MODEL = 2048
GLA_HEADS = 4
GLA_DK = D_MODEL // 2
GLA_DV = D_MODEL
GLA_HK = GLA_DK // GLA_HEADS
GLA_HV = GLA_DV // GLA_HEADS
GLA_GATE_RANK = 16
GLA_TAU = 16.0
GLA_CHUNK = 64

NSA_HEADS = 16
NSA_KV_HEADS = 4
NSA_GROUP = NSA_HEADS // NSA_KV_HEADS
NSA_HD = D_MODEL // NSA_HEADS
CMP_STRIDE = 16
CMP_LEN = 2 * CMP_STRIDE
CMP_HIDDEN = 256
SLC_BLOCK = 64
SLC_TOPN = 16
WINDOW = 512
D_FF = ((8 * D_MODEL // 3 + 255) // 256) * 256

IN_SPLIT_SIZES = (GLA_DK, GLA_DK, GLA_DV, GLA_DV, GLA_GATE_RANK,
                  NSA_HEADS * NSA_HD, 6 * NSA_KV_HEADS * NSA_HD, 3 * NSA_HEADS,
                  2 * D_MODEL)

NEG_INF = -1e30
RMS_EPS = 1e-6

Z_GQ = 0
Z_GK = Z_GQ + GLA_DK
Z_GV = Z_GK + GLA_DK
Z_GR = Z_GV + GLA_DV
Z_NQ = Z_GR + GLA_DV
Z_NKV = Z_NQ + NSA_HEADS * NSA_HD
Z_MG = Z_NKV + 6 * NSA_KV_HEADS * NSA_HD
Z_SMALL = Z_MG + 2 * D_MODEL
LANES = 128
SMALL_GLR = 0
SMALL_NGATE = GLA_GATE_RANK

VMEM_LIMIT = 60 * 1024 * 1024


def _dot(a, b):
    return jnp.dot(a, b, preferred_element_type=F32)


def _dot_nt(a, b):
    return lax.dot_general(a, b, (((1,), (1,)), ((), ())), preferred_element_type=F32)


def _dot_tn(a, b):
    return lax.dot_general(a, b, (((0,), (0,)), ((), ())), preferred_element_type=F32)


def _sigmoid(x):
    return 1.0 / (1.0 + jnp.exp(-x))


def _split2(x):
    hi = x.astype(BF16)
    lo = (x - hi.astype(F32)).astype(BF16)
    return hi, lo


def _split3(x):
    hi = x.astype(BF16)
    r1 = x - hi.astype(F32)
    mid = r1.astype(BF16)
    lo = (r1 - mid.astype(F32)).astype(BF16)
    return hi, mid, lo


def _params(sem, **kw):
    return pltpu.CompilerParams(dimension_semantics=sem, vmem_limit_bytes=VMEM_LIMIT, **kw)


_IN_OFFS = np.cumsum((0,) + IN_SPLIT_SIZES)
COL_GLR = int(_IN_OFFS[4])
COL_NGATE = int(_IN_OFFS[7])
assert COL_GLR % LANES == SMALL_GLR and COL_NGATE % LANES == SMALL_NGATE
assert SMALL_GLR + GLA_GATE_RANK == SMALL_NGATE and SMALL_NGATE + 3 * NSA_HEADS <= LANES


def _norm_kernel(x_ref, g_ref, wa_ref, wb_ref, h_ref, zs_ref):
    xf = x_ref[...]
    ms = jnp.mean(xf * xf, axis=-1, keepdims=True)
    h = (xf * lax.rsqrt(ms + RMS_EPS) * g_ref[...]).astype(BF16)
    h_ref[...] = h
    row = lax.broadcasted_iota(jnp.int32, wa_ref.shape, 0)
    w_small = jnp.where(row < SMALL_NGATE, wa_ref[...],
                        jnp.where(row < SMALL_NGATE + 3 * NSA_HEADS, wb_ref[...], 0.0))
    zs_ref[...] = _dot_nt(h, w_small.astype(BF16))


def _norm_small(x2, g, w_t, tm=512):
    m = x2.shape[0]
    return pl.pallas_call(
        _norm_kernel,
        grid=(m // tm,),
        in_specs=[
            pl.BlockSpec((tm, D_MODEL), lambda i: (i, 0)),
            pl.BlockSpec((1, D_MODEL), lambda i: (0, 0)),
            pl.BlockSpec((LANES, D_MODEL), lambda i: (COL_GLR // LANES, 0)),
            pl.BlockSpec((LANES, D_MODEL), lambda i: (COL_NGATE // LANES, 0)),
        ],
        out_specs=[pl.BlockSpec((tm, D_MODEL), lambda i: (i, 0)),
                   pl.BlockSpec((tm, LANES), lambda i: (i, 0))],
        out_shape=[jax.ShapeDtypeStruct((m, D_MODEL), BF16),
                   jax.ShapeDtypeStruct((m, LANES), F32)],
        compiler_params=_params(("parallel",)),
        name="norm_small",
    )(x2, g, w_t, w_t)


IN_TN = 1024
IN_SHIFT_B = GLA_GATE_RANK
IN_SHIFT_C = GLA_GATE_RANK + 3 * NSA_HEADS
IN_BLOCKS_A = Z_NQ // IN_TN
IN_BLOCKS_AB = Z_MG // IN_TN
IN_ROWS = 48
assert (IN_TN - IN_SHIFT_B) % IN_ROWS == 0 and (IN_TN - IN_SHIFT_C) % IN_ROWS == 0


def _in_proj_kernel(h_ref, wm_ref, we_ref, z_ref, w_scr):
    j = pl.program_id(0)

    def build(shift):
        if shift == 0:
            w_scr[...] = wm_ref[...].astype(BF16)
            return

        def body(r, carry):
            dst = pl.ds(pl.multiple_of(r * IN_ROWS, 16), IN_ROWS)
            src = pl.ds(pl.multiple_of(r * IN_ROWS + shift, 8), IN_ROWS)
            w_scr[dst, :] = wm_ref[src, :].astype(BF16)
            return carry
        lax.fori_loop(0, (IN_TN - shift) // IN_ROWS, body, 0)
        w_scr[IN_TN - shift:, :] = we_ref[:shift, :].astype(BF16)

    first = pl.program_id(1) == 0

    @pl.when(first & (j < IN_BLOCKS_A))
    def _():
        build(0)

    @pl.when(first & (j >= IN_BLOCKS_A) & (j < IN_BLOCKS_AB))
    def _():
        build(IN_SHIFT_B)

    @pl.when(first & (j >= IN_BLOCKS_AB))
    def _():
        build(IN_SHIFT_C)

    z_ref[...] = _dot_nt(h_ref[...], w_scr[...]).astype(z_ref.dtype)


def _in_proj(h, w_t, tm=2048):
    m = h.shape[0]
    tn = IN_TN
    return pl.pallas_call(
        _in_proj_kernel,
        grid=(Z_SMALL // tn, m // tm),
        in_specs=[
            pl.BlockSpec((tm, D_MODEL), lambda j, i: (i, 0)),
            pl.BlockSpec((tn, D_MODEL), lambda j, i: (j, 0)),
            pl.BlockSpec((LANES, D_MODEL), lambda j, i: ((j + 1) * (tn // LANES), 0)),
        ],
        out_specs=pl.BlockSpec((tm, tn), lambda j, i: (i, j)),
        out_shape=jax.ShapeDtypeStruct((m, Z_SMALL), BF16),
        scratch_shapes=[pltpu.VMEM((tn, D_MODEL), BF16)],
        compiler_params=_params(("parallel", "arbitrary")),
        name="in_proj",
    )(h, w_t, w_t)


GLA_HPS = 4


def _gla_kernel(q_ref, k_ref, v_ref, r_ref, sm_ref, wg_ref, bg_ref, gn_ref, o_ref,
                state_ref, *, n_chunks):
    @pl.when(pl.program_id(2) == 0)
    def _():
        state_ref[...] = jnp.zeros_like(state_ref)

    C = GLA_CHUNK
    HK, HV = GLA_HK, GLA_HV
    ts = n_chunks * C
    row = lax.broadcasted_iota(jnp.int32, (ts, ts), 0)
    col = lax.broadcasted_iota(jnp.int32, (ts, ts), 1)
    tri = jnp.where((row >= col) & ((row // C) == (col // C)), 1.0, 0.0).astype(BF16)
    crow = lax.broadcasted_iota(jnp.int32, (C, C), 0)
    ccol = lax.broadcasted_iota(jnp.int32, (C, C), 1)
    causal = crow >= ccol

    u = _dot(sm_ref[...].astype(BF16), wg_ref[...]) + bg_ref[...]
    log_a = (jnp.minimum(u, 0.0) - jnp.log(1.0 + jnp.exp(-jnp.abs(u)))) / GLA_TAU
    hi, lo = _split2(log_a)
    b_all = _dot(tri, hi) + _dot(tri, lo)
    gn = gn_ref[...]

    heads = range(GLA_HPS)
    kcol = [slice(j * HK, (j + 1) * HK) for j in heads]
    vcol = [slice(j * HV, (j + 1) * HV) for j in heads]
    b = [b_all[:, kcol[j]] for j in heads]
    k = [k_ref[:, kcol[j]].astype(F32) for j in heads]
    q_dec = [(q_ref[:, kcol[j]].astype(F32) * (HK ** -0.5) * jnp.exp(b[j])).astype(BF16) for j in heads]
    k_dec = [(k[j] * jnp.exp(-b[j])).astype(BF16) for j in heads]
    state = [state_ref[j] for j in heads]
    for c in range(n_chunks):
        rows = slice(c * C, (c + 1) * C)
        for j in heads:
            b_c = b[j][rows]
            b_last = b_c[C - 1:C, :]
            k_end = (k[j][rows] * jnp.exp(b_last - b_c)).astype(BF16)
            v_c = v_ref[rows, vcol[j]]
            attn = jnp.where(causal, _dot_nt(q_dec[j][rows], k_dec[j][rows]), 0.0).astype(BF16)
            o = _dot(attn, v_c) + _dot_nt(q_dec[j][rows], state[j].astype(BF16))
            state[j] = state[j] * jnp.exp(b_last) + _dot_tn(v_c, k_end)
            ms = jnp.mean(o * o, axis=-1, keepdims=True)
            y = o * lax.rsqrt(ms + RMS_EPS) * gn
            r = r_ref[rows, vcol[j]].astype(F32)
            o_ref[rows, vcol[j]] = (y * (r * _sigmoid(r))).astype(o_ref.dtype)
    for j in heads:
        state_ref[j] = state[j]


def _gla(z3, zs3, w_gate_pad, b_gate, g_out, ts=512):
    bsz, s, _ = z3.shape
    qb, vb = GLA_HPS * GLA_HK, GLA_HPS * GLA_HV
    return pl.pallas_call(
        functools.partial(_gla_kernel, n_chunks=ts // GLA_CHUNK),
        grid=(bsz, GLA_HEADS // GLA_HPS, s // ts),
        in_specs=[
            pl.BlockSpec((None, ts, qb), lambda b, h, i: (b, i, Z_GQ // qb + h)),
            pl.BlockSpec((None, ts, qb), lambda b, h, i: (b, i, Z_GK // qb + h)),
            pl.BlockSpec((None, ts, vb), lambda b, h, i: (b, i, Z_GV // vb + h)),
            pl.BlockSpec((None, ts, vb), lambda b, h, i: (b, i, Z_GR // vb + h)),
            pl.BlockSpec((None, ts, LANES), lambda b, h, i: (b, i, 0)),
            pl.BlockSpec((LANES, qb), lambda b, h, i: (0, h)),
            pl.BlockSpec((1, qb), lambda b, h, i: (0, h)),
            pl.BlockSpec((1, GLA_HV), lambda b, h, i: (0, 0)),
        ],
        out_specs=pl.BlockSpec((None, ts, vb), lambda b, h, i: (b, i, h)),
        out_shape=jax.ShapeDtypeStruct((bsz, s, GLA_DV), BF16),
        scratch_shapes=[pltpu.VMEM((GLA_HPS, GLA_HV, GLA_HK), F32)],
        compiler_params=_params(("parallel", "parallel", "arbitrary")),
        name="gla",
    )(z3, z3, z3, z3, zs3, w_gate_pad, b_gate, g_out)


def _cmp_kernel(t_ref, pek_ref, pev_ref, w1k_ref, w1v_ref, w2k_ref, w2v_ref, o_ref,
                tf_scr, lo_scr, hi_scr, pe_scr, w1_scr, w2_scr):
    hd = NSA_HD
    half = CMP_STRIDE * hd
    nchunk = lo_scr.shape[0] // NSA_KV_HEADS

    @pl.when(pl.program_id(0) == 0)
    def _():
        pe_scr[...] = pek_ref[...]
        w1_scr[...] = w1k_ref[...].astype(BF16)
        w2_scr[...] = w2k_ref[...].astype(BF16)

    @pl.when(pl.program_id(0) == 1)
    def _():
        pe_scr[...] = pev_ref[...]
        w1_scr[...] = w1v_ref[...].astype(BF16)
        w2_scr[...] = w2v_ref[...].astype(BF16)

    for h in range(NSA_KV_HEADS):
        tf_scr[h] = t_ref[:, h * hd:(h + 1) * hd].astype(F32)
        rows = slice(h * nchunk, (h + 1) * nchunk)
        for r in range(CMP_STRIDE):
            t_r = tf_scr.at[h][pl.ds(r, nchunk, stride=CMP_STRIDE), :]
            lo_scr[rows, r * hd:(r + 1) * hd] = (t_r + pe_scr[r:r + 1, :]).astype(BF16)
            hi_scr[rows, r * hd:(r + 1) * hd] = (t_r + pe_scr[CMP_STRIDE + r:CMP_STRIDE + r + 1, :]).astype(BF16)
    a = _dot(lo_scr[...], w1_scr[:half, :])
    bm = _dot(hi_scr[...], w1_scr[half:, :])
    pre = a + pltpu.roll(bm, bm.shape[0] - 1, 0)
    hid = (pre * _sigmoid(pre)).astype(BF16)
    out = _dot(hid, w2_scr[...]).astype(o_ref.dtype)
    for h in range(NSA_KV_HEADS):
        o_ref[h] = out[h * nchunk:(h + 1) * nchunk]


def _compress(z3, pe_k, pe_v, w1_k, w1_v, w2_k, w2_v):
    bsz, s, _ = z3.shape
    hkv, hd = NSA_KV_HEADS, NSA_HD
    nchunk = s // CMP_STRIDE
    width = CMP_STRIDE * hd
    return pl.pallas_call(
        _cmp_kernel,
        grid=(2, bsz),
        in_specs=[
            pl.BlockSpec((None, s, hkv * hd), lambda kv, b: (b, 0, Z_NKV // (hkv * hd) + kv)),
            pl.BlockSpec((CMP_LEN, hd), lambda kv, b: (0, 0)),
            pl.BlockSpec((CMP_LEN, hd), lambda kv, b: (0, 0)),
            pl.BlockSpec((2 * width, CMP_HIDDEN), lambda kv, b: (0, 0)),
            pl.BlockSpec((2 * width, CMP_HIDDEN), lambda kv, b: (0, 0)),
            pl.BlockSpec((CMP_HIDDEN, hd), lambda kv, b: (0, 0)),
            pl.BlockSpec((CMP_HIDDEN, hd), lambda kv, b: (0, 0)),
        ],
        out_specs=pl.BlockSpec((None, None, hkv, nchunk, hd), lambda kv, b: (kv, b, 0, 0, 0)),
        out_shape=jax.ShapeDtypeStruct((2, bsz, hkv, nchunk, hd), BF16),
        scratch_shapes=[pltpu.VMEM((hkv, s, hd), F32),
                        pltpu.VMEM((hkv * nchunk, width), BF16),
                        pltpu.VMEM((hkv * nchunk, width), BF16),
                        pltpu.VMEM((CMP_LEN, hd), F32),
                        pltpu.VMEM((2 * width, CMP_HIDDEN), BF16),
                        pltpu.VMEM((CMP_HIDDEN, hd), BF16)],
        compiler_params=_params(("parallel", "parallel")),
        name="nsa_compress",
    )(z3, pe_k, pe_v, w1_k, w1_v, w2_k, w2_v)


NSA_TQ = 256
NSA_TK = NSA_TQ
LOG2E = 1.4426950408889634


NSA_HPS = 4


def _nsa_kernel(q_ref, sm_ref, kc_ref, vc_ref, ks_ref, vs_ref, kw_ref, vw_ref, mcs_ref,
                o_ref, sel_scr, gate_scr, p_scr, acc_scr):
    G, TQ, TK, HD = NSA_GROUP, NSA_TQ, NSA_TK, NSA_HD
    R = G * TQ
    heads = range(NSA_HPS)
    hk0 = pl.program_id(1) * NSA_HPS
    qi = pl.program_id(2)
    qs = qi * TQ

    def lanes4(x):
        return jnp.concatenate([x] * G, axis=1)

    def hcol(j):
        return slice(j * HD, (j + 1) * HD)

    q4 = []
    for j in heads:
        qj = jnp.concatenate([q_ref[:, (j * G + g) * HD:(j * G + g + 1) * HD] for g in range(G)], axis=0)
        q4.append((qj.astype(F32) * (HD ** -0.5 * LOG2E)).astype(BF16))

    nc = kc_ref.shape[1]
    tq_c = qs + lax.broadcasted_iota(jnp.int32, (nc, TQ), 1)
    cmp_end = lax.broadcasted_iota(jnp.int32, (nc, TQ), 0) * CMP_STRIDE + (CMP_LEN - 1)
    valid_c = lanes4(jnp.where(cmp_end <= tq_c, 1.0, 0.0)) > 0.5
    mcs = mcs_ref[...]
    nb = mcs.shape[0]
    blk = lax.broadcasted_iota(jnp.int32, (nb, TQ), 0)
    tq_b = qs + lax.broadcasted_iota(jnp.int32, (nb, TQ), 1)
    cur = tq_b // SLC_BLOCK
    forced = (blk == 0) | (blk == cur) | (blk == cur - 1)
    visible = blk * SLC_BLOCK <= tq_b
    o_c = []
    for j in heads:
        s_c = jnp.where(valid_c, _dot_nt(kc_ref[j], q4[j]), NEG_INF)
        e_c = jnp.where(valid_c, jnp.exp2(s_c - jnp.max(s_c, axis=0, keepdims=True)), 0.0)
        l_c = jnp.sum(e_c, axis=0, keepdims=True)
        p_c = e_c * (1.0 / jnp.where(l_c > 0.0, l_c, 1.0))
        o_c.append(_dot_tn(vc_ref[j], p_c.astype(BF16)))

        p_sum = p_c[:, 0:TQ]
        for g in range(1, G):
            p_sum = p_sum + p_c[:, g * TQ:(g + 1) * TQ]
        imp = sum(_dot(mcs, part) for part in _split3(p_sum))
        imp = jnp.where(forced, jnp.inf, imp)
        imp = jnp.where(visible, imp, -jnp.inf)
        rank = jnp.zeros((nb, TQ), F32)
        for kb in range(nb):
            rk = imp[kb:kb + 1, :]
            ahead = jnp.where(rk > imp, 1.0, jnp.where(rk == imp, jnp.where(blk > kb, 1.0, 0.0), 0.0))
            rank = rank + ahead
        sel_scr[j] = jnp.where(rank < float(min(SLC_TOPN, nb)), 0.0, NEG_INF)

    span = WINDOW + TQ
    w0 = pl.multiple_of(jnp.maximum(qs - WINDOW, 0), TQ)
    kpos = w0 + lax.broadcasted_iota(jnp.int32, (span, TQ), 0)
    tq_w = qs + lax.broadcasted_iota(jnp.int32, (span, TQ), 1)
    bias_w = lanes4(jnp.where(kpos <= tq_w, jnp.where(kpos > tq_w - WINDOW, 0.0, NEG_INF), NEG_INF))
    o_w = []
    for j in heads:
        s_w = _dot_nt(kw_ref[pl.ds(w0, span), hcol(j)], q4[j]) + bias_w
        e_w = jnp.exp2(s_w - jnp.max(s_w, axis=0, keepdims=True)).astype(BF16)
        v_aug = jnp.concatenate([vw_ref[pl.ds(w0, span), hcol(j)], jnp.ones((span, 8), BF16)], axis=1)
        pv = _dot_tn(v_aug, e_w)
        o_w.append(pv[:HD] * (1.0 / pv[HD:HD + 1]))

    blocks_per_tile = TK // SLC_BLOCK

    def sel_bias(j, tile):
        return jnp.concatenate(
            [jnp.broadcast_to(sel_scr[j, pl.ds(tile * blocks_per_tile + i, 1), :], (SLC_BLOCK, TQ))
             for i in range(blocks_per_tile)], axis=0)

    krow = lax.broadcasted_iota(jnp.int32, (TK, TQ), 0)
    qcol = lax.broadcasted_iota(jnp.int32, (TK, TQ), 1)
    diag = pl.ds(pl.multiple_of(qs, TK), TK)
    m_d = []
    for j in heads:
        bias_d = jnp.where(krow <= qcol, sel_bias(j, qi), NEG_INF)
        s_d = _dot_nt(ks_ref[diag, hcol(j)], q4[j]) + lanes4(bias_d)
        m = jnp.max(s_d, axis=0, keepdims=True)
        m_d.append(m)
        p_scr[j] = jnp.exp2(s_d - m).astype(BF16)

    ones_col = jnp.ones((TK, 8), BF16)

    def pv_aug(rows, j):
        return _dot_tn(jnp.concatenate([vs_ref[rows, hcol(j)], ones_col], axis=1), p_scr[j])

    def kv_step(t, carry):
        ms, voff = carry
        prev = pl.ds(pl.multiple_of(voff, TK), TK)
        koff = pl.multiple_of(t * TK, TK)
        ms_n = []
        for j in heads:
            pv = pv_aug(prev, j)
            s = _dot_nt(ks_ref[pl.ds(koff, TK), hcol(j)], q4[j]) + lanes4(sel_bias(j, t))
            m_new = jnp.maximum(ms[j], jnp.max(s, axis=0, keepdims=True))
            alpha = jnp.exp2(ms[j] - m_new)
            ms_n.append(m_new)
            acc_scr[j] = alpha * (acc_scr[j] + pv)
            p_scr[j] = jnp.exp2(s - m_new).astype(BF16)
        return tuple(ms_n), koff

    acc_scr[...] = jnp.zeros_like(acc_scr)
    _, voff = lax.fori_loop(0, qi, kv_step, (tuple(m_d), qs))
    last = pl.ds(pl.multiple_of(voff, TK), TK)
    o_s = []
    for j in heads:
        acc = acc_scr[j] + pv_aug(last, j)
        o_s.append(acc[:HD] * (1.0 / acc[HD:HD + 1]))

    gate_scr[...] = _sigmoid(sm_ref[...]).T
    for j in heads:
        def gate_row(branch):
            base = SMALL_NGATE + branch * NSA_HEADS + (hk0 + j) * G
            return jnp.concatenate([gate_scr[pl.ds(base + g, 1), :] for g in range(G)], axis=1)

        o = gate_row(0) * o_c[j] + gate_row(1) * o_s[j] + gate_row(2) * o_w[j]
        for g in range(G):
            o_ref[:, (j * G + g) * HD:(j * G + g + 1) * HD] = o[:, g * TQ:(g + 1) * TQ].T.astype(o_ref.dtype)


def _nsa(z3, zs3, cmp, mcs_t):
    bsz, s, _ = z3.shape
    hkv, hd, tq, hps = NSA_KV_HEADS, NSA_HD, NSA_TQ, NSA_HPS
    qw = hps * NSA_GROUP * hd
    kw = hps * hd
    nc = cmp.shape[3]
    nb = mcs_t.shape[0]

    def kv_spec(i):
        return pl.BlockSpec((None, s, kw), lambda b, h, q: (b, 0, (Z_NKV + i * hkv * hd) // kw + h))

    return pl.pallas_call(
        _nsa_kernel,
        grid=(bsz, hkv // hps, s // tq),
        in_specs=[
            pl.BlockSpec((None, tq, qw), lambda b, h, q: (b, q, Z_NQ // qw + h)),
            pl.BlockSpec((None, tq, LANES), lambda b, h, q: (b, q, 0)),
            pl.BlockSpec((None, None, hps, nc, hd), lambda b, h, q: (0, b, h, 0, 0)),
            pl.BlockSpec((None, None, hps, nc, hd), lambda b, h, q: (1, b, h, 0, 0)),
            kv_spec(2), kv_spec(3), kv_spec(4), kv_spec(5),
            pl.BlockSpec(mcs_t.shape, lambda b, h, q: (0, 0)),
        ],
        out_specs=pl.BlockSpec((None, tq, qw), lambda b, h, q: (b, q, h)),
        out_shape=jax.ShapeDtypeStruct((bsz, s, NSA_HEADS * hd), BF16),
        scratch_shapes=[pltpu.VMEM((hps, nb, tq), F32), pltpu.VMEM((LANES, tq), F32),
                        pltpu.VMEM((hps, NSA_TK, NSA_GROUP * tq), BF16),
                        pltpu.VMEM((hps, hd + 8, NSA_GROUP * tq), F32)],
        compiler_params=_params(("parallel", "parallel", "arbitrary")),
        name="nsa_attn",
    )(z3, zs3, cmp, cmp, z3, z3, z3, z3, mcs_t)


def _merge_kernel(og_ref, on_ref, wg_ref, wn_ref, ga_ref, gb_ref, wo_ref, o_ref, wo_bf_ref, wg_scr, wn_scr):
    wo_bf_ref[...] = wo_ref[...].astype(BF16)

    @pl.when(pl.program_id(1) == 0)
    def _():
        wg_scr[...] = wg_ref[...].astype(BF16)
        wn_scr[...] = wn_ref[...].astype(BF16)

    a = _dot(og_ref[...], wg_scr[...])
    b = _dot(on_ref[...], wn_scr[...])
    ga = _sigmoid(ga_ref[...].astype(F32))
    gb = _sigmoid(gb_ref[...].astype(F32))
    o_ref[...] = (ga * a + gb * b).astype(o_ref.dtype)


def _merge(o_gla, o_nsa, w_pg, w_pn, z2, w_out, tm=512, tn=1024):
    m = o_gla.shape[0]
    n_i = m // tm
    slab = D_MODEL // ((D_MODEL // tn) * n_i)
    assert slab % 16 == 0 and slab * (D_MODEL // tn) * n_i == D_MODEL
    once = pl.Buffered(1)
    return pl.pallas_call(
        _merge_kernel,
        grid=(D_MODEL // tn, m // tm),
        in_specs=[
            pl.BlockSpec((tm, GLA_DV), lambda j, i: (i, 0)),
            pl.BlockSpec((tm, NSA_HEADS * NSA_HD), lambda j, i: (i, 0)),
            pl.BlockSpec((GLA_DV, tn), lambda j, i: (0, j), pipeline_mode=once),
            pl.BlockSpec((NSA_HEADS * NSA_HD, tn), lambda j, i: (0, j), pipeline_mode=once),
            pl.BlockSpec((tm, tn), lambda j, i: (i, Z_MG // tn + j)),
            pl.BlockSpec((tm, tn), lambda j, i: (i, (Z_MG + D_MODEL) // tn + j)),
            pl.BlockSpec((slab, D_MODEL), lambda j, i: (j * n_i + i, 0)),
        ],
        out_specs=[pl.BlockSpec((tm, tn), lambda j, i: (i, j)),
                   pl.BlockSpec((slab, D_MODEL), lambda j, i: (j * n_i + i, 0))],
        out_shape=[jax.ShapeDtypeStruct((m, D_MODEL), BF16),
                   jax.ShapeDtypeStruct((D_MODEL, D_MODEL), BF16)],
        scratch_shapes=[pltpu.VMEM((GLA_DV, tn), BF16), pltpu.VMEM((NSA_HEADS * NSA_HD, tn), BF16)],
        compiler_params=_params(("parallel", "arbitrary")),
        name="merge_proj",
    )(o_gla, o_nsa, w_pg, w_pn, z2, z2, w_out)


def _out_proj_kernel(mg_ref, w_ref, x_ref, g_ref, x1_ref, h_ref):
    x1 = x_ref[...] + _dot(mg_ref[...], w_ref[...])
    x1_ref[...] = x1
    ms = jnp.mean(x1 * x1, axis=-1, keepdims=True)
    h_ref[...] = (x1 * lax.rsqrt(ms + RMS_EPS) * g_ref[...]).astype(h_ref.dtype)


def _out_proj(merged, w_out, x2, g_ffn, tm=512):
    m = merged.shape[0]
    return pl.pallas_call(
        _out_proj_kernel,
        grid=(m // tm,),
        in_specs=[
            pl.BlockSpec((tm, D_MODEL), lambda i: (i, 0)),
            pl.BlockSpec((D_MODEL, D_MODEL), lambda i: (0, 0)),
            pl.BlockSpec((tm, D_MODEL), lambda i: (i, 0)),
            pl.BlockSpec((1, D_MODEL), lambda i: (0, 0)),
        ],
        out_specs=[
            pl.BlockSpec((tm, D_MODEL), lambda i: (i, 0)),
            pl.BlockSpec((tm, D_MODEL), lambda i: (i, 0)),
        ],
        out_shape=[jax.ShapeDtypeStruct((m, D_MODEL), F32),
                   jax.ShapeDtypeStruct((m, D_MODEL), BF16)],
        compiler_params=_params(("parallel",)),
        name="out_proj",
    )(merged, w_out, x2, g_ffn)


def _ffn_up_kernel(h_ref, wg_ref, wu_ref, wd_ref, o_ref, wd_bf_ref, wg_scr, wu_scr):
    wd_bf_ref[...] = wd_ref[...].astype(BF16)

    @pl.when(pl.program_id(1) == 0)
    def _():
        wg_scr[...] = wg_ref[...].astype(BF16)
        wu_scr[...] = wu_ref[...].astype(BF16)

    h = h_ref[...]
    g = _dot(h, wg_scr[...])
    u = _dot(h, wu_scr[...])
    o_ref[...] = (g * _sigmoid(g) * u).astype(o_ref.dtype)


def _ffn_up(h2, w_gate, w_up, w_down, tm=1024, tf=512):
    m = h2.shape[0]
    n_i = m // tm
    slab = D_FF // ((D_FF // tf) * n_i)
    assert slab % 16 == 0 and slab * (D_FF // tf) * n_i == D_FF
    return pl.pallas_call(
        _ffn_up_kernel,
        grid=(D_FF // tf, m // tm),
        in_specs=[
            pl.BlockSpec((tm, D_MODEL), lambda j, i: (i, 0)),
            pl.BlockSpec((D_MODEL, tf), lambda j, i: (0, j)),
            pl.BlockSpec((D_MODEL, tf), lambda j, i: (0, j)),
            pl.BlockSpec((slab, D_MODEL), lambda j, i: (j * n_i + i, 0)),
        ],
        out_specs=[pl.BlockSpec((tm, tf), lambda j, i: (i, j)),
                   pl.BlockSpec((slab, D_MODEL), lambda j, i: (j * n_i + i, 0))],
        out_shape=[jax.ShapeDtypeStruct((m, D_FF), BF16),
                   jax.ShapeDtypeStruct((D_FF, D_MODEL), BF16)],
        scratch_shapes=[pltpu.VMEM((D_MODEL, tf), BF16), pltpu.VMEM((D_MODEL, tf), BF16)],
        compiler_params=_params(("parallel", "arbitrary")),
        name="ffn_up",
    )(h2, w_gate, w_up, w_down)


def _ffn_down_kernel(a_ref, w_ref, x1_ref, g_ref, o_ref, *, final_norm):
    k = pl.program_id(1)

    @pl.when(k == 0)
    def _():
        o_ref[...] = x1_ref[...]

    o_ref[...] += _dot(a_ref[...], w_ref[...])

    if final_norm:
        @pl.when(k == pl.num_programs(1) - 1)
        def _():
            y = o_ref[...]
            ms = jnp.mean(y * y, axis=-1, keepdims=True)
            o_ref[...] = y * lax.rsqrt(ms + RMS_EPS) * g_ref[...]


def _ffn_down(act, w_down, x1, g_final, final_norm, tm=1024, tk=1408):
    m = act.shape[0]
    return pl.pallas_call(
        functools.partial(_ffn_down_kernel, final_norm=final_norm),
        grid=(m // tm, D_FF // tk),
        in_specs=[
            pl.BlockSpec((tm, tk), lambda i, k: (i, k)),
            pl.BlockSpec((tk, D_MODEL), lambda i, k: (k, 0)),
            pl.BlockSpec((tm, D_MODEL), lambda i, k: (i, 0)),
            pl.BlockSpec((1, D_MODEL), lambda i, k: (0, 0)),
        ],
        out_specs=pl.BlockSpec((tm, D_MODEL), lambda i, k: (i, 0)),
        out_shape=jax.ShapeDtypeStruct((m, D_MODEL), F32),
        compiler_params=_params(("parallel", "arbitrary")),
        name="ffn_down",
    )(act, w_down, x1, g_final)


def _cmp_to_slc_weights_t(seq):
    n_cmp = seq // CMP_STRIDE - 1
    n_slc = seq // SLC_BLOCK
    c0 = np.arange(n_cmp)[:, None] * CMP_STRIDE
    s0 = np.arange(n_slc)[None, :] * SLC_BLOCK
    ov = np.clip(np.minimum(c0 + CMP_LEN, s0 + SLC_BLOCK) - np.maximum(c0, s0), 0, None)
    m_cs = np.zeros((n_cmp + 1, n_slc), np.float32)
    m_cs[:n_cmp] = ov / CMP_LEN
    return jnp.asarray(m_cs.T, BF16)


def kernel(x, g_mix, w_in, w_gla_gate, b_gla_gate, g_gla_out, pe_cmp_k, w_cmp_k1, w_cmp_k2,
           pe_cmp_v, w_cmp_v1, w_cmp_v2, w_proj_gla, w_proj_nsa, w_out, g_ffn,
           w_ffn_gate, w_ffn_up, w_ffn_down, g_final):
    bsz, seq, d = x.shape
    m = bsz * seq
    depth = w_in.shape[0]
    mcs_t = _cmp_to_slc_weights_t(seq)
    xc = x.reshape(m, d)
    for l in range(depth):
        w_t = jnp.swapaxes(w_in[l], 0, 1)
        h, zs = _norm_small(xc, g_mix[l][None], w_t)
        z2 = _in_proj(h, w_t)
        z3 = z2.reshape(bsz, seq, Z_SMALL)
        zs3 = zs.reshape(bsz, seq, LANES)

        w_gate_pad = jnp.zeros((LANES, GLA_DK), F32).at[SMALL_GLR:SMALL_GLR + GLA_GATE_RANK].set(w_gla_gate[l])
        o_gla = _gla(z3, zs3, w_gate_pad.astype(BF16), b_gla_gate[l][None], g_gla_out[l][None])

        cmp = _compress(z3, pe_cmp_k[l], pe_cmp_v[l], w_cmp_k1[l], w_cmp_v1[l], w_cmp_k2[l], w_cmp_v2[l])
        o_nsa = _nsa(z3, zs3, cmp, mcs_t)

        merged, w_out_bf = _merge(o_gla.reshape(m, GLA_DV), o_nsa.reshape(m, NSA_HEADS * NSA_HD),
                                  w_proj_gla[l], w_proj_nsa[l], z2, w_out[l])
        x1, h2 = _out_proj(merged, w_out_bf, xc, g_ffn[l][None])
        act, w_down_bf = _ffn_up(h2, w_ffn_gate[l], w_ffn_up[l], w_ffn_down[l])
        xc = _ffn_down(act, w_down_bf, x1, g_final[None],
                       final_norm=(l == depth - 1))
    return xc.reshape(bsz, seq, d)
```

```python
import functools

import numpy as np
import jax
import jax.numpy as jnp
from jax import lax
from jax.experimental import pallas as pl
from jax.experimental.pallas import tpu as pltpu

F32 = jnp.float32
BF16 = jnp.bfloat16

D_MODEL = 2048
GLA_HEADS = 4
GLA_DK = D_MODEL // 2
GLA_DV = D_MODEL
GLA_HK = GLA_DK // GLA_HEADS
GLA_HV = GLA_DV // GLA_HEADS
GLA_GATE_RANK = 16
GLA_TAU = 16.0
GLA_CHUNK = 64

NSA_HEADS = 16
NSA_KV_HEADS = 4
NSA_GROUP = NSA_HEADS // NSA_KV_HEADS
NSA_HD = D_MODEL // NSA_HEADS
CMP_STRIDE = 16
CMP_LEN = 2 * CMP_STRIDE
CMP_HIDDEN = 256
SLC_BLOCK = 64
SLC_TOPN = 16
WINDOW = 512
D_FF = ((8 * D_MODEL // 3 + 255) // 256) * 256

IN_SPLIT_SIZES = (GLA_DK, GLA_DK, GLA_DV, GLA_DV, GLA_GATE_RANK,
                  NSA_HEADS * NSA_HD, 6 * NSA_KV_HEADS * NSA_HD, 3 * NSA_HEADS,
                  2 * D_MODEL)

NEG_INF = -1e30
RMS_EPS = 1e-6

Z_GQ = 0
Z_GK = Z_GQ + GLA_DK
Z_GV = Z_GK + GLA_DK
Z_GR = Z_GV + GLA_DV
Z_NQ = Z_GR + GLA_DV
Z_NKV = Z_NQ + NSA_HEADS * NSA_HD
Z_MG = Z_NKV + 6 * NSA_KV_HEADS * NSA_HD
Z_SMALL = Z_MG + 2 * D_MODEL
LANES = 128
SMALL_GLR = 0
SMALL_NGATE = GLA_GATE_RANK

VMEM_LIMIT = 60 * 1024 * 1024


def _dot(a, b):
    return jnp.dot(a, b, preferred_element_type=F32)


def _dot_nt(a, b):
    return lax.dot_general(a, b, (((1,), (1,)), ((), ())), preferred_element_type=F32)


def _dot_tn(a, b):
    return lax.dot_general(a, b, (((0,), (0,)), ((), ())), preferred_element_type=F32)


def _sigmoid(x):
    return 1.0 / (1.0 + jnp.exp(-x))


def _split2(x):
    hi = x.astype(BF16)
    lo = (x - hi.astype(F32)).astype(BF16)
    return hi, lo


def _split3(x):
    hi = x.astype(BF16)
    r1 = x - hi.astype(F32)
    mid = r1.astype(BF16)
    lo = (r1 - mid.astype(F32)).astype(BF16)
    return hi, mid, lo


def _params(sem, **kw):
    return pltpu.CompilerParams(dimension_semantics=sem, vmem_limit_bytes=VMEM_LIMIT, **kw)


_IN_OFFS = np.cumsum((0,) + IN_SPLIT_SIZES)
COL_GLR = int(_IN_OFFS[4])
COL_NGATE = int(_IN_OFFS[7])
assert COL_GLR % LANES == SMALL_GLR and COL_NGATE % LANES == SMALL_NGATE
assert SMALL_GLR + GLA_GATE_RANK == SMALL_NGATE and SMALL_NGATE + 3 * NSA_HEADS <= LANES


def _norm_kernel(x_ref, g_ref, wa_ref, wb_ref, h_ref, zs_ref):
    xf = x_ref[...]
    ms = jnp.mean(xf * xf, axis=-1, keepdims=True)
    h = (xf * lax.rsqrt(ms + RMS_EPS) * g_ref[...]).astype(BF16)
    h_ref[...] = h
    row = lax.broadcasted_iota(jnp.int32, wa_ref.shape, 0)
    w_small = jnp.where(row < SMALL_NGATE, wa_ref[...],
                        jnp.where(row < SMALL_NGATE + 3 * NSA_HEADS, wb_ref[...], 0.0))
    zs_ref[...] = _dot_nt(h, w_small.astype(BF16))


def _norm_small(x2, g, w_t, tm=1024):
    m = x2.shape[0]
    return pl.pallas_call(
        _norm_kernel,
        grid=(m // tm,),
        in_specs=[
            pl.BlockSpec((tm, D_MODEL), lambda i: (i, 0)),
            pl.BlockSpec((1, D_MODEL), lambda i: (0, 0)),
            pl.BlockSpec((LANES, D_MODEL), lambda i: (COL_GLR // LANES, 0)),
            pl.BlockSpec((LANES, D_MODEL), lambda i: (COL_NGATE // LANES, 0)),
        ],
        out_specs=[pl.BlockSpec((tm, D_MODEL), lambda i: (i, 0)),
                   pl.BlockSpec((tm, LANES), lambda i: (i, 0))],
        out_shape=[jax.ShapeDtypeStruct((m, D_MODEL), BF16),
                   jax.ShapeDtypeStruct((m, LANES), F32)],
        compiler_params=_params(("parallel",)),
        name="norm_small",
    )(x2, g, w_t, w_t)


IN_TN = 1024
IN_SHIFT_B = GLA_GATE_RANK
IN_SHIFT_C = GLA_GATE_RANK + 3 * NSA_HEADS
IN_BLOCKS_A = Z_NQ // IN_TN
IN_BLOCKS_AB = Z_MG // IN_TN
IN_ROWS = 48
assert (IN_TN - IN_SHIFT_B) % IN_ROWS == 0 and (IN_TN - IN_SHIFT_C) % IN_ROWS == 0


def _in_proj_kernel(h_ref, wm_ref, we_ref, z_ref, w_scr):
    j = pl.program_id(0)

    def build(shift):
        if shift == 0:
            w_scr[...] = wm_ref[...].astype(BF16)
            return

        def body(r, carry):
            dst = pl.ds(pl.multiple_of(r * IN_ROWS, 16), IN_ROWS)
            src = pl.ds(pl.multiple_of(r * IN_ROWS + shift, 8), IN_ROWS)
            w_scr[dst, :] = wm_ref[src, :].astype(BF16)
            return carry
        lax.fori_loop(0, (IN_TN - shift) // IN_ROWS, body, 0)
        w_scr[IN_TN - shift:, :] = we_ref[:shift, :].astype(BF16)

    first = pl.program_id(1) == 0

    @pl.when(first & (j < IN_BLOCKS_A))
    def _():
        build(0)

    @pl.when(first & (j >= IN_BLOCKS_A) & (j < IN_BLOCKS_AB))
    def _():
        build(IN_SHIFT_B)

    @pl.when(first & (j >= IN_BLOCKS_AB))
    def _():
        build(IN_SHIFT_C)

    z_ref[...] = _dot_nt(h_ref[...], w_scr[...]).astype(z_ref.dtype)


def _in_proj(h, w_t, tm=2048):
    m = h.shape[0]
    tn = IN_TN
    return pl.pallas_call(
        _in_proj_kernel,
        grid=(Z_SMALL // tn, m // tm),
        in_specs=[
            pl.BlockSpec((tm, D_MODEL), lambda j, i: (i, 0)),
            pl.BlockSpec((tn, D_MODEL), lambda j, i: (j, 0)),
            pl.BlockSpec((LANES, D_MODEL), lambda j, i: ((j + 1) * (tn // LANES), 0)),
        ],
        out_specs=pl.BlockSpec((tm, tn), lambda j, i: (i, j)),
        out_shape=jax.ShapeDtypeStruct((m, Z_SMALL), BF16),
        scratch_shapes=[pltpu.VMEM((tn, D_MODEL), BF16)],
        compiler_params=_params(("parallel", "arbitrary")),
        name="in_proj",
    )(h, w_t, w_t)


GLA_HPS = 4


def _gla_kernel(q_ref, k_ref, v_ref, r_ref, sm_ref, wg_ref, bg_ref, gn_ref, o_ref,
                state_ref, *, n_chunks):
    @pl.when(pl.program_id(2) == 0)
    def _():
        state_ref[...] = jnp.zeros_like(state_ref)

    C = GLA_CHUNK
    HK, HV = GLA_HK, GLA_HV
    ts = n_chunks * C
    row = lax.broadcasted_iota(jnp.int32, (ts, ts), 0)
    col = lax.broadcasted_iota(jnp.int32, (ts, ts), 1)
    tri = jnp.where((row >= col) & ((row // C) == (col // C)), 1.0, 0.0).astype(BF16)
    crow = lax.broadcasted_iota(jnp.int32, (C, C), 0)
    ccol = lax.broadcasted_iota(jnp.int32, (C, C), 1)
    causal = crow >= ccol

    u = _dot(sm_ref[...].astype(BF16), wg_ref[...]) + bg_ref[...]
    log_a = (jnp.minimum(u, 0.0) - jnp.log(1.0 + jnp.exp(-jnp.abs(u)))) / GLA_TAU
    hi, lo = _split2(log_a)
    b_all = _dot(tri, hi) + _dot(tri, lo)
    gn = gn_ref[...]

    heads = range(GLA_HPS)
    kcol = [slice(j * HK, (j + 1) * HK) for j in heads]
    vcol = [slice(j * HV, (j + 1) * HV) for j in heads]
    b = [b_all[:, kcol[j]] for j in heads]
    k = [k_ref[:, kcol[j]].astype(F32) for j in heads]
    q_dec = [(q_ref[:, kcol[j]].astype(F32) * (HK ** -0.5) * jnp.exp(b[j])).astype(BF16) for j in heads]
    k_dec = [(k[j] * jnp.exp(-b[j])).astype(BF16) for j in heads]
    state = [state_ref[j] for j in heads]
    for c in range(n_chunks):
        rows = slice(c * C, (c + 1) * C)
        for j in heads:
            b_c = b[j][rows]
            b_last = b_c[C - 1:C, :]
            k_end = (k[j][rows] * jnp.exp(b_last - b_c)).astype(BF16)
            v_c = v_ref[rows, vcol[j]]
            attn = jnp.where(causal, _dot_nt(q_dec[j][rows], k_dec[j][rows]), 0.0).astype(BF16)
            o = _dot(attn, v_c) + _dot_nt(q_dec[j][rows], state[j].astype(BF16))
            state[j] = state[j] * jnp.exp(b_last) + _dot_tn(v_c, k_end)
            ms = jnp.mean(o * o, axis=-1, keepdims=True)
            y = o * lax.rsqrt(ms + RMS_EPS) * gn
            r = r_ref[rows, vcol[j]].astype(F32)
            o_ref[rows, vcol[j]] = (y * (r * _sigmoid(r))).astype(o_ref.dtype)
    for j in heads:
        state_ref[j] = state[j]


def _gla(z3, zs3, w_gate_pad, b_gate, g_out, ts=512):
    bsz, s, _ = z3.shape
    qb, vb = GLA_HPS * GLA_HK, GLA_HPS * GLA_HV
    return pl.pallas_call(
        functools.partial(_gla_kernel, n_chunks=ts // GLA_CHUNK),
        grid=(bsz, GLA_HEADS // GLA_HPS, s // ts),
        in_specs=[
            pl.BlockSpec((None, ts, qb), lambda b, h, i: (b, i, Z_GQ // qb + h)),
            pl.BlockSpec((None, ts, qb), lambda b, h, i: (b, i, Z_GK // qb + h)),
            pl.BlockSpec((None, ts, vb), lambda b, h, i: (b, i, Z_GV // vb + h)),
            pl.BlockSpec((None, ts, vb), lambda b, h, i: (b, i, Z_GR // vb + h)),
            pl.BlockSpec((None, ts, LANES), lambda b, h, i: (b, i, 0)),
            pl.BlockSpec((LANES, qb), lambda b, h, i: (0, h)),
            pl.BlockSpec((1, qb), lambda b, h, i: (0, h)),
            pl.BlockSpec((1, GLA_HV), lambda b, h, i: (0, 0)),
        ],
        out_specs=pl.BlockSpec((None, ts, vb), lambda b, h, i: (b, i, h)),
        out_shape=jax.ShapeDtypeStruct((bsz, s, GLA_DV), BF16),
        scratch_shapes=[pltpu.VMEM((GLA_HPS, GLA_HV, GLA_HK), F32)],
        compiler_params=_params(("parallel", "parallel", "arbitrary")),
        name="gla",
    )(z3, z3, z3, z3, zs3, w_gate_pad, b_gate, g_out)


def _cmp_kernel(t_ref, pek_ref, pev_ref, w1k_ref, w1v_ref, w2k_ref, w2v_ref, o_ref,
                tf_scr, lo_scr, hi_scr, pe_scr, w1_scr, w2_scr):
    hd = NSA_HD
    half = CMP_STRIDE * hd
    nchunk = lo_scr.shape[0] // NSA_KV_HEADS

    @pl.when(pl.program_id(0) == 0)
    def _():
        pe_scr[...] = pek_ref[...]
        w1_scr[...] = w1k_ref[...].astype(BF16)
        w2_scr[...] = w2k_ref[...].astype(BF16)

    @pl.when(pl.program_id(0) == 1)
    def _():
        pe_scr[...] = pev_ref[...]
        w1_scr[...] = w1v_ref[...].astype(BF16)
        w2_scr[...] = w2v_ref[...].astype(BF16)

    for h in range(NSA_KV_HEADS):
        tf_scr[h] = t_ref[:, h * hd:(h + 1) * hd].astype(F32)
        rows = slice(h * nchunk, (h + 1) * nchunk)
        for r in range(CMP_STRIDE):
            t_r = tf_scr.at[h][pl.ds(r, nchunk, stride=CMP_STRIDE), :]
            lo_scr[rows, r * hd:(r + 1) * hd] = (t_r + pe_scr[r:r + 1, :]).astype(BF16)
            hi_scr[rows, r * hd:(r + 1) * hd] = (t_r + pe_scr[CMP_STRIDE + r:CMP_STRIDE + r + 1, :]).astype(BF16)
    a = _dot(lo_scr[...], w1_scr[:half, :])
    bm = _dot(hi_scr[...], w1_scr[half:, :])
    pre = a + pltpu.roll(bm, bm.shape[0] - 1, 0)
    hid = (pre * _sigmoid(pre)).astype(BF16)
    out = _dot(hid, w2_scr[...]).astype(o_ref.dtype)
    for h in range(NSA_KV_HEADS):
        o_ref[h] = out[h * nchunk:(h + 1) * nchunk]


def _compress(z3, pe_k, pe_v, w1_k, w1_v, w2_k, w2_v):
    bsz, s, _ = z3.shape
    hkv, hd = NSA_KV_HEADS, NSA_HD
    nchunk = s // CMP_STRIDE
    width = CMP_STRIDE * hd
    return pl.pallas_call(
        _cmp_kernel,
        grid=(2, bsz),
        in_specs=[
            pl.BlockSpec((None, s, hkv * hd), lambda kv, b: (b, 0, Z_NKV // (hkv * hd) + kv)),
            pl.BlockSpec((CMP_LEN, hd), lambda kv, b: (0, 0)),
            pl.BlockSpec((CMP_LEN, hd), lambda kv, b: (0, 0)),
            pl.BlockSpec((2 * width, CMP_HIDDEN), lambda kv, b: (0, 0)),
            pl.BlockSpec((2 * width, CMP_HIDDEN), lambda kv, b: (0, 0)),
            pl.BlockSpec((CMP_HIDDEN, hd), lambda kv, b: (0, 0)),
            pl.BlockSpec((CMP_HIDDEN, hd), lambda kv, b: (0, 0)),
        ],
        out_specs=pl.BlockSpec((None, None, hkv, nchunk, hd), lambda kv, b: (kv, b, 0, 0, 0)),
        out_shape=jax.ShapeDtypeStruct((2, bsz, hkv, nchunk, hd), BF16),
        scratch_shapes=[pltpu.VMEM((hkv, s, hd), F32),
                        pltpu.VMEM((hkv * nchunk, width), BF16),
                        pltpu.VMEM((hkv * nchunk, width), BF16),
                        pltpu.VMEM((CMP_LEN, hd), F32),
                        pltpu.VMEM((2 * width, CMP_HIDDEN), BF16),
                        pltpu.VMEM((CMP_HIDDEN, hd), BF16)],
        compiler_params=_params(("parallel", "parallel")),
        name="nsa_compress",
    )(z3, pe_k, pe_v, w1_k, w1_v, w2_k, w2_v)


NSA_TQ = 256
NSA_TK = NSA_TQ
LOG2E = 1.4426950408889634


NSA_HPS = 4


def _nsa_kernel(q_ref, sm_ref, kc_ref, vc_ref, ks_ref, vs_ref, kw_ref, vw_ref, mcs_ref,
                o_ref, sel_scr, gate_scr, p_scr, acc_scr):
    G, TQ, TK, HD = NSA_GROUP, NSA_TQ, NSA_TK, NSA_HD
    R = G * TQ
    heads = range(NSA_HPS)
    hk0 = pl.program_id(1) * NSA_HPS
    qi = pl.program_id(2)
    qs = qi * TQ

    def lanes4(x):
        return jnp.concatenate([x] * G, axis=1)

    def hcol(j):
        return slice(j * HD, (j + 1) * HD)

    q4 = []
    for j in heads:
        qj = jnp.concatenate([q_ref[:, (j * G + g) * HD:(j * G + g + 1) * HD] for g in range(G)], axis=0)
        q4.append((qj.astype(F32) * (HD ** -0.5 * LOG2E)).astype(BF16))

    nc = kc_ref.shape[1]
    tq_c = qs + lax.broadcasted_iota(jnp.int32, (nc, TQ), 1)
    cmp_end = lax.broadcasted_iota(jnp.int32, (nc, TQ), 0) * CMP_STRIDE + (CMP_LEN - 1)
    valid_c = lanes4(jnp.where(cmp_end <= tq_c, 1.0, 0.0)) > 0.5
    mcs = mcs_ref[...]
    nb = mcs.shape[0]
    blk = lax.broadcasted_iota(jnp.int32, (nb, TQ), 0)
    tq_b = qs + lax.broadcasted_iota(jnp.int32, (nb, TQ), 1)
    cur = tq_b // SLC_BLOCK
    forced = (blk == 0) | (blk == cur) | (blk == cur - 1)
    visible = blk * SLC_BLOCK <= tq_b
    later = [jnp.where(blk > kb, 1.0, 0.0) for kb in range(nb)]
    o_c = []
    for j in heads:
        s_c = jnp.where(valid_c, _dot_nt(kc_ref[j], q4[j]), NEG_INF)
        e_c = jnp.where(valid_c, jnp.exp2(s_c - jnp.max(s_c, axis=0, keepdims=True)), 0.0)
        l_c = jnp.sum(e_c, axis=0, keepdims=True)
        p_c = e_c * (1.0 / jnp.where(l_c > 0.0, l_c, 1.0))
        o_c.append(_dot_tn(vc_ref[j], p_c.astype(BF16)))

        p_sum = p_c[:, 0:TQ]
        for g in range(1, G):
            p_sum = p_sum + p_c[:, g * TQ:(g + 1) * TQ]
        imp = sum(_dot(mcs, part) for part in _split3(p_sum))
        imp = jnp.where(forced, jnp.inf, imp)
        imp = jnp.where(visible, imp, -jnp.inf)
        rank = jnp.zeros((nb, TQ), F32)
        for kb in range(nb):
            rk = imp[kb:kb + 1, :]
            ahead = jnp.where(rk > imp, 1.0, jnp.where(rk == imp, later[kb], 0.0))
            rank = rank + ahead
        sel_scr[j] = jnp.where(rank < float(min(SLC_TOPN, nb)), 0.0, NEG_INF)

    span = WINDOW + TQ
    w0 = pl.multiple_of(jnp.maximum(qs - WINDOW, 0), TQ)
    kpos = w0 + lax.broadcasted_iota(jnp.int32, (span, TQ), 0)
    tq_w = qs + lax.broadcasted_iota(jnp.int32, (span, TQ), 1)
    bias_w = lanes4(jnp.where(kpos <= tq_w, jnp.where(kpos > tq_w - WINDOW, 0.0, NEG_INF), NEG_INF))
    o_w = []
    for j in heads:
        s_w = _dot_nt(kw_ref[pl.ds(w0, span), hcol(j)], q4[j]) + bias_w
        e_w = jnp.exp2(s_w - jnp.max(s_w, axis=0, keepdims=True)).astype(BF16)
        v_aug = jnp.concatenate([vw_ref[pl.ds(w0, span), hcol(j)], jnp.ones((span, 8), BF16)], axis=1)
        pv = _dot_tn(v_aug, e_w)
        o_w.append(pv)

    blocks_per_tile = TK // SLC_BLOCK

    def sel_bias(j, tile):
        return jnp.concatenate(
            [jnp.broadcast_to(sel_scr[j, pl.ds(tile * blocks_per_tile + i, 1), :], (SLC_BLOCK, TQ))
             for i in range(blocks_per_tile)], axis=0)

    krow = lax.broadcasted_iota(jnp.int32, (TK, TQ), 0)
    qcol = lax.broadcasted_iota(jnp.int32, (TK, TQ), 1)
    diag = pl.ds(pl.multiple_of(qs, TK), TK)
    m_d = []
    for j in heads:
        bias_d = jnp.where(krow <= qcol, sel_bias(j, qi), NEG_INF)
        s_d = _dot_nt(ks_ref[diag, hcol(j)], q4[j]) + lanes4(bias_d)
        m = jnp.max(s_d, axis=0, keepdims=True)
        m_d.append(m)
        p_scr[j] = jnp.exp2(s_d - m).astype(BF16)

    ones_col = jnp.ones((TK, 8), BF16)

    def pv_aug(rows, j):
        return _dot_tn(jnp.concatenate([vs_ref[rows, hcol(j)], ones_col], axis=1), p_scr[j])

    def kv_step(t, carry):
        ms, voff = carry
        prev = pl.ds(pl.multiple_of(voff, TK), TK)
        koff = pl.multiple_of(t * TK, TK)
        ms_n = []
        for j in heads:
            pv = pv_aug(prev, j)
            s = _dot_nt(ks_ref[pl.ds(koff, TK), hcol(j)], q4[j]) + lanes4(sel_bias(j, t))
            m_new = jnp.maximum(ms[j], jnp.max(s, axis=0, keepdims=True))
            alpha = jnp.exp2(ms[j] - m_new)
            ms_n.append(m_new)
            acc_scr[j] = alpha * (acc_scr[j] + pv)
            p_scr[j] = jnp.exp2(s - m_new).astype(BF16)
        return tuple(ms_n), koff

    acc_scr[...] = jnp.zeros_like(acc_scr)
    _, voff = lax.fori_loop(0, qi, kv_step, (tuple(m_d), qs))
    last = pl.ds(pl.multiple_of(voff, TK), TK)
    o_s = []
    for j in heads:
        o_s.append(acc_scr[j] + pv_aug(last, j))

    gate_scr[...] = _sigmoid(sm_ref[...]).T
    for j in heads:
        def gate_row(branch):
            base = SMALL_NGATE + branch * NSA_HEADS + (hk0 + j) * G
            return jnp.concatenate([gate_scr[pl.ds(base + g, 1), :] for g in range(G)], axis=1)

        g_s = gate_row(1) * (1.0 / o_s[j][HD:HD + 1])
        g_w = gate_row(2) * (1.0 / o_w[j][HD:HD + 1])
        o = gate_row(0) * o_c[j] + g_s * o_s[j][:HD] + g_w * o_w[j][:HD]
        for g in range(G):
            o_ref[:, (j * G + g) * HD:(j * G + g + 1) * HD] = o[:, g * TQ:(g + 1) * TQ].T.astype(o_ref.dtype)


def _nsa(z3, zs3, cmp, mcs_t):
    bsz, s, _ = z3.shape
    hkv, hd, tq, hps = NSA_KV_HEADS, NSA_HD, NSA_TQ, NSA_HPS
    qw = hps * NSA_GROUP * hd
    kw = hps * hd
    nc = cmp.shape[3]
    nb = mcs_t.shape[0]

    def kv_spec(i):
        return pl.BlockSpec((None, s, kw), lambda b, h, q: (b, 0, (Z_NKV + i * hkv * hd) // kw + h))

    return pl.pallas_call(
        _nsa_kernel,
        grid=(bsz, hkv // hps, s // tq),
        in_specs=[
            pl.BlockSpec((None, tq, qw), lambda b, h, q: (b, q, Z_NQ // qw + h)),
            pl.BlockSpec((None, tq, LANES), lambda b, h, q: (b, q, 0)),
            pl.BlockSpec((None, None, hps, nc, hd), lambda b, h, q: (0, b, h, 0, 0)),
            pl.BlockSpec((None, None, hps, nc, hd), lambda b, h, q: (1, b, h, 0, 0)),
            kv_spec(2), kv_spec(3), kv_spec(4), kv_spec(5),
            pl.BlockSpec(mcs_t.shape, lambda b, h, q: (0, 0)),
        ],
        out_specs=pl.BlockSpec((None, tq, qw), lambda b, h, q: (b, q, h)),
        out_shape=jax.ShapeDtypeStruct((bsz, s, NSA_HEADS * hd), BF16),
        scratch_shapes=[pltpu.VMEM((hps, nb, tq), F32), pltpu.VMEM((LANES, tq), F32),
                        pltpu.VMEM((hps, NSA_TK, NSA_GROUP * tq), BF16),
                        pltpu.VMEM((hps, hd + 8, NSA_GROUP * tq), F32)],
        compiler_params=_params(("parallel", "parallel", "arbitrary")),
        name="nsa_attn",
    )(z3, zs3, cmp, cmp, z3, z3, z3, z3, mcs_t)


def _merge_kernel(og_ref, on_ref, wg_ref, wn_ref, ga_ref, gb_ref, wo_ref, o_ref, wo_bf_ref, wg_scr, wn_scr):
    wo_bf_ref[...] = wo_ref[...].astype(BF16)

    @pl.when(pl.program_id(1) == 0)
    def _():
        wg_scr[...] = wg_ref[...].astype(BF16)
        wn_scr[...] = wn_ref[...].astype(BF16)

    a = _dot(og_ref[...], wg_scr[...])
    b = _dot(on_ref[...], wn_scr[...])
    ga = _sigmoid(ga_ref[...].astype(F32))
    gb = _sigmoid(gb_ref[...].astype(F32))
    o_ref[...] = (ga * a + gb * b).astype(o_ref.dtype)


def _merge(o_gla, o_nsa, w_pg, w_pn, z2, w_out, tm=512, tn=1024):
    m = o_gla.shape[0]
    n_i = m // tm
    slab = D_MODEL // ((D_MODEL // tn) * n_i)
    assert slab % 16 == 0 and slab * (D_MODEL // tn) * n_i == D_MODEL
    once = pl.Buffered(1)
    return pl.pallas_call(
        _merge_kernel,
        grid=(D_MODEL // tn, m // tm),
        in_specs=[
            pl.BlockSpec((tm, GLA_DV), lambda j, i: (i, 0)),
            pl.BlockSpec((tm, NSA_HEADS * NSA_HD), lambda j, i: (i, 0)),
            pl.BlockSpec((GLA_DV, tn), lambda j, i: (0, j), pipeline_mode=once),
            pl.BlockSpec((NSA_HEADS * NSA_HD, tn), lambda j, i: (0, j), pipeline_mode=once),
            pl.BlockSpec((tm, tn), lambda j, i: (i, Z_MG // tn + j)),
            pl.BlockSpec((tm, tn), lambda j, i: (i, (Z_MG + D_MODEL) // tn + j)),
            pl.BlockSpec((slab, D_MODEL), lambda j, i: (j * n_i + i, 0)),
        ],
        out_specs=[pl.BlockSpec((tm, tn), lambda j, i: (i, j)),
                   pl.BlockSpec((slab, D_MODEL), lambda j, i: (j * n_i + i, 0))],
        out_shape=[jax.ShapeDtypeStruct((m, D_MODEL), BF16),
                   jax.ShapeDtypeStruct((D_MODEL, D_MODEL), BF16)],
        scratch_shapes=[pltpu.VMEM((GLA_DV, tn), BF16), pltpu.VMEM((NSA_HEADS * NSA_HD, tn), BF16)],
        compiler_params=_params(("parallel", "arbitrary")),
        name="merge_proj",
    )(o_gla, o_nsa, w_pg, w_pn, z2, z2, w_out)


def _out_proj_kernel(mg_ref, w_ref, x_ref, g_ref, x1_ref, h_ref):
    x1 = x_ref[...] + _dot(mg_ref[...], w_ref[...])
    x1_ref[...] = x1
    ms = jnp.mean(x1 * x1, axis=-1, keepdims=True)
    h_ref[...] = (x1 * lax.rsqrt(ms + RMS_EPS) * g_ref[...]).astype(h_ref.dtype)


def _out_proj(merged, w_out, x2, g_ffn, tm=512):
    m = merged.shape[0]
    return pl.pallas_call(
        _out_proj_kernel,
        grid=(m // tm,),
        in_specs=[
            pl.BlockSpec((tm, D_MODEL), lambda i: (i, 0)),
            pl.BlockSpec((D_MODEL, D_MODEL), lambda i: (0, 0)),
            pl.BlockSpec((tm, D_MODEL), lambda i: (i, 0)),
            pl.BlockSpec((1, D_MODEL), lambda i: (0, 0)),
        ],
        out_specs=[
            pl.BlockSpec((tm, D_MODEL), lambda i: (i, 0)),
            pl.BlockSpec((tm, D_MODEL), lambda i: (i, 0)),
        ],
        out_shape=[jax.ShapeDtypeStruct((m, D_MODEL), F32),
                   jax.ShapeDtypeStruct((m, D_MODEL), BF16)],
        compiler_params=_params(("parallel",)),
        name="out_proj",
    )(merged, w_out, x2, g_ffn)


def _ffn_up_kernel(h_ref, wg_ref, wu_ref, wd_ref, o_ref, wd_bf_ref, wg_scr, wu_scr):
    wd_bf_ref[...] = wd_ref[...].astype(BF16)

    @pl.when(pl.program_id(1) == 0)
    def _():
        wg_scr[...] = wg_ref[...].astype(BF16)
        wu_scr[...] = wu_ref[...].astype(BF16)

    h = h_ref[...]
    g = _dot(h, wg_scr[...])
    u = _dot(h, wu_scr[...])
    o_ref[...] = (g * _sigmoid(g) * u).astype(o_ref.dtype)


def _ffn_up(h2, w_gate, w_up, w_down, tm=2048, tf=512):
    m = h2.shape[0]
    n_i = m // tm
    slab = D_FF // ((D_FF // tf) * n_i)
    assert slab % 16 == 0 and slab * (D_FF // tf) * n_i == D_FF
    return pl.pallas_call(
        _ffn_up_kernel,
        grid=(D_FF // tf, m // tm),
        in_specs=[
            pl.BlockSpec((tm, D_MODEL), lambda j, i: (i, 0)),
            pl.BlockSpec((D_MODEL, tf), lambda j, i: (0, j)),
            pl.BlockSpec((D_MODEL, tf), lambda j, i: (0, j)),
            pl.BlockSpec((slab, D_MODEL), lambda j, i: (j * n_i + i, 0)),
        ],
        out_specs=[pl.BlockSpec((tm, tf), lambda j, i: (i, j)),
                   pl.BlockSpec((slab, D_MODEL), lambda j, i: (j * n_i + i, 0))],
        out_shape=[jax.ShapeDtypeStruct((m, D_FF), BF16),
                   jax.ShapeDtypeStruct((D_FF, D_MODEL), BF16)],
        scratch_shapes=[pltpu.VMEM((D_MODEL, tf), BF16), pltpu.VMEM((D_MODEL, tf), BF16)],
        compiler_params=_params(("parallel", "arbitrary")),
        name="ffn_up",
    )(h2, w_gate, w_up, w_down)


def _ffn_down_kernel(a_ref, w_ref, x1_ref, g_ref, o_ref, *, final_norm):
    k = pl.program_id(1)

    @pl.when(k == 0)
    def _():
        o_ref[...] = x1_ref[...]

    o_ref[...] += _dot(a_ref[...], w_ref[...])

    if final_norm:
        @pl.when(k == pl.num_programs(1) - 1)
        def _():
            y = o_ref[...]
            ms = jnp.mean(y * y, axis=-1, keepdims=True)
            o_ref[...] = y * lax.rsqrt(ms + RMS_EPS) * g_ref[...]


def _ffn_down(act, w_down, x1, g_final, final_norm, tm=1024, tk=1408):
    m = act.shape[0]
    return pl.pallas_call(
        functools.partial(_ffn_down_kernel, final_norm=final_norm),
        grid=(m // tm, D_FF // tk),
        in_specs=[
            pl.BlockSpec((tm, tk), lambda i, k: (i, k)),
            pl.BlockSpec((tk, D_MODEL), lambda i, k: (k, 0)),
            pl.BlockSpec((tm, D_MODEL), lambda i, k: (i, 0)),
            pl.BlockSpec((1, D_MODEL), lambda i, k: (0, 0)),
        ],
        out_specs=pl.BlockSpec((tm, D_MODEL), lambda i, k: (i, 0)),
        out_shape=jax.ShapeDtypeStruct((m, D_MODEL), F32),
        compiler_params=_params(("parallel", "arbitrary")),
        name="ffn_down",
    )(act, w_down, x1, g_final)


def _cmp_to_slc_weights_t(seq):
    n_cmp = seq // CMP_STRIDE - 1
    n_slc = seq // SLC_BLOCK
    c0 = np.arange(n_cmp)[:, None] * CMP_STRIDE
    s0 = np.arange(n_slc)[None, :] * SLC_BLOCK
    ov = np.clip(np.minimum(c0 + CMP_LEN, s0 + SLC_BLOCK) - np.maximum(c0, s0), 0, None)
    m_cs = np.zeros((n_cmp + 1, n_slc), np.float32)
    m_cs[:n_cmp] = ov / CMP_LEN
    return jnp.asarray(m_cs.T, BF16)


def kernel(x, g_mix, w_in, w_gla_gate, b_gla_gate, g_gla_out, pe_cmp_k, w_cmp_k1, w_cmp_k2,
           pe_cmp_v, w_cmp_v1, w_cmp_v2, w_proj_gla, w_proj_nsa, w_out, g_ffn,
           w_ffn_gate, w_ffn_up, w_ffn_down, g_final):
    bsz, seq, d = x.shape
    m = bsz * seq
    depth = w_in.shape[0]
    mcs_t = _cmp_to_slc_weights_t(seq)
    xc = x.reshape(m, d)
    for l in range(depth):
        w_t = jnp.swapaxes(w_in[l], 0, 1)
        h, zs = _norm_small(xc, g_mix[l][None], w_t)
        z2 = _in_proj(h, w_t)
        z3 = z2.reshape(bsz, seq, Z_SMALL)
        zs3 = zs.reshape(bsz, seq, LANES)

        w_gate_pad = jnp.zeros((LANES, GLA_DK), F32).at[SMALL_GLR:SMALL_GLR + GLA_GATE_RANK].set(w_gla_gate[l])
        o_gla = _gla(z3, zs3, w_gate_pad.astype(BF16), b_gla_gate[l][None], g_gla_out[l][None])

        cmp = _compress(z3, pe_cmp_k[l], pe_cmp_v[l], w_cmp_k1[l], w_cmp_v1[l], w_cmp_k2[l], w_cmp_v2[l])
        o_nsa = _nsa(z3, zs3, cmp, mcs_t)

        merged, w_out_bf = _merge(o_gla.reshape(m, GLA_DV), o_nsa.reshape(m, NSA_HEADS * NSA_HD),
                                  w_proj_gla[l], w_proj_nsa[l], z2, w_out[l])
        x1, h2 = _out_proj(merged, w_out_bf, xc, g_ffn[l][None])
        act, w_down_bf = _ffn_up(h2, w_ffn_gate[l], w_ffn_up[l], w_ffn_down[l])
        xc = _ffn_down(act, w_down_bf, x1, g_final[None],
                       final_norm=(l == depth - 1))
    return xc.reshape(bsz, seq, d)
```

```python
import functools

import numpy as np
import jax
import jax.numpy as jnp
from jax import lax
from jax.experimental import pallas as pl
from jax.experimental.pallas import tpu as pltpu

F32 = jnp.float32
BF16 = jnp.bfloat16

D_MODEL = 2048
GLA_HEADS = 4
GLA_DK = D_MODEL // 2
GLA_DV = D_MODEL
GLA_HK = GLA_DK // GLA_HEADS
GLA_HV = GLA_DV // GLA_HEADS
GLA_GATE_RANK = 16
GLA_TAU = 16.0
GLA_CHUNK = 64

NSA_HEADS = 16
NSA_KV_HEADS = 4
NSA_GROUP = NSA_HEADS // NSA_KV_HEADS
NSA_HD = D_MODEL // NSA_HEADS
CMP_STRIDE = 16
CMP_LEN = 2 * CMP_STRIDE
CMP_HIDDEN = 256
SLC_BLOCK = 64
SLC_TOPN = 16
WINDOW = 512
D_FF = ((8 * D_MODEL // 3 + 255) // 256) * 256

IN_SPLIT_SIZES = (GLA_DK, GLA_DK, GLA_DV, GLA_DV, GLA_GATE_RANK,
                  NSA_HEADS * NSA_HD, 6 * NSA_KV_HEADS * NSA_HD, 3 * NSA_HEADS,
                  2 * D_MODEL)

NEG_INF = -1e30
RMS_EPS = 1e-6

Z_GQ = 0
Z_GK = Z_GQ + GLA_DK
Z_GV = Z_GK + GLA_DK
Z_GR = Z_GV + GLA_DV
Z_NQ = Z_GR + GLA_DV
Z_NKV = Z_NQ + NSA_HEADS * NSA_HD
Z_MG = Z_NKV + 6 * NSA_KV_HEADS * NSA_HD
Z_SMALL = Z_MG + 2 * D_MODEL
LANES = 128
SMALL_GLR = 0
SMALL_NGATE = GLA_GATE_RANK

VMEM_LIMIT = 60 * 1024 * 1024


def _dot(a, b):
    return jnp.dot(a, b, preferred_element_type=F32)


def _dot_nt(a, b):
    return lax.dot_general(a, b, (((1,), (1,)), ((), ())), preferred_element_type=F32)


def _dot_tn(a, b):
    return lax.dot_general(a, b, (((0,), (0,)), ((), ())), preferred_element_type=F32)


def _sigmoid(x):
    return 1.0 / (1.0 + jnp.exp(-x))


def _split2(x):
    hi = x.astype(BF16)
    lo = (x - hi.astype(F32)).astype(BF16)
    return hi, lo


def _split3(x):
    hi = x.astype(BF16)
    r1 = x - hi.astype(F32)
    mid = r1.astype(BF16)
    lo = (r1 - mid.astype(F32)).astype(BF16)
    return hi, mid, lo


def _params(sem, **kw):
    return pltpu.CompilerParams(dimension_semantics=sem, vmem_limit_bytes=VMEM_LIMIT, **kw)


_IN_OFFS = np.cumsum((0,) + IN_SPLIT_SIZES)
COL_GLR = int(_IN_OFFS[4])
COL_NGATE = int(_IN_OFFS[7])
assert COL_GLR % LANES == SMALL_GLR and COL_NGATE % LANES == SMALL_NGATE
assert SMALL_GLR + GLA_GATE_RANK == SMALL_NGATE and SMALL_NGATE + 3 * NSA_HEADS <= LANES


def _norm_kernel(x_ref, g_ref, wa_ref, wb_ref, h_ref, zs_ref):
    xf = x_ref[...]
    ms = jnp.mean(xf * xf, axis=-1, keepdims=True)
    h = (xf * lax.rsqrt(ms + RMS_EPS) * g_ref[...]).astype(BF16)
    h_ref[...] = h
    row = lax.broadcasted_iota(jnp.int32, wa_ref.shape, 0)
    w_small = jnp.where(row < SMALL_NGATE, wa_ref[...],
                        jnp.where(row < SMALL_NGATE + 3 * NSA_HEADS, wb_ref[...], 0.0))
    zs_ref[...] = _dot_nt(h, w_small.astype(BF16))


def _norm_small(x2, g, w_t, tm=1024):
    m = x2.shape[0]
    return pl.pallas_call(
        _norm_kernel,
        grid=(m // tm,),
        in_specs=[
            pl.BlockSpec((tm, D_MODEL), lambda i: (i, 0)),
            pl.BlockSpec((1, D_MODEL), lambda i: (0, 0)),
            pl.BlockSpec((LANES, D_MODEL), lambda i: (COL_GLR // LANES, 0)),
            pl.BlockSpec((LANES, D_MODEL), lambda i: (COL_NGATE // LANES, 0)),
        ],
        out_specs=[pl.BlockSpec((tm, D_MODEL), lambda i: (i, 0)),
                   pl.BlockSpec((tm, LANES), lambda i: (i, 0))],
        out_shape=[jax.ShapeDtypeStruct((m, D_MODEL), BF16),
                   jax.ShapeDtypeStruct((m, LANES), F32)],
        compiler_params=_params(("parallel",)),
        name="norm_small",
    )(x2, g, w_t, w_t)


IN_TN = 1024
IN_SHIFT_B = GLA_GATE_RANK
IN_SHIFT_C = GLA_GATE_RANK + 3 * NSA_HEADS
IN_BLOCKS_A = Z_NQ // IN_TN
IN_BLOCKS_AB = Z_MG // IN_TN
IN_ROWS = 48
assert (IN_TN - IN_SHIFT_B) % IN_ROWS == 0 and (IN_TN - IN_SHIFT_C) % IN_ROWS == 0


def _in_proj_kernel(h_ref, wm_ref, we_ref, z_ref, w_scr):
    j = pl.program_id(0)

    def build(shift):
        if shift == 0:
            w_scr[...] = wm_ref[...].astype(BF16)
            return

        def body(r, carry):
            dst = pl.ds(pl.multiple_of(r * IN_ROWS, 16), IN_ROWS)
            src = pl.ds(pl.multiple_of(r * IN_ROWS + shift, 8), IN_ROWS)
            w_scr[dst, :] = wm_ref[src, :].astype(BF16)
            return carry
        lax.fori_loop(0, (IN_TN - shift) // IN_ROWS, body, 0)
        w_scr[IN_TN - shift:, :] = we_ref[:shift, :].astype(BF16)

    first = pl.program_id(1) == 0

    @pl.when(first & (j < IN_BLOCKS_A))
    def _():
        build(0)

    @pl.when(first & (j >= IN_BLOCKS_A) & (j < IN_BLOCKS_AB))
    def _():
        build(IN_SHIFT_B)

    @pl.when(first & (j >= IN_BLOCKS_AB))
    def _():
        build(IN_SHIFT_C)

    z_ref[...] = _dot_nt(h_ref[...], w_scr[...]).astype(z_ref.dtype)


def _in_proj(h, w_t, tm=2048):
    m = h.shape[0]
    tn = IN_TN
    return pl.pallas_call(
        _in_proj_kernel,
        grid=(Z_SMALL // tn, m // tm),
        in_specs=[
            pl.BlockSpec((tm, D_MODEL), lambda j, i: (i, 0)),
            pl.BlockSpec((tn, D_MODEL), lambda j, i: (j, 0)),
            pl.BlockSpec((LANES, D_MODEL), lambda j, i: ((j + 1) * (tn // LANES), 0)),
        ],
        out_specs=pl.BlockSpec((tm, tn), lambda j, i: (i, j)),
        out_shape=jax.ShapeDtypeStruct((m, Z_SMALL), BF16),
        scratch_shapes=[pltpu.VMEM((tn, D_MODEL), BF16)],
        compiler_params=_params(("parallel", "arbitrary")),
        name="in_proj",
    )(h, w_t, w_t)


GLA_HPS = 4


def _gla_kernel(q_ref, k_ref, v_ref, r_ref, sm_ref, wg_ref, bg_ref, gn_ref, o_ref,
                state_ref, *, n_chunks):
    @pl.when(pl.program_id(2) == 0)
    def _():
        state_ref[...] = jnp.zeros_like(state_ref)

    C = GLA_CHUNK
    HK, HV = GLA_HK, GLA_HV
    ts = n_chunks * C
    row = lax.broadcasted_iota(jnp.int32, (ts, ts), 0)
    col = lax.broadcasted_iota(jnp.int32, (ts, ts), 1)
    tri = jnp.where((row >= col) & ((row // C) == (col // C)), 1.0, 0.0).astype(BF16)
    crow = lax.broadcasted_iota(jnp.int32, (C, C), 0)
    ccol = lax.broadcasted_iota(jnp.int32, (C, C), 1)
    causal = crow >= ccol

    u = _dot(sm_ref[...].astype(BF16), wg_ref[...]) + bg_ref[...]
    log_a = (jnp.minimum(u, 0.0) - jnp.log(1.0 + jnp.exp(-jnp.abs(u)))) / GLA_TAU
    hi, lo = _split2(log_a)
    b_all = _dot(tri, hi) + _dot(tri, lo)
    gn = gn_ref[...]

    heads = range(GLA_HPS)
    kcol = [slice(j * HK, (j + 1) * HK) for j in heads]
    vcol = [slice(j * HV, (j + 1) * HV) for j in heads]
    b = [b_all[:, kcol[j]] for j in heads]
    k = [k_ref[:, kcol[j]].astype(F32) for j in heads]
    q_dec = [(q_ref[:, kcol[j]].astype(F32) * (HK ** -0.5) * jnp.exp(b[j])).astype(BF16) for j in heads]
    k_dec = [(k[j] * jnp.exp(-b[j])).astype(BF16) for j in heads]
    state = [state_ref[j] for j in heads]
    for c in range(n_chunks):
        rows = slice(c * C, (c + 1) * C)
        for j in heads:
            b_c = b[j][rows]
            b_last = b_c[C - 1:C, :]
            k_end = (k[j][rows] * jnp.exp(b_last - b_c)).astype(BF16)
            v_c = v_ref[rows, vcol[j]]
            attn = jnp.where(causal, _dot_nt(q_dec[j][rows], k_dec[j][rows]), 0.0).astype(BF16)
            o = _dot(attn, v_c) + _dot_nt(q_dec[j][rows], state[j].astype(BF16))
            state[j] = state[j] * jnp.exp(b_last) + _dot_tn(v_c, k_end)
            ms = jnp.mean(o * o, axis=-1, keepdims=True)
            y = o * lax.rsqrt(ms + RMS_EPS) * gn
            r = r_ref[rows, vcol[j]].astype(F32)
            o_ref[rows, vcol[j]] = (y * (r * _sigmoid(r))).astype(o_ref.dtype)
    for j in heads:
        state_ref[j] = state[j]


def _gla(z3, zs3, w_gate_pad, b_gate, g_out, ts=512):
    bsz, s, _ = z3.shape
    qb, vb = GLA_HPS * GLA_HK, GLA_HPS * GLA_HV
    return pl.pallas_call(
        functools.partial(_gla_kernel, n_chunks=ts // GLA_CHUNK),
        grid=(bsz, GLA_HEADS // GLA_HPS, s // ts),
        in_specs=[
            pl.BlockSpec((None, ts, qb), lambda b, h, i: (b, i, Z_GQ // qb + h)),
            pl.BlockSpec((None, ts, qb), lambda b, h, i: (b, i, Z_GK // qb + h)),
            pl.BlockSpec((None, ts, vb), lambda b, h, i: (b, i, Z_GV // vb + h)),
            pl.BlockSpec((None, ts, vb), lambda b, h, i: (b, i, Z_GR // vb + h)),
            pl.BlockSpec((None, ts, LANES), lambda b, h, i: (b, i, 0)),
            pl.BlockSpec((LANES, qb), lambda b, h, i: (0, h)),
            pl.BlockSpec((1, qb), lambda b, h, i: (0, h)),
            pl.BlockSpec((1, GLA_HV), lambda b, h, i: (0, 0)),
        ],
        out_specs=pl.BlockSpec((None, ts, vb), lambda b, h, i: (b, i, h)),
        out_shape=jax.ShapeDtypeStruct((bsz, s, GLA_DV), BF16),
        scratch_shapes=[pltpu.VMEM((GLA_HPS, GLA_HV, GLA_HK), F32)],
        compiler_params=_params(("parallel", "parallel", "arbitrary")),
        name="gla",
    )(z3, z3, z3, z3, zs3, w_gate_pad, b_gate, g_out)


def _cmp_kernel(t_ref, pek_ref, pev_ref, w1k_ref, w1v_ref, w2k_ref, w2v_ref, o_ref,
                tf_scr, lo_scr, hi_scr, pe_scr, w1_scr, w2_scr):
    hd = NSA_HD
    half = CMP_STRIDE * hd
    nchunk = lo_scr.shape[0] // NSA_KV_HEADS

    @pl.when(pl.program_id(0) == 0)
    def _():
        pe_scr[...] = pek_ref[...]
        w1_scr[...] = w1k_ref[...].astype(BF16)
        w2_scr[...] = w2k_ref[...].astype(BF16)

    @pl.when(pl.program_id(0) == 1)
    def _():
        pe_scr[...] = pev_ref[...]
        w1_scr[...] = w1v_ref[...].astype(BF16)
        w2_scr[...] = w2v_ref[...].astype(BF16)

    for h in range(NSA_KV_HEADS):
        tf_scr[h] = t_ref[:, h * hd:(h + 1) * hd].astype(F32)
        rows = slice(h * nchunk, (h + 1) * nchunk)
        for r in range(CMP_STRIDE):
            t_r = tf_scr.at[h][pl.ds(r, nchunk, stride=CMP_STRIDE), :]
            lo_scr[rows, r * hd:(r + 1) * hd] = (t_r + pe_scr[r:r + 1, :]).astype(BF16)
            hi_scr[rows, r * hd:(r + 1) * hd] = (t_r + pe_scr[CMP_STRIDE + r:CMP_STRIDE + r + 1, :]).astype(BF16)
    a = _dot(lo_scr[...], w1_scr[:half, :])
    bm = _dot(hi_scr[...], w1_scr[half:, :])
    pre = a + pltpu.roll(bm, bm.shape[0] - 1, 0)
    hid = (pre * _sigmoid(pre)).astype(BF16)
    out = _dot(hid, w2_scr[...]).astype(o_ref.dtype)
    for h in range(NSA_KV_HEADS):
        o_ref[h] = out[h * nchunk:(h + 1) * nchunk]


def _compress(z3, pe_k, pe_v, w1_k, w1_v, w2_k, w2_v):
    bsz, s, _ = z3.shape
    hkv, hd = NSA_KV_HEADS, NSA_HD
    nchunk = s // CMP_STRIDE
    width = CMP_STRIDE * hd
    return pl.pallas_call(
        _cmp_kernel,
        grid=(2, bsz),
        in_specs=[
            pl.BlockSpec((None, s, hkv * hd), lambda kv, b: (b, 0, Z_NKV // (hkv * hd) + kv)),
            pl.BlockSpec((CMP_LEN, hd), lambda kv, b: (0, 0)),
            pl.BlockSpec((CMP_LEN, hd), lambda kv, b: (0, 0)),
            pl.BlockSpec((2 * width, CMP_HIDDEN), lambda kv, b: (0, 0)),
            pl.BlockSpec((2 * width, CMP_HIDDEN), lambda kv, b: (0, 0)),
            pl.BlockSpec((CMP_HIDDEN, hd), lambda kv, b: (0, 0)),
            pl.BlockSpec((CMP_HIDDEN, hd), lambda kv, b: (0, 0)),
        ],
        out_specs=pl.BlockSpec((None, None, hkv, nchunk, hd), lambda kv, b: (kv, b, 0, 0, 0)),
        out_shape=jax.ShapeDtypeStruct((2, bsz, hkv, nchunk, hd), BF16),
        scratch_shapes=[pltpu.VMEM((hkv, s, hd), F32),
                        pltpu.VMEM((hkv * nchunk, width), BF16),
                        pltpu.VMEM((hkv * nchunk, width), BF16),
                        pltpu.VMEM((CMP_LEN, hd), F32),
                        pltpu.VMEM((2 * width, CMP_HIDDEN), BF16),
                        pltpu.VMEM((CMP_HIDDEN, hd), BF16)],
        compiler_params=_params(("parallel", "parallel")),
        name="nsa_compress",
    )(z3, pe_k, pe_v, w1_k, w1_v, w2_k, w2_v)


NSA_TQ = 256
NSA_TK = NSA_TQ
LOG2E = 1.4426950408889634


NSA_HPS = 4


def _nsa_kernel(q_ref, sm_ref, kc_ref, vc_ref, ks_ref, vs_ref, kw_ref, vw_ref, mcs_ref,
                o_ref, sel_scr, gate_scr, p_scr, acc_scr):
    G, TQ, TK, HD = NSA_GROUP, NSA_TQ, NSA_TK, NSA_HD
    R = G * TQ
    heads = range(NSA_HPS)
    hk0 = pl.program_id(1) * NSA_HPS
    qi = pl.program_id(2)
    qs = qi * TQ

    def lanes4(x):
        return jnp.concatenate([x] * G, axis=1)

    def hcol(j):
        return slice(j * HD, (j + 1) * HD)

    q4 = []
    for j in heads:
        qj = jnp.concatenate([q_ref[:, (j * G + g) * HD:(j * G + g + 1) * HD] for g in range(G)], axis=0)
        q4.append((qj.astype(F32) * (HD ** -0.5 * LOG2E)).astype(BF16))

    nc = kc_ref.shape[1]
    tq_c = qs + lax.broadcasted_iota(jnp.int32, (nc, TQ), 1)
    cmp_end = lax.broadcasted_iota(jnp.int32, (nc, TQ), 0) * CMP_STRIDE + (CMP_LEN - 1)
    valid_c = lanes4(jnp.where(cmp_end <= tq_c, 1.0, 0.0)) > 0.5
    mcs = mcs_ref[...]
    nb = mcs.shape[0]
    blk = lax.broadcasted_iota(jnp.int32, (nb, TQ), 0)
    tq_b = qs + lax.broadcasted_iota(jnp.int32, (nb, TQ), 1)
    cur = tq_b // SLC_BLOCK
    forced = (blk == 0) | (blk == cur) | (blk == cur - 1)
    visible = blk * SLC_BLOCK <= tq_b
    later = [jnp.where(blk > kb, 1.0, 0.0) for kb in range(nb)]
    o_c = []
    for j in heads:
        s_c = jnp.where(valid_c, _dot_nt(kc_ref[j], q4[j]), NEG_INF)
        e_c = jnp.where(valid_c, jnp.exp2(s_c - jnp.max(s_c, axis=0, keepdims=True)), 0.0)
        l_c = jnp.sum(e_c, axis=0, keepdims=True)
        p_c = e_c * (1.0 / jnp.where(l_c > 0.0, l_c, 1.0))
        o_c.append(_dot_tn(vc_ref[j], p_c.astype(BF16)))

        p_sum = p_c[:, 0:TQ]
        for g in range(1, G):
            p_sum = p_sum + p_c[:, g * TQ:(g + 1) * TQ]
        imp = sum(_dot(mcs, part) for part in _split3(p_sum))
        imp = jnp.where(forced, jnp.inf, imp)
        imp = jnp.where(visible, imp, -jnp.inf)
        rank = jnp.zeros((nb, TQ), F32)
        for kb in range(nb):
            rk = imp[kb:kb + 1, :]
            ahead = jnp.where(rk > imp, 1.0, jnp.where(rk == imp, later[kb], 0.0))
            rank = rank + ahead
        sel_scr[j] = jnp.where(rank < float(min(SLC_TOPN, nb)), 0.0, NEG_INF)

    span = WINDOW + TQ
    w0 = pl.multiple_of(jnp.maximum(qs - WINDOW, 0), TQ)
    kpos = w0 + lax.broadcasted_iota(jnp.int32, (span, TQ), 0)
    tq_w = qs + lax.broadcasted_iota(jnp.int32, (span, TQ), 1)
    bias_w = lanes4(jnp.where(kpos <= tq_w, jnp.where(kpos > tq_w - WINDOW, 0.0, NEG_INF), NEG_INF))
    o_w = []
    for j in heads:
        s_w = _dot_nt(kw_ref[pl.ds(w0, span), hcol(j)], q4[j]) + bias_w
        e_w = jnp.exp2(s_w - jnp.max(s_w, axis=0, keepdims=True)).astype(BF16)
        v_aug = jnp.concatenate([vw_ref[pl.ds(w0, span), hcol(j)], jnp.ones((span, 8), BF16)], axis=1)
        pv = _dot_tn(v_aug, e_w)
        o_w.append(pv)

    blocks_per_tile = TK // SLC_BLOCK

    def sel_bias(j, tile):
        return jnp.concatenate(
            [jnp.broadcast_to(sel_scr[j, pl.ds(tile * blocks_per_tile + i, 1), :], (SLC_BLOCK, TQ))
             for i in range(blocks_per_tile)], axis=0)

    krow = lax.broadcasted_iota(jnp.int32, (TK, TQ), 0)
    qcol = lax.broadcasted_iota(jnp.int32, (TK, TQ), 1)
    diag = pl.ds(pl.multiple_of(qs, TK), TK)
    m_d = []
    for j in heads:
        bias_d = jnp.where(krow <= qcol, sel_bias(j, qi), NEG_INF)
        s_d = _dot_nt(ks_ref[diag, hcol(j)], q4[j]) + lanes4(bias_d)
        m = jnp.max(s_d, axis=0, keepdims=True)
        m_d.append(m)
        p_scr[j] = jnp.exp2(s_d - m).astype(BF16)

    ones_col = jnp.ones((TK, 8), BF16)

    def pv_aug(rows, j):
        return _dot_tn(jnp.concatenate([vs_ref[rows, hcol(j)], ones_col], axis=1), p_scr[j])

    def kv_step(t, carry):
        ms, voff = carry
        prev = pl.ds(pl.multiple_of(voff, TK), TK)
        koff = pl.multiple_of(t * TK, TK)
        ms_n = []
        for j in heads:
            pv = pv_aug(prev, j)
            s = _dot_nt(ks_ref[pl.ds(koff, TK), hcol(j)], q4[j]) + lanes4(sel_bias(j, t))
            m_new = jnp.maximum(ms[j], jnp.max(s, axis=0, keepdims=True))
            alpha = jnp.exp2(ms[j] - m_new)
            ms_n.append(m_new)
            acc_scr[j] = alpha * (acc_scr[j] + pv)
            p_scr[j] = jnp.exp2(s - m_new).astype(BF16)
        return tuple(ms_n), koff

    acc_scr[...] = jnp.zeros_like(acc_scr)
    _, voff = lax.fori_loop(0, qi, kv_step, (tuple(m_d), qs))
    last = pl.ds(pl.multiple_of(voff, TK), TK)
    o_s = []
    for j in heads:
        o_s.append(acc_scr[j] + pv_aug(last, j))

    gate_scr[...] = _sigmoid(sm_ref[...]).T
    for j in heads:
        def gate_row(branch):
            base = SMALL_NGATE + branch * NSA_HEADS + (hk0 + j) * G
            return jnp.concatenate([gate_scr[pl.ds(base + g, 1), :] for g in range(G)], axis=1)

        g_s = gate_row(1) * (1.0 / o_s[j][HD:HD + 1])
        g_w = gate_row(2) * (1.0 / o_w[j][HD:HD + 1])
        o = gate_row(0) * o_c[j] + g_s * o_s[j][:HD] + g_w * o_w[j][:HD]
        for g in range(G):
            o_ref[:, (j * G + g) * HD:(j * G + g + 1) * HD] = o[:, g * TQ:(g + 1) * TQ].T.astype(o_ref.dtype)


def _nsa(z3, zs3, cmp, mcs_t):
    bsz, s, _ = z3.shape
    hkv, hd, tq, hps = NSA_KV_HEADS, NSA_HD, NSA_TQ, NSA_HPS
    qw = hps * NSA_GROUP * hd
    kw = hps * hd
    nc = cmp.shape[3]
    nb = mcs_t.shape[0]

    def kv_spec(i):
        return pl.BlockSpec((None, s, kw), lambda b, h, q: (b, 0, (Z_NKV + i * hkv * hd) // kw + h))

    return pl.pallas_call(
        _nsa_kernel,
        grid=(bsz, hkv // hps, s // tq),
        in_specs=[
            pl.BlockSpec((None, tq, qw), lambda b, h, q: (b, q, Z_NQ // qw + h)),
            pl.BlockSpec((None, tq, LANES), lambda b, h, q: (b, q, 0)),
            pl.BlockSpec((None, None, hps, nc, hd), lambda b, h, q: (0, b, h, 0, 0)),
            pl.BlockSpec((None, None, hps, nc, hd), lambda b, h, q: (1, b, h, 0, 0)),
            kv_spec(2), kv_spec(3), kv_spec(4), kv_spec(5),
            pl.BlockSpec(mcs_t.shape, lambda b, h, q: (0, 0)),
        ],
        out_specs=pl.BlockSpec((None, tq, qw), lambda b, h, q: (b, q, h)),
        out_shape=jax.ShapeDtypeStruct((bsz, s, NSA_HEADS * hd), BF16),
        scratch_shapes=[pltpu.VMEM((hps, nb, tq), F32), pltpu.VMEM((LANES, tq), F32),
                        pltpu.VMEM((hps, NSA_TK, NSA_GROUP * tq), BF16),
                        pltpu.VMEM((hps, hd + 8, NSA_GROUP * tq), F32)],
        compiler_params=_params(("parallel", "parallel", "arbitrary")),
        name="nsa_attn",
    )(z3, zs3, cmp, cmp, z3, z3, z3, z3, mcs_t)


def _merge_kernel(og_ref, on_ref, wg_ref, wn_ref, ga_ref, gb_ref, wo_ref, o_ref, wo_bf_ref, wg_scr, wn_scr):
    wo_bf_ref[...] = wo_ref[...].astype(BF16)

    @pl.when(pl.program_id(1) == 0)
    def _():
        wg_scr[...] = wg_ref[...].astype(BF16)
        wn_scr[...] = wn_ref[...].astype(BF16)

    a = _dot(og_ref[...], wg_scr[...])
    b = _dot(on_ref[...], wn_scr[...])
    ga = _sigmoid(ga_ref[...].astype(F32))
    gb = _sigmoid(gb_ref[...].astype(F32))
    o_ref[...] = (ga * a + gb * b).astype(o_ref.dtype)


def _merge(o_gla, o_nsa, w_pg, w_pn, z2, w_out, tm=512, tn=1024):
    m = o_gla.shape[0]
    n_i = m // tm
    slab = D_MODEL // ((D_MODEL // tn) * n_i)
    assert slab % 16 == 0 and slab * (D_MODEL // tn) * n_i == D_MODEL
    return pl.pallas_call(
        _merge_kernel,
        grid=(D_MODEL // tn, m // tm),
        in_specs=[
            pl.BlockSpec((tm, GLA_DV), lambda j, i: (i, 0)),
            pl.BlockSpec((tm, NSA_HEADS * NSA_HD), lambda j, i: (i, 0)),
            pl.BlockSpec((GLA_DV, tn), lambda j, i: (0, j)),
            pl.BlockSpec((NSA_HEADS * NSA_HD, tn), lambda j, i: (0, j)),
            pl.BlockSpec((tm, tn), lambda j, i: (i, Z_MG // tn + j)),
            pl.BlockSpec((tm, tn), lambda j, i: (i, (Z_MG + D_MODEL) // tn + j)),
            pl.BlockSpec((slab, D_MODEL), lambda j, i: (j * n_i + i, 0)),
        ],
        out_specs=[pl.BlockSpec((tm, tn), lambda j, i: (i, j)),
                   pl.BlockSpec((slab, D_MODEL), lambda j, i: (j * n_i + i, 0))],
        out_shape=[jax.ShapeDtypeStruct((m, D_MODEL), BF16),
                   jax.ShapeDtypeStruct((D_MODEL, D_MODEL), BF16)],
        scratch_shapes=[pltpu.VMEM((GLA_DV, tn), BF16), pltpu.VMEM((NSA_HEADS * NSA_HD, tn), BF16)],
        compiler_params=_params(("parallel", "arbitrary")),
        name="merge_proj",
    )(o_gla, o_nsa, w_pg, w_pn, z2, z2, w_out)


def _out_proj_kernel(mg_ref, w_ref, x_ref, g_ref, x1_ref, h_ref):
    x1 = x_ref[...] + _dot(mg_ref[...], w_ref[...])
    x1_ref[...] = x1
    ms = jnp.mean(x1 * x1, axis=-1, keepdims=True)
    h_ref[...] = (x1 * lax.rsqrt(ms + RMS_EPS) * g_ref[...]).astype(h_ref.dtype)


def _out_proj(merged, w_out, x2, g_ffn, tm=512):
    m = merged.shape[0]
    return pl.pallas_call(
        _out_proj_kernel,
        grid=(m // tm,),
        in_specs=[
            pl.BlockSpec((tm, D_MODEL), lambda i: (i, 0)),
            pl.BlockSpec((D_MODEL, D_MODEL), lambda i: (0, 0)),
            pl.BlockSpec((tm, D_MODEL), lambda i: (i, 0)),
            pl.BlockSpec((1, D_MODEL), lambda i: (0, 0)),
        ],
        out_specs=[
            pl.BlockSpec((tm, D_MODEL), lambda i: (i, 0)),
            pl.BlockSpec((tm, D_MODEL), lambda i: (i, 0)),
        ],
        out_shape=[jax.ShapeDtypeStruct((m, D_MODEL), F32),
                   jax.ShapeDtypeStruct((m, D_MODEL), BF16)],
        compiler_params=_params(("parallel",)),
        name="out_proj",
    )(merged, w_out, x2, g_ffn)


def _ffn_up_kernel(h_ref, wg_ref, wu_ref, wd_ref, o_ref, wd_bf_ref, wg_scr, wu_scr):
    wd_bf_ref[...] = wd_ref[...].astype(BF16)

    @pl.when(pl.program_id(1) == 0)
    def _():
        wg_scr[...] = wg_ref[...].astype(BF16)
        wu_scr[...] = wu_ref[...].astype(BF16)

    h = h_ref[...]
    g = _dot(h, wg_scr[...])
    u = _dot(h, wu_scr[...])
    o_ref[...] = (g * _sigmoid(g) * u).astype(o_ref.dtype)


def _ffn_up(h2, w_gate, w_up, w_down, tm=2048, tf=512):
    m = h2.shape[0]
    n_i = m // tm
    slab = D_FF // ((D_FF // tf) * n_i)
    assert slab % 16 == 0 and slab * (D_FF // tf) * n_i == D_FF
    return pl.pallas_call(
        _ffn_up_kernel,
        grid=(D_FF // tf, m // tm),
        in_specs=[
            pl.BlockSpec((tm, D_MODEL), lambda j, i: (i, 0)),
            pl.BlockSpec((D_MODEL, tf), lambda j, i: (0, j)),
            pl.BlockSpec((D_MODEL, tf), lambda j, i: (0, j)),
            pl.BlockSpec((slab, D_MODEL), lambda j, i: (j * n_i + i, 0)),
        ],
        out_specs=[pl.BlockSpec((tm, tf), lambda j, i: (i, j)),
                   pl.BlockSpec((slab, D_MODEL), lambda j, i: (j * n_i + i, 0))],
        out_shape=[jax.ShapeDtypeStruct((m, D_FF), BF16),
                   jax.ShapeDtypeStruct((D_FF, D_MODEL), BF16)],
        scratch_shapes=[pltpu.VMEM((D_MODEL, tf), BF16), pltpu.VMEM((D_MODEL, tf), BF16)],
        compiler_params=_params(("parallel", "arbitrary")),
        name="ffn_up",
    )(h2, w_gate, w_up, w_down)


def _ffn_down_kernel(a_ref, w_ref, x1_ref, g_ref, o_ref, *, final_norm):
    k = pl.program_id(1)

    @pl.when(k == 0)
    def _():
        o_ref[...] = x1_ref[...]

    o_ref[...] += _dot(a_ref[...], w_ref[...])

    if final_norm:
        @pl.when(k == pl.num_programs(1) - 1)
        def _():
            y = o_ref[...]
            ms = jnp.mean(y * y, axis=-1, keepdims=True)
            o_ref[...] = y * lax.rsqrt(ms + RMS_EPS) * g_ref[...]


def _ffn_down(act, w_down, x1, g_final, final_norm, tm=1024, tk=1408):
    m = act.shape[0]
    return pl.pallas_call(
        functools.partial(_ffn_down_kernel, final_norm=final_norm),
        grid=(m // tm, D_FF // tk),
        in_specs=[
            pl.BlockSpec((tm, tk), lambda i, k: (i, k)),
            pl.BlockSpec((tk, D_MODEL), lambda i, k: (k, 0)),
            pl.BlockSpec((tm, D_MODEL), lambda i, k: (i, 0)),
            pl.BlockSpec((1, D_MODEL), lambda i, k: (0, 0)),
        ],
        out_specs=pl.BlockSpec((tm, D_MODEL), lambda i, k: (i, 0)),
        out_shape=jax.ShapeDtypeStruct((m, D_MODEL), F32),
        compiler_params=_params(("parallel", "arbitrary")),
        name="ffn_down",
    )(act, w_down, x1, g_final)


def _cmp_to_slc_weights_t(seq):
    n_cmp = seq // CMP_STRIDE - 1
    n_slc = seq // SLC_BLOCK
    c0 = np.arange(n_cmp)[:, None] * CMP_STRIDE
    s0 = np.arange(n_slc)[None, :] * SLC_BLOCK
    ov = np.clip(np.minimum(c0 + CMP_LEN, s0 + SLC_BLOCK) - np.maximum(c0, s0), 0, None)
    m_cs = np.zeros((n_cmp + 1, n_slc), np.float32)
    m_cs[:n_cmp] = ov / CMP_LEN
    return jnp.asarray(m_cs.T, BF16)


def kernel(x, g_mix, w_in, w_gla_gate, b_gla_gate, g_gla_out, pe_cmp_k, w_cmp_k1, w_cmp_k2,
           pe_cmp_v, w_cmp_v1, w_cmp_v2, w_proj_gla, w_proj_nsa, w_out, g_ffn,
           w_ffn_gate, w_ffn_up, w_ffn_down, g_final):
    bsz, seq, d = x.shape
    m = bsz * seq
    depth = w_in.shape[0]
    mcs_t = _cmp_to_slc_weights_t(seq)
    xc = x.reshape(m, d)
    for l in range(depth):
        w_t = jnp.swapaxes(w_in[l], 0, 1)
        h, zs = _norm_small(xc, g_mix[l][None], w_t)
        z2 = _in_proj(h, w_t)
        z3 = z2.reshape(bsz, seq, Z_SMALL)
        zs3 = zs.reshape(bsz, seq, LANES)

        w_gate_pad = jnp.zeros((LANES, GLA_DK), F32).at[SMALL_GLR:SMALL_GLR + GLA_GATE_RANK].set(w_gla_gate[l])
        o_gla = _gla(z3, zs3, w_gate_pad.astype(BF16), b_gla_gate[l][None], g_gla_out[l][None])

        cmp = _compress(z3, pe_cmp_k[l], pe_cmp_v[l], w_cmp_k1[l], w_cmp_v1[l], w_cmp_k2[l], w_cmp_v2[l])
        o_nsa = _nsa(z3, zs3, cmp, mcs_t)

        merged, w_out_bf = _merge(o_gla.reshape(m, GLA_DV), o_nsa.reshape(m, NSA_HEADS * NSA_HD),
                                  w_proj_gla[l], w_proj_nsa[l], z2, w_out[l])
        x1, h2 = _out_proj(merged, w_out_bf, xc, g_ffn[l][None])
        act, w_down_bf = _ffn_up(h2, w_ffn_gate[l], w_ffn_up[l], w_ffn_down[l])
        xc = _ffn_down(act, w_down_bf, x1, g_final[None],
                       final_norm=(l == depth - 1))
    return xc.reshape(bsz, seq, d)
```

```python
import functools

import numpy as np
import jax
import jax.numpy as jnp
from jax import lax
from jax.experimental import pallas as pl
from jax.experimental.pallas import tpu as pltpu

F32 = jnp.float32
BF16 = jnp.bfloat16

D_MODEL = 2048
GLA_HEADS = 4
GLA_DK = D_MODEL // 2
GLA_DV = D_MODEL
GLA_HK = GLA_DK // GLA_HEADS
GLA_HV = GLA_DV // GLA_HEADS
GLA_GATE_RANK = 16
GLA_TAU = 16.0
GLA_CHUNK = 64

NSA_HEADS = 16
NSA_KV_HEADS = 4
NSA_GROUP = NSA_HEADS // NSA_KV_HEADS
NSA_HD = D_MODEL // NSA_HEADS
CMP_STRIDE = 16
CMP_LEN = 2 * CMP_STRIDE
CMP_HIDDEN = 256
SLC_BLOCK = 64
SLC_TOPN = 16
WINDOW = 512
D_FF = ((8 * D_MODEL // 3 + 255) // 256) * 256

IN_SPLIT_SIZES = (GLA_DK, GLA_DK, GLA_DV, GLA_DV, GLA_GATE_RANK,
                  NSA_HEADS * NSA_HD, 6 * NSA_KV_HEADS * NSA_HD, 3 * NSA_HEADS,
                  2 * D_MODEL)

NEG_INF = -1e30
RMS_EPS = 1e-6

Z_GQ = 0
Z_GK = Z_GQ + GLA_DK
Z_GV = Z_GK + GLA_DK
Z_GR = Z_GV + GLA_DV
Z_NQ = Z_GR + GLA_DV
Z_NKV = Z_NQ + NSA_HEADS * NSA_HD
Z_MG = Z_NKV + 6 * NSA_KV_HEADS * NSA_HD
Z_SMALL = Z_MG + 2 * D_MODEL
LANES = 128
SMALL_GLR = 0
SMALL_NGATE = GLA_GATE_RANK

VMEM_LIMIT = 60 * 1024 * 1024


def _dot(a, b):
    return jnp.dot(a, b, preferred_element_type=F32)


def _dot_nt(a, b):
    return lax.dot_general(a, b, (((1,), (1,)), ((), ())), preferred_element_type=F32)


def _dot_tn(a, b):
    return lax.dot_general(a, b, (((0,), (0,)), ((), ())), preferred_element_type=F32)


def _sigmoid(x):
    return 1.0 / (1.0 + jnp.exp(-x))


def _split2(x):
    hi = x.astype(BF16)
    lo = (x - hi.astype(F32)).astype(BF16)
    return hi, lo


def _split3(x):
    hi = x.astype(BF16)
    r1 = x - hi.astype(F32)
    mid = r1.astype(BF16)
    lo = (r1 - mid.astype(F32)).astype(BF16)
    return hi, mid, lo


def _params(sem, **kw):
    return pltpu.CompilerParams(dimension_semantics=sem, vmem_limit_bytes=VMEM_LIMIT, **kw)


_IN_OFFS = np.cumsum((0,) + IN_SPLIT_SIZES)
COL_GLR = int(_IN_OFFS[4])
COL_NGATE = int(_IN_OFFS[7])
assert COL_GLR % LANES == SMALL_GLR and COL_NGATE % LANES == SMALL_NGATE
assert SMALL_GLR + GLA_GATE_RANK == SMALL_NGATE and SMALL_NGATE + 3 * NSA_HEADS <= LANES


def _norm_kernel(x_ref, g_ref, wa_ref, wb_ref, h_ref, zs_ref):
    xf = x_ref[...]
    ms = jnp.mean(xf * xf, axis=-1, keepdims=True)
    h = (xf * lax.rsqrt(ms + RMS_EPS) * g_ref[...]).astype(BF16)
    h_ref[...] = h
    row = lax.broadcasted_iota(jnp.int32, wa_ref.shape, 0)
    w_small = jnp.where(row < SMALL_NGATE, wa_ref[...],
                        jnp.where(row < SMALL_NGATE + 3 * NSA_HEADS, wb_ref[...], 0.0))
    zs_ref[...] = _dot_nt(h, w_small.astype(BF16))


def _norm_small(x2, g, w_t, tm=1024):
    m = x2.shape[0]
    return pl.pallas_call(
        _norm_kernel,
        grid=(m // tm,),
        in_specs=[
            pl.BlockSpec((tm, D_MODEL), lambda i: (i, 0)),
            pl.BlockSpec((1, D_MODEL), lambda i: (0, 0)),
            pl.BlockSpec((LANES, D_MODEL), lambda i: (COL_GLR // LANES, 0)),
            pl.BlockSpec((LANES, D_MODEL), lambda i: (COL_NGATE // LANES, 0)),
        ],
        out_specs=[pl.BlockSpec((tm, D_MODEL), lambda i: (i, 0)),
                   pl.BlockSpec((tm, LANES), lambda i: (i, 0))],
        out_shape=[jax.ShapeDtypeStruct((m, D_MODEL), BF16),
                   jax.ShapeDtypeStruct((m, LANES), F32)],
        compiler_params=_params(("parallel",)),
        name="norm_small",
    )(x2, g, w_t, w_t)


IN_TN = 1024
IN_SHIFT_B = GLA_GATE_RANK
IN_SHIFT_C = GLA_GATE_RANK + 3 * NSA_HEADS
IN_BLOCKS_A = Z_NQ // IN_TN
IN_BLOCKS_AB = Z_MG // IN_TN
IN_ROWS = 48
assert (IN_TN - IN_SHIFT_B) % IN_ROWS == 0 and (IN_TN - IN_SHIFT_C) % IN_ROWS == 0


def _in_proj_kernel(h_ref, wm_ref, we_ref, z_ref, w_scr):
    j = pl.program_id(0)

    def build(shift):
        if shift == 0:
            w_scr[...] = wm_ref[...].astype(BF16)
            return

        def body(r, carry):
            dst = pl.ds(pl.multiple_of(r * IN_ROWS, 16), IN_ROWS)
            src = pl.ds(pl.multiple_of(r * IN_ROWS + shift, 8), IN_ROWS)
            w_scr[dst, :] = wm_ref[src, :].astype(BF16)
            return carry
        lax.fori_loop(0, (IN_TN - shift) // IN_ROWS, body, 0)
        w_scr[IN_TN - shift:, :] = we_ref[:shift, :].astype(BF16)

    first = pl.program_id(1) == 0

    @pl.when(first & (j < IN_BLOCKS_A))
    def _():
        build(0)

    @pl.when(first & (j >= IN_BLOCKS_A) & (j < IN_BLOCKS_AB))
    def _():
        build(IN_SHIFT_B)

    @pl.when(first & (j >= IN_BLOCKS_AB))
    def _():
        build(IN_SHIFT_C)

    z_ref[...] = _dot_nt(h_ref[...], w_scr[...]).astype(z_ref.dtype)


def _in_proj(h, w_t, tm=2048):
    m = h.shape[0]
    tn = IN_TN
    return pl.pallas_call(
        _in_proj_kernel,
        grid=(Z_SMALL // tn, m // tm),
        in_specs=[
            pl.BlockSpec((tm, D_MODEL), lambda j, i: (i, 0)),
            pl.BlockSpec((tn, D_MODEL), lambda j, i: (j, 0)),
            pl.BlockSpec((LANES, D_MODEL), lambda j, i: ((j + 1) * (tn // LANES), 0)),
        ],
        out_specs=pl.BlockSpec((tm, tn), lambda j, i: (i, j)),
        out_shape=jax.ShapeDtypeStruct((m, Z_SMALL), BF16),
        scratch_shapes=[pltpu.VMEM((tn, D_MODEL), BF16)],
        compiler_params=_params(("parallel", "arbitrary")),
        name="in_proj",
    )(h, w_t, w_t)


GLA_HPS = 4


def _gla_kernel(q_ref, k_ref, v_ref, r_ref, sm_ref, wg_ref, bg_ref, gn_ref, tri_ref, o_ref,
                state_ref, *, n_chunks):
    @pl.when(pl.program_id(2) == 0)
    def _():
        state_ref[...] = jnp.zeros_like(state_ref)

    C = GLA_CHUNK
    HK, HV = GLA_HK, GLA_HV
    ts = n_chunks * C
    tri = tri_ref[...]
    crow = lax.broadcasted_iota(jnp.int32, (C, C), 0)
    ccol = lax.broadcasted_iota(jnp.int32, (C, C), 1)
    causal = crow >= ccol

    u = _dot(sm_ref[...].astype(BF16), wg_ref[...]) + bg_ref[...]
    log_a = (jnp.minimum(u, 0.0) - jnp.log(1.0 + jnp.exp(-jnp.abs(u)))) / GLA_TAU
    hi, lo = _split2(log_a)
    b_all = _dot(tri, hi) + _dot(tri, lo)
    gn = gn_ref[...]

    heads = range(GLA_HPS)
    kcol = [slice(j * HK, (j + 1) * HK) for j in heads]
    vcol = [slice(j * HV, (j + 1) * HV) for j in heads]
    b = [b_all[:, kcol[j]] for j in heads]
    k = [k_ref[:, kcol[j]].astype(F32) for j in heads]
    q_dec = [(q_ref[:, kcol[j]].astype(F32) * (HK ** -0.5) * jnp.exp(b[j])).astype(BF16) for j in heads]
    k_dec = [(k[j] * jnp.exp(-b[j])).astype(BF16) for j in heads]
    state = [state_ref[j] for j in heads]
    for c in range(n_chunks):
        rows = slice(c * C, (c + 1) * C)
        for j in heads:
            b_c = b[j][rows]
            b_last = b_c[C - 1:C, :]
            k_end = (k[j][rows] * jnp.exp(b_last - b_c)).astype(BF16)
            v_c = v_ref[rows, vcol[j]]
            attn = jnp.where(causal, _dot_nt(q_dec[j][rows], k_dec[j][rows]), 0.0).astype(BF16)
            o = _dot(attn, v_c) + _dot_nt(q_dec[j][rows], state[j].astype(BF16))
            state[j] = state[j] * jnp.exp(b_last) + _dot_tn(v_c, k_end)
            ms = jnp.mean(o * o, axis=-1, keepdims=True)
            y = o * lax.rsqrt(ms + RMS_EPS) * gn
            r = r_ref[rows, vcol[j]].astype(F32)
            o_ref[rows, vcol[j]] = (y * (r * _sigmoid(r))).astype(o_ref.dtype)
    for j in heads:
        state_ref[j] = state[j]


def _gla(z3, zs3, w_gate_pad, b_gate, g_out, ts=512):
    bsz, s, _ = z3.shape
    qb, vb = GLA_HPS * GLA_HK, GLA_HPS * GLA_HV
    r = np.arange(ts)
    tri = jnp.asarray((r[:, None] >= r[None, :]) & (r[:, None] // GLA_CHUNK == r[None, :] // GLA_CHUNK), BF16)
    return pl.pallas_call(
        functools.partial(_gla_kernel, n_chunks=ts // GLA_CHUNK),
        grid=(bsz, GLA_HEADS // GLA_HPS, s // ts),
        in_specs=[
            pl.BlockSpec((None, ts, qb), lambda b, h, i: (b, i, Z_GQ // qb + h)),
            pl.BlockSpec((None, ts, qb), lambda b, h, i: (b, i, Z_GK // qb + h)),
            pl.BlockSpec((None, ts, vb), lambda b, h, i: (b, i, Z_GV // vb + h)),
            pl.BlockSpec((None, ts, vb), lambda b, h, i: (b, i, Z_GR // vb + h)),
            pl.BlockSpec((None, ts, LANES), lambda b, h, i: (b, i, 0)),
            pl.BlockSpec((LANES, qb), lambda b, h, i: (0, h)),
            pl.BlockSpec((1, qb), lambda b, h, i: (0, h)),
            pl.BlockSpec((1, GLA_HV), lambda b, h, i: (0, 0)),
            pl.BlockSpec((ts, ts), lambda b, h, i: (0, 0)),
        ],
        out_specs=pl.BlockSpec((None, ts, vb), lambda b, h, i: (b, i, h)),
        out_shape=jax.ShapeDtypeStruct((bsz, s, GLA_DV), BF16),
        scratch_shapes=[pltpu.VMEM((GLA_HPS, GLA_HV, GLA_HK), F32)],
        compiler_params=_params(("parallel", "parallel", "arbitrary")),
        name="gla",
    )(z3, z3, z3, z3, zs3, w_gate_pad, b_gate, g_out, tri)


def _cmp_kernel(t_ref, pek_ref, pev_ref, w1k_ref, w1v_ref, w2k_ref, w2v_ref, o_ref,
                tf_scr, lo_scr, hi_scr, pe_scr, w1_scr, w2_scr):
    hd = NSA_HD
    half = CMP_STRIDE * hd
    nchunk = lo_scr.shape[0] // NSA_KV_HEADS

    @pl.when(pl.program_id(0) == 0)
    def _():
        pe_scr[...] = pek_ref[...]
        w1_scr[...] = w1k_ref[...].astype(BF16)
        w2_scr[...] = w2k_ref[...].astype(BF16)

    @pl.when(pl.program_id(0) == 1)
    def _():
        pe_scr[...] = pev_ref[...]
        w1_scr[...] = w1v_ref[...].astype(BF16)
        w2_scr[...] = w2v_ref[...].astype(BF16)

    for h in range(NSA_KV_HEADS):
        tf_scr[h] = t_ref[:, h * hd:(h + 1) * hd].astype(F32)
        rows = slice(h * nchunk, (h + 1) * nchunk)
        for r in range(CMP_STRIDE):
            t_r = tf_scr.at[h][pl.ds(r, nchunk, stride=CMP_STRIDE), :]
            lo_scr[rows, r * hd:(r + 1) * hd] = (t_r + pe_scr[r:r + 1, :]).astype(BF16)
            hi_scr[rows, r * hd:(r + 1) * hd] = (t_r + pe_scr[CMP_STRIDE + r:CMP_STRIDE + r + 1, :]).astype(BF16)
    a = _dot(lo_scr[...], w1_scr[:half, :])
    bm = _dot(hi_scr[...], w1_scr[half:, :])
    pre = a + pltpu.roll(bm, bm.shape[0] - 1, 0)
    hid = (pre * _sigmoid(pre)).astype(BF16)
    out = _dot(hid, w2_scr[...]).astype(o_ref.dtype)
    for h in range(NSA_KV_HEADS):
        o_ref[h] = out[h * nchunk:(h + 1) * nchunk]


def _compress(z3, pe_k, pe_v, w1_k, w1_v, w2_k, w2_v):
    bsz, s, _ = z3.shape
    hkv, hd = NSA_KV_HEADS, NSA_HD
    nchunk = s // CMP_STRIDE
    width = CMP_STRIDE * hd
    return pl.pallas_call(
        _cmp_kernel,
        grid=(2, bsz),
        in_specs=[
            pl.BlockSpec((None, s, hkv * hd), lambda kv, b: (b, 0, Z_NKV // (hkv * hd) + kv)),
            pl.BlockSpec((CMP_LEN, hd), lambda kv, b: (0, 0)),
            pl.BlockSpec((CMP_LEN, hd), lambda kv, b: (0, 0)),
            pl.BlockSpec((2 * width, CMP_HIDDEN), lambda kv, b: (0, 0)),
            pl.BlockSpec((2 * width, CMP_HIDDEN), lambda kv, b: (0, 0)),
            pl.BlockSpec((CMP_HIDDEN, hd), lambda kv, b: (0, 0)),
            pl.BlockSpec((CMP_HIDDEN, hd), lambda kv, b: (0, 0)),
        ],
        out_specs=pl.BlockSpec((None, None, hkv, nchunk, hd), lambda kv, b: (kv, b, 0, 0, 0)),
        out_shape=jax.ShapeDtypeStruct((2, bsz, hkv, nchunk, hd), BF16),
        scratch_shapes=[pltpu.VMEM((hkv, s, hd), F32),
                        pltpu.VMEM((hkv * nchunk, width), BF16),
                        pltpu.VMEM((hkv * nchunk, width), BF16),
                        pltpu.VMEM((CMP_LEN, hd), F32),
                        pltpu.VMEM((2 * width, CMP_HIDDEN), BF16),
                        pltpu.VMEM((CMP_HIDDEN, hd), BF16)],
        compiler_params=_params(("parallel", "parallel")),
        name="nsa_compress",
    )(z3, pe_k, pe_v, w1_k, w1_v, w2_k, w2_v)


NSA_TQ = 256
NSA_TK = NSA_TQ
LOG2E = 1.4426950408889634


NSA_HPS = 4


def _nsa_kernel(q_ref, sm_ref, kc_ref, vc_ref, ks_ref, vs_ref, kw_ref, vw_ref, mcs_ref, wb_ref,
                o_ref, sel_scr, gate_scr, p_scr, acc_scr):
    G, TQ, TK, HD = NSA_GROUP, NSA_TQ, NSA_TK, NSA_HD
    R = G * TQ
    heads = range(NSA_HPS)
    hk0 = pl.program_id(1) * NSA_HPS
    qi = pl.program_id(2)
    qs = qi * TQ

    def lanes4(x):
        return jnp.concatenate([x] * G, axis=1)

    def hcol(j):
        return slice(j * HD, (j + 1) * HD)

    q4 = []
    for j in heads:
        qj = jnp.concatenate([q_ref[:, (j * G + g) * HD:(j * G + g + 1) * HD] for g in range(G)], axis=0)
        q4.append((qj.astype(F32) * (HD ** -0.5 * LOG2E)).astype(BF16))

    nc = kc_ref.shape[1]
    tq_c = qs + lax.broadcasted_iota(jnp.int32, (nc, TQ), 1)
    cmp_end = lax.broadcasted_iota(jnp.int32, (nc, TQ), 0) * CMP_STRIDE + (CMP_LEN - 1)
    valid_c = lanes4(jnp.where(cmp_end <= tq_c, 1.0, 0.0)) > 0.5
    mcs = mcs_ref[...]
    nb = mcs.shape[0]
    blk = lax.broadcasted_iota(jnp.int32, (nb, TQ), 0)
    tq_b = qs + lax.broadcasted_iota(jnp.int32, (nb, TQ), 1)
    cur = tq_b // SLC_BLOCK
    forced = (blk == 0) | (blk == cur) | (blk == cur - 1)
    visible = blk * SLC_BLOCK <= tq_b
    later = [jnp.where(blk > kb, 1.0, 0.0) for kb in range(nb)]
    o_c = []
    for j in heads:
        s_c = jnp.where(valid_c, _dot_nt(kc_ref[j], q4[j]), NEG_INF)
        e_c = jnp.where(valid_c, jnp.exp2(s_c - jnp.max(s_c, axis=0, keepdims=True)), 0.0)
        l_c = jnp.sum(e_c, axis=0, keepdims=True)
        p_c = e_c * (1.0 / jnp.where(l_c > 0.0, l_c, 1.0))
        o_c.append(_dot_tn(vc_ref[j], p_c.astype(BF16)))

        p_sum = p_c[:, 0:TQ]
        for g in range(1, G):
            p_sum = p_sum + p_c[:, g * TQ:(g + 1) * TQ]
        imp = sum(_dot(mcs, part) for part in _split3(p_sum))
        imp = jnp.where(forced, jnp.inf, imp)
        imp = jnp.where(visible, imp, -jnp.inf)
        rank = jnp.zeros((nb, TQ), F32)
        for kb in range(nb):
            rk = imp[kb:kb + 1, :]
            ahead = jnp.where(rk > imp, 1.0, jnp.where(rk == imp, later[kb], 0.0))
            rank = rank + ahead
        sel_scr[j] = jnp.where(rank < float(min(SLC_TOPN, nb)), 0.0, NEG_INF)

    span = WINDOW + TQ
    w0 = pl.multiple_of(jnp.maximum(qs - WINDOW, 0), TQ)
    bias_w = lanes4(wb_ref[...])
    o_w = []
    for j in heads:
        s_w = _dot_nt(kw_ref[pl.ds(w0, span), hcol(j)], q4[j]) + bias_w
        e_w = jnp.exp2(s_w - jnp.max(s_w, axis=0, keepdims=True)).astype(BF16)
        v_aug = jnp.concatenate([vw_ref[pl.ds(w0, span), hcol(j)], jnp.ones((span, 8), BF16)], axis=1)
        pv = _dot_tn(v_aug, e_w)
        o_w.append(pv)

    blocks_per_tile = TK // SLC_BLOCK

    def sel_bias(j, tile):
        return jnp.concatenate(
            [jnp.broadcast_to(sel_scr[j, pl.ds(tile * blocks_per_tile + i, 1), :], (SLC_BLOCK, TQ))
             for i in range(blocks_per_tile)], axis=0)

    krow = lax.broadcasted_iota(jnp.int32, (TK, TQ), 0)
    qcol = lax.broadcasted_iota(jnp.int32, (TK, TQ), 1)
    diag = pl.ds(pl.multiple_of(qs, TK), TK)
    m_d = []
    for j in heads:
        bias_d = jnp.where(krow <= qcol, sel_bias(j, qi), NEG_INF)
        s_d = _dot_nt(ks_ref[diag, hcol(j)], q4[j]) + lanes4(bias_d)
        m = jnp.max(s_d, axis=0, keepdims=True)
        m_d.append(m)
        p_scr[j] = jnp.exp2(s_d - m).astype(BF16)

    ones_col = jnp.ones((TK, 8), BF16)

    def pv_aug(rows, j):
        return _dot_tn(jnp.concatenate([vs_ref[rows, hcol(j)], ones_col], axis=1), p_scr[j])

    def kv_step(t, carry):
        ms, voff = carry
        prev = pl.ds(pl.multiple_of(voff, TK), TK)
        koff = pl.multiple_of(t * TK, TK)
        ms_n = []
        for j in heads:
            pv = pv_aug(prev, j)
            s = _dot_nt(ks_ref[pl.ds(koff, TK), hcol(j)], q4[j]) + lanes4(sel_bias(j, t))
            m_new = jnp.maximum(ms[j], jnp.max(s, axis=0, keepdims=True))
            alpha = jnp.exp2(ms[j] - m_new)
            ms_n.append(m_new)
            acc_scr[j] = alpha * (acc_scr[j] + pv)
            p_scr[j] = jnp.exp2(s - m_new).astype(BF16)
        return tuple(ms_n), koff

    acc_scr[...] = jnp.zeros_like(acc_scr)
    _, voff = lax.fori_loop(0, qi, kv_step, (tuple(m_d), qs))
    last = pl.ds(pl.multiple_of(voff, TK), TK)
    o_s = []
    for j in heads:
        o_s.append(acc_scr[j] + pv_aug(last, j))

    gate_scr[...] = _sigmoid(sm_ref[...]).T
    for j in heads:
        def gate_row(branch):
            base = SMALL_NGATE + branch * NSA_HEADS + (hk0 + j) * G
            return jnp.concatenate([gate_scr[pl.ds(base + g, 1), :] for g in range(G)], axis=1)

        g_s = gate_row(1) * (1.0 / o_s[j][HD:HD + 1])
        g_w = gate_row(2) * (1.0 / o_w[j][HD:HD + 1])
        o = gate_row(0) * o_c[j] + g_s * o_s[j][:HD] + g_w * o_w[j][:HD]
        for g in range(G):
            o_ref[:, (j * G + g) * HD:(j * G + g + 1) * HD] = o[:, g * TQ:(g + 1) * TQ].T.astype(o_ref.dtype)


def _nsa(z3, zs3, cmp, mcs_t, win_bias):
    bsz, s, _ = z3.shape
    hkv, hd, tq, hps = NSA_KV_HEADS, NSA_HD, NSA_TQ, NSA_HPS
    qw = hps * NSA_GROUP * hd
    kw = hps * hd
    nc = cmp.shape[3]
    nb = mcs_t.shape[0]

    def kv_spec(i):
        return pl.BlockSpec((None, s, kw), lambda b, h, q: (b, 0, (Z_NKV + i * hkv * hd) // kw + h))

    return pl.pallas_call(
        _nsa_kernel,
        grid=(bsz, hkv // hps, s // tq),
        in_specs=[
            pl.BlockSpec((None, tq, qw), lambda b, h, q: (b, q, Z_NQ // qw + h)),
            pl.BlockSpec((None, tq, LANES), lambda b, h, q: (b, q, 0)),
            pl.BlockSpec((None, None, hps, nc, hd), lambda b, h, q: (0, b, h, 0, 0)),
            pl.BlockSpec((None, None, hps, nc, hd), lambda b, h, q: (1, b, h, 0, 0)),
            kv_spec(2), kv_spec(3), kv_spec(4), kv_spec(5),
            pl.BlockSpec(mcs_t.shape, lambda b, h, q: (0, 0)),
            pl.BlockSpec((None,) + win_bias.shape[1:], lambda b, h, q: (jnp.minimum(q, win_bias.shape[0] - 1), 0, 0)),
        ],
        out_specs=pl.BlockSpec((None, tq, qw), lambda b, h, q: (b, q, h)),
        out_shape=jax.ShapeDtypeStruct((bsz, s, NSA_HEADS * hd), BF16),
        scratch_shapes=[pltpu.VMEM((hps, nb, tq), F32), pltpu.VMEM((LANES, tq), F32),
                        pltpu.VMEM((hps, NSA_TK, NSA_GROUP * tq), BF16),
                        pltpu.VMEM((hps, hd + 8, NSA_GROUP * tq), F32)],
        compiler_params=_params(("parallel", "parallel", "arbitrary")),
        name="nsa_attn",
    )(z3, zs3, cmp, cmp, z3, z3, z3, z3, mcs_t, win_bias)


def _merge_kernel(og_ref, on_ref, wg_ref, wn_ref, ga_ref, gb_ref, wo_ref, o_ref, wo_bf_ref, wg_scr, wn_scr):
    wo_bf_ref[...] = wo_ref[...].astype(BF16)

    @pl.when(pl.program_id(1) == 0)
    def _():
        wg_scr[...] = wg_ref[...].astype(BF16)
        wn_scr[...] = wn_ref[...].astype(BF16)

    a = _dot(og_ref[...], wg_scr[...])
    b = _dot(on_ref[...], wn_scr[...])
    ga = _sigmoid(ga_ref[...].astype(F32))
    gb = _sigmoid(gb_ref[...].astype(F32))
    o_ref[...] = (ga * a + gb * b).astype(o_ref.dtype)


def _merge(o_gla, o_nsa, w_pg, w_pn, z2, w_out, tm=512, tn=1024):
    m = o_gla.shape[0]
    n_i = m // tm
    slab = D_MODEL // ((D_MODEL // tn) * n_i)
    assert slab % 16 == 0 and slab * (D_MODEL // tn) * n_i == D_MODEL
    return pl.pallas_call(
        _merge_kernel,
        grid=(D_MODEL // tn, m // tm),
        in_specs=[
            pl.BlockSpec((tm, GLA_DV), lambda j, i: (i, 0)),
            pl.BlockSpec((tm, NSA_HEADS * NSA_HD), lambda j, i: (i, 0)),
            pl.BlockSpec((GLA_DV, tn), lambda j, i: (0, j)),
            pl.BlockSpec((NSA_HEADS * NSA_HD, tn), lambda j, i: (0, j)),
            pl.BlockSpec((tm, tn), lambda j, i: (i, Z_MG // tn + j)),
            pl.BlockSpec((tm, tn), lambda j, i: (i, (Z_MG + D_MODEL) // tn + j)),
            pl.BlockSpec((slab, D_MODEL), lambda j, i: (j * n_i + i, 0)),
        ],
        out_specs=[pl.BlockSpec((tm, tn), lambda j, i: (i, j)),
                   pl.BlockSpec((slab, D_MODEL), lambda j, i: (j * n_i + i, 0))],
        out_shape=[jax.ShapeDtypeStruct((m, D_MODEL), BF16),
                   jax.ShapeDtypeStruct((D_MODEL, D_MODEL), BF16)],
        scratch_shapes=[pltpu.VMEM((GLA_DV, tn), BF16), pltpu.VMEM((NSA_HEADS * NSA_HD, tn), BF16)],
        compiler_params=_params(("parallel", "arbitrary")),
        name="merge_proj",
    )(o_gla, o_nsa, w_pg, w_pn, z2, z2, w_out)


def _out_proj_kernel(mg_ref, w_ref, x_ref, g_ref, x1_ref, h_ref):
    x1 = x_ref[...] + _dot(mg_ref[...], w_ref[...])
    x1_ref[...] = x1
    ms = jnp.mean(x1 * x1, axis=-1, keepdims=True)
    h_ref[...] = (x1 * lax.rsqrt(ms + RMS_EPS) * g_ref[...]).astype(h_ref.dtype)


def _out_proj(merged, w_out, x2, g_ffn, tm=512):
    m = merged.shape[0]
    return pl.pallas_call(
        _out_proj_kernel,
        grid=(m // tm,),
        in_specs=[
            pl.BlockSpec((tm, D_MODEL), lambda i: (i, 0)),
            pl.BlockSpec((D_MODEL, D_MODEL), lambda i: (0, 0)),
            pl.BlockSpec((tm, D_MODEL), lambda i: (i, 0)),
            pl.BlockSpec((1, D_MODEL), lambda i: (0, 0)),
        ],
        out_specs=[
            pl.BlockSpec((tm, D_MODEL), lambda i: (i, 0)),
            pl.BlockSpec((tm, D_MODEL), lambda i: (i, 0)),
        ],
        out_shape=[jax.ShapeDtypeStruct((m, D_MODEL), F32),
                   jax.ShapeDtypeStruct((m, D_MODEL), BF16)],
        compiler_params=_params(("parallel",)),
        name="out_proj",
    )(merged, w_out, x2, g_ffn)


def _ffn_up_kernel(h_ref, wg_ref, wu_ref, wd_ref, o_ref, wd_bf_ref, wg_scr, wu_scr):
    wd_bf_ref[...] = wd_ref[...].astype(BF16)

    @pl.when(pl.program_id(1) == 0)
    def _():
        wg_scr[...] = wg_ref[...].astype(BF16)
        wu_scr[...] = wu_ref[...].astype(BF16)

    h = h_ref[...]
    g = _dot(h, wg_scr[...])
    u = _dot(h, wu_scr[...])
    o_ref[...] = (g * _sigmoid(g) * u).astype(o_ref.dtype)


def _ffn_up(h2, w_gate, w_up, w_down, tm=2048, tf=512):
    m = h2.shape[0]
    n_i = m // tm
    slab = D_FF // ((D_FF // tf) * n_i)
    assert slab % 16 == 0 and slab * (D_FF // tf) * n_i == D_FF
    return pl.pallas_call(
        _ffn_up_kernel,
        grid=(D_FF // tf, m // tm),
        in_specs=[
            pl.BlockSpec((tm, D_MODEL), lambda j, i: (i, 0)),
            pl.BlockSpec((D_MODEL, tf), lambda j, i: (0, j)),
            pl.BlockSpec((D_MODEL, tf), lambda j, i: (0, j)),
            pl.BlockSpec((slab, D_MODEL), lambda j, i: (j * n_i + i, 0)),
        ],
        out_specs=[pl.BlockSpec((tm, tf), lambda j, i: (i, j)),
                   pl.BlockSpec((slab, D_MODEL), lambda j, i: (j * n_i + i, 0))],
        out_shape=[jax.ShapeDtypeStruct((m, D_FF), BF16),
                   jax.ShapeDtypeStruct((D_FF, D_MODEL), BF16)],
        scratch_shapes=[pltpu.VMEM((D_MODEL, tf), BF16), pltpu.VMEM((D_MODEL, tf), BF16)],
        compiler_params=_params(("parallel", "arbitrary")),
        name="ffn_up",
    )(h2, w_gate, w_up, w_down)


def _ffn_down_kernel(a_ref, w_ref, x1_ref, g_ref, o_ref, *, final_norm):
    k = pl.program_id(1)

    @pl.when(k == 0)
    def _():
        o_ref[...] = x1_ref[...]

    o_ref[...] += _dot(a_ref[...], w_ref[...])

    if final_norm:
        @pl.when(k == pl.num_programs(1) - 1)
        def _():
            y = o_ref[...]
            ms = jnp.mean(y * y, axis=-1, keepdims=True)
            o_ref[...] = y * lax.rsqrt(ms + RMS_EPS) * g_ref[...]


def _ffn_down(act, w_down, x1, g_final, final_norm, tm=1024, tk=1408):
    m = act.shape[0]
    return pl.pallas_call(
        functools.partial(_ffn_down_kernel, final_norm=final_norm),
        grid=(m // tm, D_FF // tk),
        in_specs=[
            pl.BlockSpec((tm, tk), lambda i, k: (i, k)),
            pl.BlockSpec((tk, D_MODEL), lambda i, k: (k, 0)),
            pl.BlockSpec((tm, D_MODEL), lambda i, k: (i, 0)),
            pl.BlockSpec((1, D_MODEL), lambda i, k: (0, 0)),
        ],
        out_specs=pl.BlockSpec((tm, D_MODEL), lambda i, k: (i, 0)),
        out_shape=jax.ShapeDtypeStruct((m, D_MODEL), F32),
        compiler_params=_params(("parallel", "arbitrary")),
        name="ffn_down",
    )(act, w_down, x1, g_final)


def _window_bias():
    tq = NSA_TQ
    i = np.arange(WINDOW + tq)[:, None]
    q = np.arange(tq)[None, :]
    out = []
    for p in range(WINDOW // tq + 1):
        qs = p * tq
        w0 = max(qs - WINDOW, 0)
        valid = (w0 + i <= qs + q) & (w0 + i > qs + q - WINDOW)
        out.append(np.where(valid, 0.0, NEG_INF))
    return jnp.asarray(np.stack(out), F32)


def _cmp_to_slc_weights_t(seq):
    n_cmp = seq // CMP_STRIDE - 1
    n_slc = seq // SLC_BLOCK
    c0 = np.arange(n_cmp)[:, None] * CMP_STRIDE
    s0 = np.arange(n_slc)[None, :] * SLC_BLOCK
    ov = np.clip(np.minimum(c0 + CMP_LEN, s0 + SLC_BLOCK) - np.maximum(c0, s0), 0, None)
    m_cs = np.zeros((n_cmp + 1, n_slc), np.float32)
    m_cs[:n_cmp] = ov / CMP_LEN
    return jnp.asarray(m_cs.T, BF16)


def kernel(x, g_mix, w_in, w_gla_gate, b_gla_gate, g_gla_out, pe_cmp_k, w_cmp_k1, w_cmp_k2,
           pe_cmp_v, w_cmp_v1, w_cmp_v2, w_proj_gla, w_proj_nsa, w_out, g_ffn,
           w_ffn_gate, w_ffn_up, w_ffn_down, g_final):
    bsz, seq, d = x.shape
    m = bsz * seq
    depth = w_in.shape[0]
    mcs_t = _cmp_to_slc_weights_t(seq)
    win_bias = _window_bias()
    xc = x.reshape(m, d)
    for l in range(depth):
        w_t = jnp.swapaxes(w_in[l], 0, 1)
        h, zs = _norm_small(xc, g_mix[l][None], w_t)
        z2 = _in_proj(h, w_t)
        z3 = z2.reshape(bsz, seq, Z_SMALL)
        zs3 = zs.reshape(bsz, seq, LANES)

        w_gate_pad = jnp.zeros((LANES, GLA_DK), F32).at[SMALL_GLR:SMALL_GLR + GLA_GATE_RANK].set(w_gla_gate[l])
        o_gla = _gla(z3, zs3, w_gate_pad.astype(BF16), b_gla_gate[l][None], g_gla_out[l][None])

        cmp = _compress(z3, pe_cmp_k[l], pe_cmp_v[l], w_cmp_k1[l], w_cmp_v1[l], w_cmp_k2[l], w_cmp_v2[l])
        o_nsa = _nsa(z3, zs3, cmp, mcs_t, win_bias)

        merged, w_out_bf = _merge(o_gla.reshape(m, GLA_DV), o_nsa.reshape(m, NSA_HEADS * NSA_HD),
                                  w_proj_gla[l], w_proj_nsa[l], z2, w_out[l])
        x1, h2 = _out_proj(merged, w_out_bf, xc, g_ffn[l][None])
        act, w_down_bf = _ffn_up(h2, w_ffn_gate[l], w_ffn_up[l], w_ffn_down[l])
        xc = _ffn_down(act, w_down_bf, x1, g_final[None],
                       final_norm=(l == depth - 1))
    return xc.reshape(bsz, seq, d)
```

```python
import functools

import numpy as np
import jax
import jax.numpy as jnp
from jax import lax
from jax.experimental import pallas as pl
from jax.experimental.pallas import tpu as pltpu

F32 = jnp.float32
BF16 = jnp.bfloat16

D_MODEL = 2048
GLA_HEADS = 4
GLA_DK = D_MODEL // 2
GLA_DV = D_MODEL
GLA_HK = GLA_DK // GLA_HEADS
GLA_HV = GLA_DV // GLA_HEADS
GLA_GATE_RANK = 16
GLA_TAU = 16.0
GLA_CHUNK = 64

NSA_HEADS = 16
NSA_KV_HEADS = 4
NSA_GROUP = NSA_HEADS // NSA_KV_HEADS
NSA_HD = D_MODEL // NSA_HEADS
CMP_STRIDE = 16
CMP_LEN = 2 * CMP_STRIDE
CMP_HIDDEN = 256
SLC_BLOCK = 64
SLC_TOPN = 16
WINDOW = 512
D_FF = ((8 * D_MODEL // 3 + 255) // 256) * 256

IN_SPLIT_SIZES = (GLA_DK, GLA_DK, GLA_DV, GLA_DV, GLA_GATE_RANK,
                  NSA_HEADS * NSA_HD, 6 * NSA_KV_HEADS * NSA_HD, 3 * NSA_HEADS,
                  2 * D_MODEL)

NEG_INF = -1e30
RMS_EPS = 1e-6

Z_GQ = 0
Z_GK = Z_GQ + GLA_DK
Z_GV = Z_GK + GLA_DK
Z_GR = Z_GV + GLA_DV
Z_NQ = Z_GR + GLA_DV
Z_NKV = Z_NQ + NSA_HEADS * NSA_HD
Z_MG = Z_NKV + 6 * NSA_KV_HEADS * NSA_HD
Z_SMALL = Z_MG + 2 * D_MODEL
LANES = 128
SMALL_GLR = 0
SMALL_NGATE = GLA_GATE_RANK

VMEM_LIMIT = 60 * 1024 * 1024


def _dot(a, b):
    return jnp.dot(a, b, preferred_element_type=F32)


def _dot_nt(a, b):
    return lax.dot_general(a, b, (((1,), (1,)), ((), ())), preferred_element_type=F32)


def _dot_tn(a, b):
    return lax.dot_general(a, b, (((0,), (0,)), ((), ())), preferred_element_type=F32)


def _sigmoid(x):
    return 1.0 / (1.0 + jnp.exp(-x))


def _split2(x):
    hi = x.astype(BF16)
    lo = (x - hi.astype(F32)).astype(BF16)
    return hi, lo


def _split3(x):
    hi = x.astype(BF16)
    r1 = x - hi.astype(F32)
    mid = r1.astype(BF16)
    lo = (r1 - mid.astype(F32)).astype(BF16)
    return hi, mid, lo


def _params(sem, **kw):
    return pltpu.CompilerParams(dimension_semantics=sem, vmem_limit_bytes=VMEM_LIMIT, **kw)


_IN_OFFS = np.cumsum((0,) + IN_SPLIT_SIZES)
COL_GLR = int(_IN_OFFS[4])
COL_NGATE = int(_IN_OFFS[7])
assert COL_GLR % LANES == SMALL_GLR and COL_NGATE % LANES == SMALL_NGATE
assert SMALL_GLR + GLA_GATE_RANK == SMALL_NGATE and SMALL_NGATE + 3 * NSA_HEADS <= LANES


def _norm_kernel(x_ref, g_ref, wa_ref, wb_ref, h_ref, zs_ref):
    xf = x_ref[...]
    ms = jnp.mean(xf * xf, axis=-1, keepdims=True)
    h = (xf * lax.rsqrt(ms + RMS_EPS) * g_ref[...]).astype(BF16)
    h_ref[...] = h
    row = lax.broadcasted_iota(jnp.int32, wa_ref.shape, 0)
    w_small = jnp.where(row < SMALL_NGATE, wa_ref[...],
                        jnp.where(row < SMALL_NGATE + 3 * NSA_HEADS, wb_ref[...], 0.0))
    zs_ref[...] = _dot_nt(h, w_small.astype(BF16))


def _norm_small(x2, g, w_t, tm=1024):
    m = x2.shape[0]
    return pl.pallas_call(
        _norm_kernel,
        grid=(m // tm,),
        in_specs=[
            pl.BlockSpec((tm, D_MODEL), lambda i: (i, 0)),
            pl.BlockSpec((1, D_MODEL), lambda i: (0, 0)),
            pl.BlockSpec((LANES, D_MODEL), lambda i: (COL_GLR // LANES, 0)),
            pl.BlockSpec((LANES, D_MODEL), lambda i: (COL_NGATE // LANES, 0)),
        ],
        out_specs=[pl.BlockSpec((tm, D_MODEL), lambda i: (i, 0)),
                   pl.BlockSpec((tm, LANES), lambda i: (i, 0))],
        out_shape=[jax.ShapeDtypeStruct((m, D_MODEL), BF16),
                   jax.ShapeDtypeStruct((m, LANES), F32)],
        compiler_params=_params(("parallel",)),
        name="norm_small",
    )(x2, g, w_t, w_t)


IN_TN = 1024
IN_SHIFT_B = GLA_GATE_RANK
IN_SHIFT_C = GLA_GATE_RANK + 3 * NSA_HEADS
IN_BLOCKS_A = Z_NQ // IN_TN
IN_BLOCKS_AB = Z_MG // IN_TN
IN_ROWS = 48
assert (IN_TN - IN_SHIFT_B) % IN_ROWS == 0 and (IN_TN - IN_SHIFT_C) % IN_ROWS == 0


def _in_proj_kernel(h_ref, wm_ref, we_ref, z_ref, w_scr):
    j = pl.program_id(0)

    def build(shift):
        if shift == 0:
            w_scr[...] = wm_ref[...].astype(BF16)
            return

        def body(r, carry):
            dst = pl.ds(pl.multiple_of(r * IN_ROWS, 16), IN_ROWS)
            src = pl.ds(pl.multiple_of(r * IN_ROWS + shift, 8), IN_ROWS)
            w_scr[dst, :] = wm_ref[src, :].astype(BF16)
            return carry
        lax.fori_loop(0, (IN_TN - shift) // IN_ROWS, body, 0)
        w_scr[IN_TN - shift:, :] = we_ref[:shift, :].astype(BF16)

    first = pl.program_id(1) == 0

    @pl.when(first & (j < IN_BLOCKS_A))
    def _():
        build(0)

    @pl.when(first & (j >= IN_BLOCKS_A) & (j < IN_BLOCKS_AB))
    def _():
        build(IN_SHIFT_B)

    @pl.when(first & (j >= IN_BLOCKS_AB))
    def _():
        build(IN_SHIFT_C)

    z_ref[...] = _dot_nt(h_ref[...], w_scr[...]).astype(z_ref.dtype)


def _in_proj(h, w_t, tm=2048):
    m = h.shape[0]
    tn = IN_TN
    return pl.pallas_call(
        _in_proj_kernel,
        grid=(Z_SMALL // tn, m // tm),
        in_specs=[
            pl.BlockSpec((tm, D_MODEL), lambda j, i: (i, 0)),
            pl.BlockSpec((tn, D_MODEL), lambda j, i: (j, 0)),
            pl.BlockSpec((LANES, D_MODEL), lambda j, i: ((j + 1) * (tn // LANES), 0)),
        ],
        out_specs=pl.BlockSpec((tm, tn), lambda j, i: (i, j)),
        out_shape=jax.ShapeDtypeStruct((m, Z_SMALL), BF16),
        scratch_shapes=[pltpu.VMEM((tn, D_MODEL), BF16)],
        compiler_params=_params(("parallel", "arbitrary")),
        name="in_proj",
    )(h, w_t, w_t)


GLA_HPS = 4


def _gla_kernel(q_ref, k_ref, v_ref, r_ref, sm_ref, wg_ref, bg_ref, gn_ref, tri_ref, o_ref,
                state_ref, *, n_chunks):
    @pl.when(pl.program_id(2) == 0)
    def _():
        state_ref[...] = jnp.zeros_like(state_ref)

    C = GLA_CHUNK
    HK, HV = GLA_HK, GLA_HV
    ts = n_chunks * C
    tri = tri_ref[...]
    crow = lax.broadcasted_iota(jnp.int32, (C, C), 0)
    ccol = lax.broadcasted_iota(jnp.int32, (C, C), 1)
    causal = crow >= ccol

    u = _dot(sm_ref[...].astype(BF16), wg_ref[...]) + bg_ref[...]
    log_a = (jnp.minimum(u, 0.0) - jnp.log(1.0 + jnp.exp(-jnp.abs(u)))) / GLA_TAU
    hi, lo = _split2(log_a)
    b_all = _dot(tri, hi) + _dot(tri, lo)
    gn = gn_ref[...]

    heads = range(GLA_HPS)
    kcol = [slice(j * HK, (j + 1) * HK) for j in heads]
    vcol = [slice(j * HV, (j + 1) * HV) for j in heads]
    b = [b_all[:, kcol[j]] for j in heads]
    k = [k_ref[:, kcol[j]].astype(F32) for j in heads]
    q_dec = [(q_ref[:, kcol[j]].astype(F32) * (HK ** -0.5) * jnp.exp(b[j])).astype(BF16) for j in heads]
    k_dec = [(k[j] * jnp.exp(-b[j])).astype(BF16) for j in heads]
    state = [state_ref[j] for j in heads]
    for c in range(n_chunks):
        rows = slice(c * C, (c + 1) * C)
        for j in heads:
            b_c = b[j][rows]
            b_last = b_c[C - 1:C, :]
            k_end = (k[j][rows] * jnp.exp(b_last - b_c)).astype(BF16)
            v_c = v_ref[rows, vcol[j]]
            attn = jnp.where(causal, _dot_nt(q_dec[j][rows], k_dec[j][rows]), 0.0).astype(BF16)
            o = _dot(attn, v_c) + _dot_nt(q_dec[j][rows], state[j].astype(BF16))
            state[j] = state[j] * jnp.exp(b_last) + _dot_tn(v_c, k_end)
            ms = jnp.mean(o * o, axis=-1, keepdims=True)
            y = o * lax.rsqrt(ms + RMS_EPS) * gn
            r = r_ref[rows, vcol[j]].astype(F32)
            o_ref[rows, vcol[j]] = (y * (r * _sigmoid(r))).astype(o_ref.dtype)
    for j in heads:
        state_ref[j] = state[j]


def _gla(z3, zs3, w_gate_pad, b_gate, g_out, ts=512):
    bsz, s, _ = z3.shape
    qb, vb = GLA_HPS * GLA_HK, GLA_HPS * GLA_HV
    r = np.arange(ts)
    tri = jnp.asarray((r[:, None] >= r[None, :]) & (r[:, None] // GLA_CHUNK == r[None, :] // GLA_CHUNK), BF16)
    return pl.pallas_call(
        functools.partial(_gla_kernel, n_chunks=ts // GLA_CHUNK),
        grid=(bsz, GLA_HEADS // GLA_HPS, s // ts),
        in_specs=[
            pl.BlockSpec((None, ts, qb), lambda b, h, i: (b, i, Z_GQ // qb + h)),
            pl.BlockSpec((None, ts, qb), lambda b, h, i: (b, i, Z_GK // qb + h)),
            pl.BlockSpec((None, ts, vb), lambda b, h, i: (b, i, Z_GV // vb + h)),
            pl.BlockSpec((None, ts, vb), lambda b, h, i: (b, i, Z_GR // vb + h)),
            pl.BlockSpec((None, ts, LANES), lambda b, h, i: (b, i, 0)),
            pl.BlockSpec((LANES, qb), lambda b, h, i: (0, h)),
            pl.BlockSpec((1, qb), lambda b, h, i: (0, h)),
            pl.BlockSpec((1, GLA_HV), lambda b, h, i: (0, 0)),
            pl.BlockSpec((ts, ts), lambda b, h, i: (0, 0)),
        ],
        out_specs=pl.BlockSpec((None, ts, vb), lambda b, h, i: (b, i, h)),
        out_shape=jax.ShapeDtypeStruct((bsz, s, GLA_DV), BF16),
        scratch_shapes=[pltpu.VMEM((GLA_HPS, GLA_HV, GLA_HK), F32)],
        compiler_params=_params(("parallel", "parallel", "arbitrary")),
        name="gla",
    )(z3, z3, z3, z3, zs3, w_gate_pad, b_gate, g_out, tri)


def _cmp_kernel(t_ref, pek_ref, pev_ref, w1k_ref, w1v_ref, w2k_ref, w2v_ref, o_ref,
                tf_scr, lo_scr, hi_scr, pe_scr, w1_scr, w2_scr):
    hd = NSA_HD
    half = CMP_STRIDE * hd
    nchunk = lo_scr.shape[0] // NSA_KV_HEADS

    @pl.when(pl.program_id(0) == 0)
    def _():
        pe_scr[...] = pek_ref[...]
        w1_scr[...] = w1k_ref[...].astype(BF16)
        w2_scr[...] = w2k_ref[...].astype(BF16)

    @pl.when(pl.program_id(0) == 1)
    def _():
        pe_scr[...] = pev_ref[...]
        w1_scr[...] = w1v_ref[...].astype(BF16)
        w2_scr[...] = w2v_ref[...].astype(BF16)

    for h in range(NSA_KV_HEADS):
        tf_scr[h] = t_ref[:, h * hd:(h + 1) * hd].astype(F32)
        rows = slice(h * nchunk, (h + 1) * nchunk)
        for r in range(CMP_STRIDE):
            t_r = tf_scr.at[h][pl.ds(r, nchunk, stride=CMP_STRIDE), :]
            lo_scr[rows, r * hd:(r + 1) * hd] = (t_r + pe_scr[r:r + 1, :]).astype(BF16)
            hi_scr[rows, r * hd:(r + 1) * hd] = (t_r + pe_scr[CMP_STRIDE + r:CMP_STRIDE + r + 1, :]).astype(BF16)
    a = _dot(lo_scr[...], w1_scr[:half, :])
    bm = _dot(hi_scr[...], w1_scr[half:, :])
    pre = a + pltpu.roll(bm, bm.shape[0] - 1, 0)
    hid = (pre * _sigmoid(pre)).astype(BF16)
    out = _dot(hid, w2_scr[...]).astype(o_ref.dtype)
    for h in range(NSA_KV_HEADS):
        o_ref[h] = out[h * nchunk:(h + 1) * nchunk]


def _compress(z3, pe_k, pe_v, w1_k, w1_v, w2_k, w2_v):
    bsz, s, _ = z3.shape
    hkv, hd = NSA_KV_HEADS, NSA_HD
    nchunk = s // CMP_STRIDE
    width = CMP_STRIDE * hd
    return pl.pallas_call(
        _cmp_kernel,
        grid=(2, bsz),
        in_specs=[
            pl.BlockSpec((None, s, hkv * hd), lambda kv, b: (b, 0, Z_NKV // (hkv * hd) + kv)),
            pl.BlockSpec((CMP_LEN, hd), lambda kv, b: (0, 0)),
            pl.BlockSpec((CMP_LEN, hd), lambda kv, b: (0, 0)),
            pl.BlockSpec((2 * width, CMP_HIDDEN), lambda kv, b: (0, 0)),
            pl.BlockSpec((2 * width, CMP_HIDDEN), lambda kv, b: (0, 0)),
            pl.BlockSpec((CMP_HIDDEN, hd), lambda kv, b: (0, 0)),
            pl.BlockSpec((CMP_HIDDEN, hd), lambda kv, b: (0, 0)),
        ],
        out_specs=pl.BlockSpec((None, None, hkv, nchunk, hd), lambda kv, b: (kv, b, 0, 0, 0)),
        out_shape=jax.ShapeDtypeStruct((2, bsz, hkv, nchunk, hd), BF16),
        scratch_shapes=[pltpu.VMEM((hkv, s, hd), F32),
                        pltpu.VMEM((hkv * nchunk, width), BF16),
                        pltpu.VMEM((hkv * nchunk, width), BF16),
                        pltpu.VMEM((CMP_LEN, hd), F32),
                        pltpu.VMEM((2 * width, CMP_HIDDEN), BF16),
                        pltpu.VMEM((CMP_HIDDEN, hd), BF16)],
        compiler_params=_params(("parallel", "parallel")),
        name="nsa_compress",
    )(z3, pe_k, pe_v, w1_k, w1_v, w2_k, w2_v)


NSA_TQ = 256
NSA_TK = NSA_TQ
LOG2E = 1.4426950408889634


NSA_HPS = 4


def _nsa_kernel(q_ref, sm_ref, kc_ref, vc_ref, ks_ref, vs_ref, kw_ref, vw_ref, mcs_ref, wb_ref,
                o_ref, sel_scr, gate_scr, p_scr, acc_scr):
    G, TQ, TK, HD = NSA_GROUP, NSA_TQ, NSA_TK, NSA_HD
    R = G * TQ
    heads = range(NSA_HPS)
    hk0 = pl.program_id(1) * NSA_HPS
    qi = pl.program_id(2)
    qs = qi * TQ

    def lanes4(x):
        return jnp.concatenate([x] * G, axis=1)

    def hcol(j):
        return slice(j * HD, (j + 1) * HD)

    q4 = []
    for j in heads:
        qj = jnp.concatenate([q_ref[:, (j * G + g) * HD:(j * G + g + 1) * HD] for g in range(G)], axis=0)
        q4.append((qj.astype(F32) * (HD ** -0.5 * LOG2E)).astype(BF16))

    nc = kc_ref.shape[1]
    tq_c = qs + lax.broadcasted_iota(jnp.int32, (nc, TQ), 1)
    cmp_end = lax.broadcasted_iota(jnp.int32, (nc, TQ), 0) * CMP_STRIDE + (CMP_LEN - 1)
    bias_c = lanes4(jnp.where(cmp_end <= tq_c, 0.0, NEG_INF))
    keep_c = lanes4(jnp.where(cmp_end <= tq_c, 1.0, 0.0))
    mcs = mcs_ref[...]
    nb = mcs.shape[0]
    blk = lax.broadcasted_iota(jnp.int32, (nb, TQ), 0)
    tq_b = qs + lax.broadcasted_iota(jnp.int32, (nb, TQ), 1)
    cur = tq_b // SLC_BLOCK
    forced = (blk == 0) | (blk == cur) | (blk == cur - 1)
    visible = blk * SLC_BLOCK <= tq_b
    later = [jnp.where(blk > kb, 1.0, 0.0) for kb in range(nb)]
    o_c = []
    for j in heads:
        s_c = _dot_nt(kc_ref[j], q4[j]) + bias_c
        e_c = jnp.exp2(s_c - jnp.max(s_c, axis=0, keepdims=True)) * keep_c
        l_c = jnp.sum(e_c, axis=0, keepdims=True)
        p_c = e_c * (1.0 / jnp.where(l_c > 0.0, l_c, 1.0))
        o_c.append(_dot_tn(vc_ref[j], p_c.astype(BF16)))

        p_sum = p_c[:, 0:TQ]
        for g in range(1, G):
            p_sum = p_sum + p_c[:, g * TQ:(g + 1) * TQ]
        imp = sum(_dot(mcs, part) for part in _split3(p_sum))
        imp = jnp.where(forced, jnp.inf, imp)
        imp = jnp.where(visible, imp, -jnp.inf)
        rank = jnp.zeros((nb, TQ), F32)
        for kb in range(nb):
            rk = imp[kb:kb + 1, :]
            ahead = jnp.where(rk > imp, 1.0, jnp.where(rk == imp, later[kb], 0.0))
            rank = rank + ahead
        sel_scr[j] = jnp.where(rank < float(min(SLC_TOPN, nb)), 0.0, NEG_INF)

    span = WINDOW + TQ
    w0 = pl.multiple_of(jnp.maximum(qs - WINDOW, 0), TQ)
    bias_w = lanes4(wb_ref[...])
    o_w = []
    for j in heads:
        s_w = _dot_nt(kw_ref[pl.ds(w0, span), hcol(j)], q4[j]) + bias_w
        e_w = jnp.exp2(s_w - jnp.max(s_w, axis=0, keepdims=True)).astype(BF16)
        v_aug = jnp.concatenate([vw_ref[pl.ds(w0, span), hcol(j)], jnp.ones((span, 8), BF16)], axis=1)
        pv = _dot_tn(v_aug, e_w)
        o_w.append(pv)

    blocks_per_tile = TK // SLC_BLOCK

    def sel_bias(j, tile):
        return jnp.concatenate(
            [jnp.broadcast_to(sel_scr[j, pl.ds(tile * blocks_per_tile + i, 1), :], (SLC_BLOCK, TQ))
             for i in range(blocks_per_tile)], axis=0)

    krow = lax.broadcasted_iota(jnp.int32, (TK, TQ), 0)
    qcol = lax.broadcasted_iota(jnp.int32, (TK, TQ), 1)
    diag = pl.ds(pl.multiple_of(qs, TK), TK)
    m_d = []
    for j in heads:
        bias_d = jnp.where(krow <= qcol, sel_bias(j, qi), NEG_INF)
        s_d = _dot_nt(ks_ref[diag, hcol(j)], q4[j]) + lanes4(bias_d)
        m = jnp.max(s_d, axis=0, keepdims=True)
        m_d.append(m)
        p_scr[j] = jnp.exp2(s_d - m).astype(BF16)

    ones_col = jnp.ones((TK, 8), BF16)

    def pv_aug(rows, j):
        return _dot_tn(jnp.concatenate([vs_ref[rows, hcol(j)], ones_col], axis=1), p_scr[j])

    def kv_step(t, carry):
        ms, voff = carry
        prev = pl.ds(pl.multiple_of(voff, TK), TK)
        koff = pl.multiple_of(t * TK, TK)
        ms_n = []
        for j in heads:
            pv = pv_aug(prev, j)
            s = _dot_nt(ks_ref[pl.ds(koff, TK), hcol(j)], q4[j]) + lanes4(sel_bias(j, t))
            m_new = jnp.maximum(ms[j], jnp.max(s, axis=0, keepdims=True))
            alpha = jnp.exp2(ms[j] - m_new)
            ms_n.append(m_new)
            acc_scr[j] = alpha * (acc_scr[j] + pv)
            p_scr[j] = jnp.exp2(s - m_new).astype(BF16)
        return tuple(ms_n), koff

    acc_scr[...] = jnp.zeros_like(acc_scr)
    _, voff = lax.fori_loop(0, qi, kv_step, (tuple(m_d), qs))
    last = pl.ds(pl.multiple_of(voff, TK), TK)
    o_s = []
    for j in heads:
        o_s.append(acc_scr[j] + pv_aug(last, j))

    gate_scr[...] = _sigmoid(sm_ref[...]).T
    for j in heads:
        def gate_row(branch):
            base = SMALL_NGATE + branch * NSA_HEADS + (hk0 + j) * G
            return jnp.concatenate([gate_scr[pl.ds(base + g, 1), :] for g in range(G)], axis=1)

        g_s = gate_row(1) * (1.0 / o_s[j][HD:HD + 1])
        g_w = gate_row(2) * (1.0 / o_w[j][HD:HD + 1])
        o = gate_row(0) * o_c[j] + g_s * o_s[j][:HD] + g_w * o_w[j][:HD]
        for g in range(G):
            o_ref[:, (j * G + g) * HD:(j * G + g + 1) * HD] = o[:, g * TQ:(g + 1) * TQ].T.astype(o_ref.dtype)


def _nsa(z3, zs3, cmp, mcs_t, win_bias):
    bsz, s, _ = z3.shape
    hkv, hd, tq, hps = NSA_KV_HEADS, NSA_HD, NSA_TQ, NSA_HPS
    qw = hps * NSA_GROUP * hd
    kw = hps * hd
    nc = cmp.shape[3]
    nb = mcs_t.shape[0]

    def kv_spec(i):
        return pl.BlockSpec((None, s, kw), lambda b, h, q: (b, 0, (Z_NKV + i * hkv * hd) // kw + h))

    return pl.pallas_call(
        _nsa_kernel,
        grid=(bsz, hkv // hps, s // tq),
        in_specs=[
            pl.BlockSpec((None, tq, qw), lambda b, h, q: (b, q, Z_NQ // qw + h)),
            pl.BlockSpec((None, tq, LANES), lambda b, h, q: (b, q, 0)),
            pl.BlockSpec((None, None, hps, nc, hd), lambda b, h, q: (0, b, h, 0, 0)),
            pl.BlockSpec((None, None, hps, nc, hd), lambda b, h, q: (1, b, h, 0, 0)),
            kv_spec(2), kv_spec(3), kv_spec(4), kv_spec(5),
            pl.BlockSpec(mcs_t.shape, lambda b, h, q: (0, 0)),
            pl.BlockSpec((None,) + win_bias.shape[1:], lambda b, h, q: (jnp.minimum(q, win_bias.shape[0] - 1), 0, 0)),
        ],
        out_specs=pl.BlockSpec((None, tq, qw), lambda b, h, q: (b, q, h)),
        out_shape=jax.ShapeDtypeStruct((bsz, s, NSA_HEADS * hd), BF16),
        scratch_shapes=[pltpu.VMEM((hps, nb, tq), F32), pltpu.VMEM((LANES, tq), F32),
                        pltpu.VMEM((hps, NSA_TK, NSA_GROUP * tq), BF16),
                        pltpu.VMEM((hps, hd + 8, NSA_GROUP * tq), F32)],
        compiler_params=_params(("parallel", "parallel", "arbitrary")),
        name="nsa_attn",
    )(z3, zs3, cmp, cmp, z3, z3, z3, z3, mcs_t, win_bias)


def _merge_kernel(og_ref, on_ref, wg_ref, wn_ref, ga_ref, gb_ref, wo_ref, o_ref, wo_bf_ref, wg_scr, wn_scr):
    wo_bf_ref[...] = wo_ref[...].astype(BF16)

    @pl.when(pl.program_id(1) == 0)
    def _():
        wg_scr[...] = wg_ref[...].astype(BF16)
        wn_scr[...] = wn_ref[...].astype(BF16)

    a = _dot(og_ref[...], wg_scr[...])
    b = _dot(on_ref[...], wn_scr[...])
    ga = _sigmoid(ga_ref[...].astype(F32))
    gb = _sigmoid(gb_ref[...].astype(F32))
    o_ref[...] = (ga * a + gb * b).astype(o_ref.dtype)


def _merge(o_gla, o_nsa, w_pg, w_pn, z2, w_out, tm=512, tn=1024):
    m = o_gla.shape[0]
    n_i = m // tm
    slab = D_MODEL // ((D_MODEL // tn) * n_i)
    assert slab % 16 == 0 and slab * (D_MODEL // tn) * n_i == D_MODEL
    return pl.pallas_call(
        _merge_kernel,
        grid=(D_MODEL // tn, m // tm),
        in_specs=[
            pl.BlockSpec((tm, GLA_DV), lambda j, i: (i, 0)),
            pl.BlockSpec((tm, NSA_HEADS * NSA_HD), lambda j, i: (i, 0)),
            pl.BlockSpec((GLA_DV, tn), lambda j, i: (0, j)),
            pl.BlockSpec((NSA_HEADS * NSA_HD, tn), lambda j, i: (0, j)),
            pl.BlockSpec((tm, tn), lambda j, i: (i, Z_MG // tn + j)),
            pl.BlockSpec((tm, tn), lambda j, i: (i, (Z_MG + D_MODEL) // tn + j)),
            pl.BlockSpec((slab, D_MODEL), lambda j, i: (j * n_i + i, 0)),
        ],
        out_specs=[pl.BlockSpec((tm, tn), lambda j, i: (i, j)),
                   pl.BlockSpec((slab, D_MODEL), lambda j, i: (j * n_i + i, 0))],
        out_shape=[jax.ShapeDtypeStruct((m, D_MODEL), BF16),
                   jax.ShapeDtypeStruct((D_MODEL, D_MODEL), BF16)],
        scratch_shapes=[pltpu.VMEM((GLA_DV, tn), BF16), pltpu.VMEM((NSA_HEADS * NSA_HD, tn), BF16)],
        compiler_params=_params(("parallel", "arbitrary")),
        name="merge_proj",
    )(o_gla, o_nsa, w_pg, w_pn, z2, z2, w_out)


def _out_proj_kernel(mg_ref, w_ref, x_ref, g_ref, x1_ref, h_ref):
    x1 = x_ref[...] + _dot(mg_ref[...], w_ref[...])
    x1_ref[...] = x1
    ms = jnp.mean(x1 * x1, axis=-1, keepdims=True)
    h_ref[...] = (x1 * lax.rsqrt(ms + RMS_EPS) * g_ref[...]).astype(h_ref.dtype)


def _out_proj(merged, w_out, x2, g_ffn, tm=512):
    m = merged.shape[0]
    return pl.pallas_call(
        _out_proj_kernel,
        grid=(m // tm,),
        in_specs=[
            pl.BlockSpec((tm, D_MODEL), lambda i: (i, 0)),
            pl.BlockSpec((D_MODEL, D_MODEL), lambda i: (0, 0)),
            pl.BlockSpec((tm, D_MODEL), lambda i: (i, 0)),
            pl.BlockSpec((1, D_MODEL), lambda i: (0, 0)),
        ],
        out_specs=[
            pl.BlockSpec((tm, D_MODEL), lambda i: (i, 0)),
            pl.BlockSpec((tm, D_MODEL), lambda i: (i, 0)),
        ],
        out_shape=[jax.ShapeDtypeStruct((m, D_MODEL), F32),
                   jax.ShapeDtypeStruct((m, D_MODEL), BF16)],
        compiler_params=_params(("parallel",)),
        name="out_proj",
    )(merged, w_out, x2, g_ffn)


def _ffn_up_kernel(h_ref, wg_ref, wu_ref, wd_ref, o_ref, wd_bf_ref, wg_scr, wu_scr):
    wd_bf_ref[...] = wd_ref[...].astype(BF16)

    @pl.when(pl.program_id(1) == 0)
    def _():
        wg_scr[...] = wg_ref[...].astype(BF16)
        wu_scr[...] = wu_ref[...].astype(BF16)

    h = h_ref[...]
    g = _dot(h, wg_scr[...])
    u = _dot(h, wu_scr[...])
    o_ref[...] = (g * _sigmoid(g) * u).astype(o_ref.dtype)


def _ffn_up(h2, w_gate, w_up, w_down, tm=2048, tf=512):
    m = h2.shape[0]
    n_i = m // tm
    slab = D_FF // ((D_FF // tf) * n_i)
    assert slab % 16 == 0 and slab * (D_FF // tf) * n_i == D_FF
    return pl.pallas_call(
        _ffn_up_kernel,
        grid=(D_FF // tf, m // tm),
        in_specs=[
            pl.BlockSpec((tm, D_MODEL), lambda j, i: (i, 0)),
            pl.BlockSpec((D_MODEL, tf), lambda j, i: (0, j)),
            pl.BlockSpec((D_MODEL, tf), lambda j, i: (0, j)),
            pl.BlockSpec((slab, D_MODEL), lambda j, i: (j * n_i + i, 0)),
        ],
        out_specs=[pl.BlockSpec((tm, tf), lambda j, i: (i, j)),
                   pl.BlockSpec((slab, D_MODEL), lambda j, i: (j * n_i + i, 0))],
        out_shape=[jax.ShapeDtypeStruct((m, D_FF), BF16),
                   jax.ShapeDtypeStruct((D_FF, D_MODEL), BF16)],
        scratch_shapes=[pltpu.VMEM((D_MODEL, tf), BF16), pltpu.VMEM((D_MODEL, tf), BF16)],
        compiler_params=_params(("parallel", "arbitrary")),
        name="ffn_up",
    )(h2, w_gate, w_up, w_down)


def _ffn_down_kernel(a_ref, w_ref, x1_ref, g_ref, o_ref, *, final_norm):
    k = pl.program_id(1)

    @pl.when(k == 0)
    def _():
        o_ref[...] = x1_ref[...]

    o_ref[...] += _dot(a_ref[...], w_ref[...])

    if final_norm:
        @pl.when(k == pl.num_programs(1) - 1)
        def _():
            y = o_ref[...]
            ms = jnp.mean(y * y, axis=-1, keepdims=True)
            o_ref[...] = y * lax.rsqrt(ms + RMS_EPS) * g_ref[...]


def _ffn_down(act, w_down, x1, g_final, final_norm, tm=1024, tk=1408):
    m = act.shape[0]
    return pl.pallas_call(
        functools.partial(_ffn_down_kernel, final_norm=final_norm),
        grid=(m // tm, D_FF // tk),
        in_specs=[
            pl.BlockSpec((tm, tk), lambda i, k: (i, k)),
            pl.BlockSpec((tk, D_MODEL), lambda i, k: (k, 0)),
            pl.BlockSpec((tm, D_MODEL), lambda i, k: (i, 0)),
            pl.BlockSpec((1, D_MODEL), lambda i, k: (0, 0)),
        ],
        out_specs=pl.BlockSpec((tm, D_MODEL), lambda i, k: (i, 0)),
        out_shape=jax.ShapeDtypeStruct((m, D_MODEL), F32),
        compiler_params=_params(("parallel", "arbitrary")),
        name="ffn_down",
    )(act, w_down, x1, g_final)


def _window_bias():
    tq = NSA_TQ
    i = np.arange(WINDOW + tq)[:, None]
    q = np.arange(tq)[None, :]
    out = []
    for p in range(WINDOW // tq + 1):
        qs = p * tq
        w0 = max(qs - WINDOW, 0)
        valid = (w0 + i <= qs + q) & (w0 + i > qs + q - WINDOW)
        out.append(np.where(valid, 0.0, NEG_INF))
    return jnp.asarray(np.stack(out), F32)


def _cmp_to_slc_weights_t(seq):
    n_cmp = seq // CMP_STRIDE - 1
    n_slc = seq // SLC_BLOCK
    c0 = np.arange(n_cmp)[:, None] * CMP_STRIDE
    s0 = np.arange(n_slc)[None, :] * SLC_BLOCK
    ov = np.clip(np.minimum(c0 + CMP_LEN, s0 + SLC_BLOCK) - np.maximum(c0, s0), 0, None)
    m_cs = np.zeros((n_cmp + 1, n_slc), np.float32)
    m_cs[:n_cmp] = ov / CMP_LEN
    return jnp.asarray(m_cs.T, BF16)


def kernel(x, g_mix, w_in, w_gla_gate, b_gla_gate, g_gla_out, pe_cmp_k, w_cmp_k1, w_cmp_k2,
           pe_cmp_v, w_cmp_v1, w_cmp_v2, w_proj_gla, w_proj_nsa, w_out, g_ffn,
           w_ffn_gate, w_ffn_up, w_ffn_down, g_final):
    bsz, seq, d = x.shape
    m = bsz * seq
    depth = w_in.shape[0]
    mcs_t = _cmp_to_slc_weights_t(seq)
    win_bias = _window_bias()
    xc = x.reshape(m, d)
    for l in range(depth):
        w_t = jnp.swapaxes(w_in[l], 0, 1)
        h, zs = _norm_small(xc, g_mix[l][None], w_t)
        z2 = _in_proj(h, w_t)
        z3 = z2.reshape(bsz, seq, Z_SMALL)
        zs3 = zs.reshape(bsz, seq, LANES)

        w_gate_pad = jnp.zeros((LANES, GLA_DK), F32).at[SMALL_GLR:SMALL_GLR + GLA_GATE_RANK].set(w_gla_gate[l])
        o_gla = _gla(z3, zs3, w_gate_pad.astype(BF16), b_gla_gate[l][None], g_gla_out[l][None])

        cmp = _compress(z3, pe_cmp_k[l], pe_cmp_v[l], w_cmp_k1[l], w_cmp_v1[l], w_cmp_k2[l], w_cmp_v2[l])
        o_nsa = _nsa(z3, zs3, cmp, mcs_t, win_bias)

        merged, w_out_bf = _merge(o_gla.reshape(m, GLA_DV), o_nsa.reshape(m, NSA_HEADS * NSA_HD),
                                  w_proj_gla[l], w_proj_nsa[l], z2, w_out[l])
        x1, h2 = _out_proj(merged, w_out_bf, xc, g_ffn[l][None])
        act, w_down_bf = _ffn_up(h2, w_ffn_gate[l], w_ffn_up[l], w_ffn_down[l])
        xc = _ffn_down(act, w_down_bf, x1, g_final[None],
                       final_norm=(l == depth - 1))
    return xc.reshape(bsz, seq, d)
```

```python
import functools

import numpy as np
import jax
import jax.numpy as jnp
from jax import lax
from jax.experimental import pallas as pl
from jax.experimental.pallas import tpu as pltpu

F32 = jnp.float32
BF16 = jnp.bfloat16

D_MODEL = 2048
GLA_HEADS = 4
GLA_DK = D_MODEL // 2
GLA_DV = D_MODEL
GLA_HK = GLA_DK // GLA_HEADS
GLA_HV = GLA_DV // GLA_HEADS
GLA_GATE_RANK = 16
GLA_TAU = 16.0
GLA_CHUNK = 64

NSA_HEADS = 16
NSA_KV_HEADS = 4
NSA_GROUP = NSA_HEADS // NSA_KV_HEADS
NSA_HD = D_MODEL // NSA_HEADS
CMP_STRIDE = 16
CMP_LEN = 2 * CMP_STRIDE
CMP_HIDDEN = 256
SLC_BLOCK = 64
SLC_TOPN = 16
WINDOW = 512
D_FF = ((8 * D_MODEL // 3 + 255) // 256) * 256

IN_SPLIT_SIZES = (GLA_DK, GLA_DK, GLA_DV, GLA_DV, GLA_GATE_RANK,
                  NSA_HEADS * NSA_HD, 6 * NSA_KV_HEADS * NSA_HD, 3 * NSA_HEADS,
                  2 * D_MODEL)

NEG_INF = -1e30
RMS_EPS = 1e-6

Z_GQ = 0
Z_GK = Z_GQ + GLA_DK
Z_GV = Z_GK + GLA_DK
Z_GR = Z_GV + GLA_DV
Z_NQ = Z_GR + GLA_DV
Z_NKV = Z_NQ + NSA_HEADS * NSA_HD
Z_MG = Z_NKV + 6 * NSA_KV_HEADS * NSA_HD
Z_SMALL = Z_MG + 2 * D_MODEL
LANES = 128
SUBLANES = 8
SMALL_GLR = 0
SMALL_NGATE = GLA_GATE_RANK

VMEM_LIMIT = 60 * 1024 * 1024


def _dot(a, b):
    return jnp.dot(a, b, preferred_element_type=F32)


def _dot_nt(a, b):
    return lax.dot_general(a, b, (((1,), (1,)), ((), ())), preferred_element_type=F32)


def _dot_tn(a, b):
    return lax.dot_general(a, b, (((0,), (0,)), ((), ())), preferred_element_type=F32)


def _sigmoid(x):
    return 1.0 / (1.0 + jnp.exp(-x))


def _split2(x):
    hi = x.astype(BF16)
    lo = (x - hi.astype(F32)).astype(BF16)
    return hi, lo


def _split3(x):
    hi = x.astype(BF16)
    r1 = x - hi.astype(F32)
    mid = r1.astype(BF16)
    lo = (r1 - mid.astype(F32)).astype(BF16)
    return hi, mid, lo


def _params(sem, **kw):
    return pltpu.CompilerParams(dimension_semantics=sem, vmem_limit_bytes=VMEM_LIMIT, **kw)


_IN_OFFS = np.cumsum((0,) + IN_SPLIT_SIZES)
COL_GLR = int(_IN_OFFS[4])
COL_NGATE = int(_IN_OFFS[7])
assert COL_GLR % LANES == SMALL_GLR and COL_NGATE % LANES == SMALL_NGATE
assert SMALL_GLR + GLA_GATE_RANK == SMALL_NGATE and SMALL_NGATE + 3 * NSA_HEADS <= LANES


def _norm_kernel(x_ref, g_ref, wa_ref, wb_ref, h_ref, zs_ref):
    xf = x_ref[...]
    ms = jnp.mean(xf * xf, axis=-1, keepdims=True)
    h = (xf * lax.rsqrt(ms + RMS_EPS) * g_ref[...]).astype(BF16)
    h_ref[...] = h
    row = lax.broadcasted_iota(jnp.int32, wa_ref.shape, 0)
    w_small = jnp.where(row < SMALL_NGATE, wa_ref[...],
                        jnp.where(row < SMALL_NGATE + 3 * NSA_HEADS, wb_ref[...], 0.0))
    zs_ref[...] = _dot_nt(h, w_small.astype(BF16))


def _norm_small(x2, g, w_t, tm=1024):
    m = x2.shape[0]
    return pl.pallas_call(
        _norm_kernel,
        grid=(m // tm,),
        in_specs=[
            pl.BlockSpec((tm, D_MODEL), lambda i: (i, 0)),
            pl.BlockSpec((1, D_MODEL), lambda i: (0, 0)),
            pl.BlockSpec((LANES, D_MODEL), lambda i: (COL_GLR // LANES, 0)),
            pl.BlockSpec((LANES, D_MODEL), lambda i: (COL_NGATE // LANES, 0)),
        ],
        out_specs=[pl.BlockSpec((tm, D_MODEL), lambda i: (i, 0)),
                   pl.BlockSpec((tm, LANES), lambda i: (i, 0))],
        out_shape=[jax.ShapeDtypeStruct((m, D_MODEL), BF16),
                   jax.ShapeDtypeStruct((m, LANES), F32)],
        compiler_params=_params(("parallel",)),
        name="norm_small",
    )(x2, g, w_t, w_t)


IN_TN = 1024
IN_SHIFT_B = GLA_GATE_RANK
IN_SHIFT_C = GLA_GATE_RANK + 3 * NSA_HEADS
IN_BLOCKS_A = Z_NQ // IN_TN
IN_BLOCKS_AB = Z_MG // IN_TN
IN_ROWS = 48
assert (IN_TN - IN_SHIFT_B) % IN_ROWS == 0 and (IN_TN - IN_SHIFT_C) % IN_ROWS == 0


def _in_proj_kernel(h_ref, wm_ref, we_ref, z_ref, w_scr):
    j = pl.program_id(0)

    def build(shift):
        if shift == 0:
            w_scr[...] = wm_ref[...].astype(BF16)
            return

        def body(r, carry):
            dst = pl.ds(pl.multiple_of(r * IN_ROWS, 16), IN_ROWS)
            src = pl.ds(pl.multiple_of(r * IN_ROWS + shift, 8), IN_ROWS)
            w_scr[dst, :] = wm_ref[src, :].astype(BF16)
            return carry
        lax.fori_loop(0, (IN_TN - shift) // IN_ROWS, body, 0)
        w_scr[IN_TN - shift:, :] = we_ref[:shift, :].astype(BF16)

    first = pl.program_id(1) == 0

    @pl.when(first & (j < IN_BLOCKS_A))
    def _():
        build(0)

    @pl.when(first & (j >= IN_BLOCKS_A) & (j < IN_BLOCKS_AB))
    def _():
        build(IN_SHIFT_B)

    @pl.when(first & (j >= IN_BLOCKS_AB))
    def _():
        build(IN_SHIFT_C)

    z_ref[...] = _dot_nt(h_ref[...], w_scr[...]).astype(z_ref.dtype)


def _in_proj(h, w_t, tm=2048):
    m = h.shape[0]
    tn = IN_TN
    return pl.pallas_call(
        _in_proj_kernel,
        grid=(Z_SMALL // tn, m // tm),
        in_specs=[
            pl.BlockSpec((tm, D_MODEL), lambda j, i: (i, 0)),
            pl.BlockSpec((tn, D_MODEL), lambda j, i: (j, 0)),
            pl.BlockSpec((LANES, D_MODEL), lambda j, i: ((j + 1) * (tn // LANES), 0)),
        ],
        out_specs=pl.BlockSpec((tm, tn), lambda j, i: (i, j)),
        out_shape=jax.ShapeDtypeStruct((m, Z_SMALL), BF16),
        scratch_shapes=[pltpu.VMEM((tn, D_MODEL), BF16)],
        compiler_params=_params(("parallel", "arbitrary")),
        name="in_proj",
    )(h, w_t, w_t)


GLA_HPS = 4


def _gla_kernel(q_ref, k_ref, v_ref, r_ref, sm_ref, wg_ref, bg_ref, gn_ref, tri_ref, o_ref,
                state_ref, *, n_chunks):
    @pl.when(pl.program_id(2) == 0)
    def _():
        state_ref[...] = jnp.zeros_like(state_ref)

    C = GLA_CHUNK
    HK, HV = GLA_HK, GLA_HV
    ts = n_chunks * C
    tri = tri_ref[...]
    crow = lax.broadcasted_iota(jnp.int32, (C, C), 0)
    ccol = lax.broadcasted_iota(jnp.int32, (C, C), 1)
    causal = crow >= ccol

    u = _dot(sm_ref[...].astype(BF16), wg_ref[...]) + bg_ref[...]
    log_a = (jnp.minimum(u, 0.0) - jnp.log(1.0 + jnp.exp(-jnp.abs(u)))) / GLA_TAU
    hi, lo = _split2(log_a)
    b_all = _dot(tri, hi) + _dot(tri, lo)
    gn = gn_ref[...]

    heads = range(GLA_HPS)
    kcol = [slice(j * HK, (j + 1) * HK) for j in heads]
    vcol = [slice(j * HV, (j + 1) * HV) for j in heads]
    b = [b_all[:, kcol[j]] for j in heads]
    k = [k_ref[:, kcol[j]].astype(F32) for j in heads]
    q_dec = [(q_ref[:, kcol[j]].astype(F32) * (HK ** -0.5) * jnp.exp(b[j])).astype(BF16) for j in heads]
    k_dec = [(k[j] * jnp.exp(-b[j])).astype(BF16) for j in heads]
    state = [state_ref[j] for j in heads]
    for c in range(n_chunks):
        rows = slice(c * C, (c + 1) * C)
        for j in heads:
            b_c = b[j][rows]
            b_last = b_c[C - 1:C, :]
            k_end = (k[j][rows] * jnp.exp(b_last - b_c)).astype(BF16)
            v_c = v_ref[rows, vcol[j]]
            attn = jnp.where(causal, _dot_nt(q_dec[j][rows], k_dec[j][rows]), 0.0).astype(BF16)
            o = _dot(attn, v_c) + _dot_nt(q_dec[j][rows], state[j].astype(BF16))
            state[j] = state[j] * jnp.exp(b_last) + _dot_tn(v_c, k_end)
            ms = jnp.mean(o * o, axis=-1, keepdims=True)
            y = o * lax.rsqrt(ms + RMS_EPS) * gn
            r = r_ref[rows, vcol[j]].astype(F32)
            o_ref[rows, vcol[j]] = (y * (r * _sigmoid(r))).astype(o_ref.dtype)
    for j in heads:
        state_ref[j] = state[j]


def _gla(z3, zs3, w_gate_pad, b_gate, g_out, ts=512):
    bsz, s, _ = z3.shape
    qb, vb = GLA_HPS * GLA_HK, GLA_HPS * GLA_HV
    r = np.arange(ts)
    tri = jnp.asarray((r[:, None] >= r[None, :]) & (r[:, None] // GLA_CHUNK == r[None, :] // GLA_CHUNK), BF16)
    return pl.pallas_call(
        functools.partial(_gla_kernel, n_chunks=ts // GLA_CHUNK),
        grid=(bsz, GLA_HEADS // GLA_HPS, s // ts),
        in_specs=[
            pl.BlockSpec((None, ts, qb), lambda b, h, i: (b, i, Z_GQ // qb + h)),
            pl.BlockSpec((None, ts, qb), lambda b, h, i: (b, i, Z_GK // qb + h)),
            pl.BlockSpec((None, ts, vb), lambda b, h, i: (b, i, Z_GV // vb + h)),
            pl.BlockSpec((None, ts, vb), lambda b, h, i: (b, i, Z_GR // vb + h)),
            pl.BlockSpec((None, ts, LANES), lambda b, h, i: (b, i, 0)),
            pl.BlockSpec((LANES, qb), lambda b, h, i: (0, h)),
            pl.BlockSpec((1, qb), lambda b, h, i: (0, h)),
            pl.BlockSpec((1, GLA_HV), lambda b, h, i: (0, 0)),
            pl.BlockSpec((ts, ts), lambda b, h, i: (0, 0)),
        ],
        out_specs=pl.BlockSpec((None, ts, vb), lambda b, h, i: (b, i, h)),
        out_shape=jax.ShapeDtypeStruct((bsz, s, GLA_DV), BF16),
        scratch_shapes=[pltpu.VMEM((GLA_HPS, GLA_HV, GLA_HK), F32)],
        compiler_params=_params(("parallel", "parallel", "arbitrary")),
        name="gla",
    )(z3, z3, z3, z3, zs3, w_gate_pad, b_gate, g_out, tri)


def _cmp_kernel(t_ref, pek_ref, pev_ref, w1k_ref, w1v_ref, w2k_ref, w2v_ref, o_ref,
                tf_scr, lo_scr, hi_scr, pe_scr, w1_scr, w2_scr):
    hd = NSA_HD
    half = CMP_STRIDE * hd
    nchunk = lo_scr.shape[0] // NSA_KV_HEADS

    @pl.when(pl.program_id(0) == 0)
    def _():
        pe_scr[...] = pek_ref[...]
        w1_scr[...] = w1k_ref[...].astype(BF16)
        w2_scr[...] = w2k_ref[...].astype(BF16)

    @pl.when(pl.program_id(0) == 1)
    def _():
        pe_scr[...] = pev_ref[...]
        w1_scr[...] = w1v_ref[...].astype(BF16)
        w2_scr[...] = w2v_ref[...].astype(BF16)

    for h in range(NSA_KV_HEADS):
        tf_scr[h] = t_ref[:, h * hd:(h + 1) * hd].astype(F32)
        rows = slice(h * nchunk, (h + 1) * nchunk)
        for r in range(CMP_STRIDE):
            t_r = tf_scr.at[h][pl.ds(r, nchunk, stride=CMP_STRIDE), :]
            lo_scr[rows, r * hd:(r + 1) * hd] = (t_r + pe_scr[r:r + 1, :]).astype(BF16)
            hi_scr[rows, r * hd:(r + 1) * hd] = (t_r + pe_scr[CMP_STRIDE + r:CMP_STRIDE + r + 1, :]).astype(BF16)
    a = _dot(lo_scr[...], w1_scr[:half, :])
    bm = _dot(hi_scr[...], w1_scr[half:, :])
    pre = a + pltpu.roll(bm, bm.shape[0] - 1, 0)
    hid = (pre * _sigmoid(pre)).astype(BF16)
    out = _dot(hid, w2_scr[...]).astype(o_ref.dtype)
    for h in range(NSA_KV_HEADS):
        o_ref[h] = out[h * nchunk:(h + 1) * nchunk]


def _compress(z3, pe_k, pe_v, w1_k, w1_v, w2_k, w2_v):
    bsz, s, _ = z3.shape
    hkv, hd = NSA_KV_HEADS, NSA_HD
    nchunk = s // CMP_STRIDE
    width = CMP_STRIDE * hd
    return pl.pallas_call(
        _cmp_kernel,
        grid=(2, bsz),
        in_specs=[
            pl.BlockSpec((None, s, hkv * hd), lambda kv, b: (b, 0, Z_NKV // (hkv * hd) + kv)),
            pl.BlockSpec((CMP_LEN, hd), lambda kv, b: (0, 0)),
            pl.BlockSpec((CMP_LEN, hd), lambda kv, b: (0, 0)),
            pl.BlockSpec((2 * width, CMP_HIDDEN), lambda kv, b: (0, 0)),
            pl.BlockSpec((2 * width, CMP_HIDDEN), lambda kv, b: (0, 0)),
            pl.BlockSpec((CMP_HIDDEN, hd), lambda kv, b: (0, 0)),
            pl.BlockSpec((CMP_HIDDEN, hd), lambda kv, b: (0, 0)),
        ],
        out_specs=pl.BlockSpec((None, None, hkv, nchunk, hd), lambda kv, b: (kv, b, 0, 0, 0)),
        out_shape=jax.ShapeDtypeStruct((2, bsz, hkv, nchunk, hd), BF16),
        scratch_shapes=[pltpu.VMEM((hkv, s, hd), F32),
                        pltpu.VMEM((hkv * nchunk, width), BF16),
                        pltpu.VMEM((hkv * nchunk, width), BF16),
                        pltpu.VMEM((CMP_LEN, hd), F32),
                        pltpu.VMEM((2 * width, CMP_HIDDEN), BF16),
                        pltpu.VMEM((CMP_HIDDEN, hd), BF16)],
        compiler_params=_params(("parallel", "parallel")),
        name="nsa_compress",
    )(z3, pe_k, pe_v, w1_k, w1_v, w2_k, w2_v)


NSA_TQ = 256
NSA_TK = NSA_TQ
LOG2E = 1.4426950408889634


NSA_HPS = 4


def _nsa_kernel(q_ref, sm_ref, kc_ref, vc_ref, ks_ref, vs_ref, kw_ref, vw_ref, mcs_ref, wb_ref,
                o_ref, sel_scr, gate_scr, p_scr, acc_scr):
    G, TQ, TK, HD = NSA_GROUP, NSA_TQ, NSA_TK, NSA_HD
    R = G * TQ
    heads = range(NSA_HPS)
    hk0 = pl.program_id(1) * NSA_HPS
    qi = pl.program_id(2)
    qs = qi * TQ

    def lanes4(x):
        return jnp.concatenate([x] * G, axis=1)

    def hcol(j):
        return slice(j * HD, (j + 1) * HD)

    q4 = []
    for j in heads:
        qj = jnp.concatenate([q_ref[:, (j * G + g) * HD:(j * G + g + 1) * HD] for g in range(G)], axis=0)
        q4.append((qj.astype(F32) * (HD ** -0.5 * LOG2E)).astype(BF16))

    nc = kc_ref.shape[1]
    tq_c = qs + lax.broadcasted_iota(jnp.int32, (nc, TQ), 1)
    cmp_end = lax.broadcasted_iota(jnp.int32, (nc, TQ), 0) * CMP_STRIDE + (CMP_LEN - 1)
    valid_c = lanes4(jnp.where(cmp_end <= tq_c, 1.0, 0.0)) > 0.5
    mcs = mcs_ref[...]
    nb = mcs.shape[0]
    blk = lax.broadcasted_iota(jnp.int32, (nb, TQ), 0)
    tq_b = qs + lax.broadcasted_iota(jnp.int32, (nb, TQ), 1)
    cur = tq_b // SLC_BLOCK
    forced = (blk == 0) | (blk == cur) | (blk == cur - 1)
    visible = blk * SLC_BLOCK <= tq_b
    later = [jnp.where(blk > kb, 1.0, 0.0) for kb in range(nb)]
    o_c = []
    for j in heads:
        s_c = jnp.where(valid_c, _dot_nt(kc_ref[j], q4[j]), NEG_INF)
        e_c = jnp.where(valid_c, jnp.exp2(s_c - jnp.max(s_c, axis=0, keepdims=True)), 0.0)
        l_c = jnp.sum(e_c, axis=0, keepdims=True)
        p_c = e_c * (1.0 / jnp.where(l_c > 0.0, l_c, 1.0))
        o_c.append(_dot_tn(vc_ref[j], p_c.astype(BF16)))

        p_sum = p_c[:, 0:TQ]
        for g in range(1, G):
            p_sum = p_sum + p_c[:, g * TQ:(g + 1) * TQ]
        imp = sum(_dot(mcs, part) for part in _split3(p_sum))
        imp = jnp.where(forced, jnp.inf, imp)
        imp = jnp.where(visible, imp, -jnp.inf)
        rank = jnp.zeros((nb, TQ), F32)
        for kb in range(nb):
            rk = imp[kb:kb + 1, :]
            ahead = jnp.where(rk > imp, 1.0, jnp.where(rk == imp, later[kb], 0.0))
            rank = rank + ahead
        sel_scr[j] = jnp.where(rank < float(min(SLC_TOPN, nb)), 0.0, NEG_INF)

    span = WINDOW + TQ
    w0 = pl.multiple_of(jnp.maximum(qs - WINDOW, 0), TQ)
    bias_w = lanes4(wb_ref[...])
    o_w = []
    for j in heads:
        s_w = _dot_nt(kw_ref[pl.ds(w0, span), hcol(j)], q4[j]) + bias_w
        e_w = jnp.exp2(s_w - jnp.max(s_w, axis=0, keepdims=True)).astype(BF16)
        v_aug = jnp.concatenate([vw_ref[pl.ds(w0, span), hcol(j)], jnp.ones((span, SUBLANES), BF16)], axis=1)
        pv = _dot_tn(v_aug, e_w)
        o_w.append(pv)

    blocks_per_tile = TK // SLC_BLOCK

    def sel_bias(j, tile):
        return jnp.concatenate(
            [jnp.broadcast_to(sel_scr[j, pl.ds(tile * blocks_per_tile + i, 1), :], (SLC_BLOCK, TQ))
             for i in range(blocks_per_tile)], axis=0)

    krow = lax.broadcasted_iota(jnp.int32, (TK, TQ), 0)
    qcol = lax.broadcasted_iota(jnp.int32, (TK, TQ), 1)
    diag = pl.ds(pl.multiple_of(qs, TK), TK)
    m_d = []
    for j in heads:
        bias_d = jnp.where(krow <= qcol, sel_bias(j, qi), NEG_INF)
        s_d = _dot_nt(ks_ref[diag, hcol(j)], q4[j]) + lanes4(bias_d)
        m = jnp.max(s_d, axis=0, keepdims=True)
        m_d.append(m)
        p_scr[j] = jnp.exp2(s_d - m).astype(BF16)

    ones_col = jnp.ones((TK, SUBLANES), BF16)

    def pv_aug(rows, j):
        return _dot_tn(jnp.concatenate([vs_ref[rows, hcol(j)], ones_col], axis=1), p_scr[j])

    def kv_step(t, carry):
        ms, voff = carry
        prev = pl.ds(pl.multiple_of(voff, TK), TK)
        koff = pl.multiple_of(t * TK, TK)
        ms_n = []
        for j in heads:
            pv = pv_aug(prev, j)
            s = _dot_nt(ks_ref[pl.ds(koff, TK), hcol(j)], q4[j]) + lanes4(sel_bias(j, t))
            m_new = jnp.maximum(ms[j], jnp.max(s, axis=0, keepdims=True))
            alpha = jnp.exp2(ms[j] - m_new)
            ms_n.append(m_new)
            acc_scr[j] = alpha * (acc_scr[j] + pv)
            p_scr[j] = jnp.exp2(s - m_new).astype(BF16)
        return tuple(ms_n), koff

    acc_scr[...] = jnp.zeros_like(acc_scr)
    _, voff = lax.fori_loop(0, qi, kv_step, (tuple(m_d), qs))
    last = pl.ds(pl.multiple_of(voff, TK), TK)
    o_s = []
    for j in heads:
        o_s.append(acc_scr[j] + pv_aug(last, j))

    gate_scr[...] = _sigmoid(sm_ref[...]).T
    for j in heads:
        def gate_row(branch):
            base = SMALL_NGATE + branch * NSA_HEADS + (hk0 + j) * G
            return jnp.concatenate([gate_scr[pl.ds(base + g, 1), :] for g in range(G)], axis=1)

        g_s = gate_row(1) * (1.0 / o_s[j][HD:HD + 1])
        g_w = gate_row(2) * (1.0 / o_w[j][HD:HD + 1])
        o = gate_row(0) * o_c[j] + g_s * o_s[j][:HD] + g_w * o_w[j][:HD]
        for g in range(G):
            o_ref[:, (j * G + g) * HD:(j * G + g + 1) * HD] = o[:, g * TQ:(g + 1) * TQ].T.astype(o_ref.dtype)


def _nsa(z3, zs3, cmp, mcs_t, win_bias):
    bsz, s, _ = z3.shape
    hkv, hd, tq, hps = NSA_KV_HEADS, NSA_HD, NSA_TQ, NSA_HPS
    qw = hps * NSA_GROUP * hd
    kw = hps * hd
    nc = cmp.shape[3]
    nb = mcs_t.shape[0]

    def kv_spec(i):
        return pl.BlockSpec((None, s, kw), lambda b, h, q: (b, 0, (Z_NKV + i * hkv * hd) // kw + h))

    return pl.pallas_call(
        _nsa_kernel,
        grid=(bsz, hkv // hps, s // tq),
        in_specs=[
            pl.BlockSpec((None, tq, qw), lambda b, h, q: (b, q, Z_NQ // qw + h)),
            pl.BlockSpec((None, tq, LANES), lambda b, h, q: (b, q, 0)),
            pl.BlockSpec((None, None, hps, nc, hd), lambda b, h, q: (0, b, h, 0, 0)),
            pl.BlockSpec((None, None, hps, nc, hd), lambda b, h, q: (1, b, h, 0, 0)),
            kv_spec(2), kv_spec(3), kv_spec(4), kv_spec(5),
            pl.BlockSpec(mcs_t.shape, lambda b, h, q: (0, 0)),
            pl.BlockSpec((None,) + win_bias.shape[1:], lambda b, h, q: (jnp.minimum(q, win_bias.shape[0] - 1), 0, 0)),
        ],
        out_specs=pl.BlockSpec((None, tq, qw), lambda b, h, q: (b, q, h)),
        out_shape=jax.ShapeDtypeStruct((bsz, s, NSA_HEADS * hd), BF16),
        scratch_shapes=[pltpu.VMEM((hps, nb, tq), F32), pltpu.VMEM((LANES, tq), F32),
                        pltpu.VMEM((hps, NSA_TK, NSA_GROUP * tq), BF16),
                        pltpu.VMEM((hps, hd + SUBLANES, NSA_GROUP * tq), F32)],
        compiler_params=_params(("parallel", "parallel", "arbitrary")),
        name="nsa_attn",
    )(z3, zs3, cmp, cmp, z3, z3, z3, z3, mcs_t, win_bias)


def _merge_kernel(og_ref, on_ref, wg_ref, wn_ref, ga_ref, gb_ref, wo_ref, o_ref, wo_bf_ref, wg_scr, wn_scr):
    wo_bf_ref[...] = wo_ref[...].astype(BF16)

    @pl.when(pl.program_id(1) == 0)
    def _():
        wg_scr[...] = wg_ref[...].astype(BF16)
        wn_scr[...] = wn_ref[...].astype(BF16)

    a = _dot(og_ref[...], wg_scr[...])
    b = _dot(on_ref[...], wn_scr[...])
    ga = _sigmoid(ga_ref[...].astype(F32))
    gb = _sigmoid(gb_ref[...].astype(F32))
    o_ref[...] = (ga * a + gb * b).astype(o_ref.dtype)


def _merge(o_gla, o_nsa, w_pg, w_pn, z2, w_out, tm=512, tn=1024):
    m = o_gla.shape[0]
    n_i = m // tm
    slab = D_MODEL // ((D_MODEL // tn) * n_i)
    assert slab % 16 == 0 and slab * (D_MODEL // tn) * n_i == D_MODEL
    return pl.pallas_call(
        _merge_kernel,
        grid=(D_MODEL // tn, m // tm),
        in_specs=[
            pl.BlockSpec((tm, GLA_DV), lambda j, i: (i, 0)),
            pl.BlockSpec((tm, NSA_HEADS * NSA_HD), lambda j, i: (i, 0)),
            pl.BlockSpec((GLA_DV, tn), lambda j, i: (0, j)),
            pl.BlockSpec((NSA_HEADS * NSA_HD, tn), lambda j, i: (0, j)),
            pl.BlockSpec((tm, tn), lambda j, i: (i, Z_MG // tn + j)),
            pl.BlockSpec((tm, tn), lambda j, i: (i, (Z_MG + D_MODEL) // tn + j)),
            pl.BlockSpec((slab, D_MODEL), lambda j, i: (j * n_i + i, 0)),
        ],
        out_specs=[pl.BlockSpec((tm, tn), lambda j, i: (i, j)),
                   pl.BlockSpec((slab, D_MODEL), lambda j, i: (j * n_i + i, 0))],
        out_shape=[jax.ShapeDtypeStruct((m, D_MODEL), BF16),
                   jax.ShapeDtypeStruct((D_MODEL, D_MODEL), BF16)],
        scratch_shapes=[pltpu.VMEM((GLA_DV, tn), BF16), pltpu.VMEM((NSA_HEADS * NSA_HD, tn), BF16)],
        compiler_params=_params(("parallel", "arbitrary")),
        name="merge_proj",
    )(o_gla, o_nsa, w_pg, w_pn, z2, z2, w_out)


def _out_proj_kernel(mg_ref, w_ref, x_ref, g_ref, x1_ref, h_ref):
    x1 = x_ref[...] + _dot(mg_ref[...], w_ref[...])
    x1_ref[...] = x1
    ms = jnp.mean(x1 * x1, axis=-1, keepdims=True)
    h_ref[...] = (x1 * lax.rsqrt(ms + RMS_EPS) * g_ref[...]).astype(h_ref.dtype)


def _out_proj(merged, w_out, x2, g_ffn, tm=512):
    m = merged.shape[0]
    return pl.pallas_call(
        _out_proj_kernel,
        grid=(m // tm,),
        in_specs=[
            pl.BlockSpec((tm, D_MODEL), lambda i: (i, 0)),
            pl.BlockSpec((D_MODEL, D_MODEL), lambda i: (0, 0)),
            pl.BlockSpec((tm, D_MODEL), lambda i: (i, 0)),
            pl.BlockSpec((1, D_MODEL), lambda i: (0, 0)),
        ],
        out_specs=[
            pl.BlockSpec((tm, D_MODEL), lambda i: (i, 0)),
            pl.BlockSpec((tm, D_MODEL), lambda i: (i, 0)),
        ],
        out_shape=[jax.ShapeDtypeStruct((m, D_MODEL), F32),
                   jax.ShapeDtypeStruct((m, D_MODEL), BF16)],
        compiler_params=_params(("parallel",)),
        name="out_proj",
    )(merged, w_out, x2, g_ffn)


def _ffn_up_kernel(h_ref, wg_ref, wu_ref, wd_ref, o_ref, wd_bf_ref, wg_scr, wu_scr):
    wd_bf_ref[...] = wd_ref[...].astype(BF16)

    @pl.when(pl.program_id(1) == 0)
    def _():
        wg_scr[...] = wg_ref[...].astype(BF16)
        wu_scr[...] = wu_ref[...].astype(BF16)

    h = h_ref[...]
    g = _dot(h, wg_scr[...])
    u = _dot(h, wu_scr[...])
    o_ref[...] = (g * _sigmoid(g) * u).astype(o_ref.dtype)


def _ffn_up(h2, w_gate, w_up, w_down, tm=2048, tf=512):
    m = h2.shape[0]
    n_i = m // tm
    slab = D_FF // ((D_FF // tf) * n_i)
    assert slab % 16 == 0 and slab * (D_FF // tf) * n_i == D_FF
    return pl.pallas_call(
        _ffn_up_kernel,
        grid=(D_FF // tf, m // tm),
        in_specs=[
            pl.BlockSpec((tm, D_MODEL), lambda j, i: (i, 0)),
            pl.BlockSpec((D_MODEL, tf), lambda j, i: (0, j)),
            pl.BlockSpec((D_MODEL, tf), lambda j, i: (0, j)),
            pl.BlockSpec((slab, D_MODEL), lambda j, i: (j * n_i + i, 0)),
        ],
        out_specs=[pl.BlockSpec((tm, tf), lambda j, i: (i, j)),
                   pl.BlockSpec((slab, D_MODEL), lambda j, i: (j * n_i + i, 0))],
        out_shape=[jax.ShapeDtypeStruct((m, D_FF), BF16),
                   jax.ShapeDtypeStruct((D_FF, D_MODEL), BF16)],
        scratch_shapes=[pltpu.VMEM((D_MODEL, tf), BF16), pltpu.VMEM((D_MODEL, tf), BF16)],
        compiler_params=_params(("parallel", "arbitrary")),
        name="ffn_up",
    )(h2, w_gate, w_up, w_down)


def _ffn_down_kernel(a_ref, w_ref, x1_ref, g_ref, o_ref, *, final_norm):
    k = pl.program_id(1)

    @pl.when(k == 0)
    def _():
        o_ref[...] = x1_ref[...]

    o_ref[...] += _dot(a_ref[...], w_ref[...])

    if final_norm:
        @pl.when(k == pl.num_programs(1) - 1)
        def _():
            y = o_ref[...]
            ms = jnp.mean(y * y, axis=-1, keepdims=True)
            o_ref[...] = y * lax.rsqrt(ms + RMS_EPS) * g_ref[...]


def _ffn_down(act, w_down, x1, g_final, final_norm, tm=1024, tk=1408):
    m = act.shape[0]
    return pl.pallas_call(
        functools.partial(_ffn_down_kernel, final_norm=final_norm),
        grid=(m // tm, D_FF // tk),
        in_specs=[
            pl.BlockSpec((tm, tk), lambda i, k: (i, k)),
            pl.BlockSpec((tk, D_MODEL), lambda i, k: (k, 0)),
            pl.BlockSpec((tm, D_MODEL), lambda i, k: (i, 0)),
            pl.BlockSpec((1, D_MODEL), lambda i, k: (0, 0)),
        ],
        out_specs=pl.BlockSpec((tm, D_MODEL), lambda i, k: (i, 0)),
        out_shape=jax.ShapeDtypeStruct((m, D_MODEL), F32),
        compiler_params=_params(("parallel", "arbitrary")),
        name="ffn_down",
    )(act, w_down, x1, g_final)


def _window_bias():
    tq = NSA_TQ
    i = np.arange(WINDOW + tq)[:, None]
    q = np.arange(tq)[None, :]
    out = []
    for p in range(WINDOW // tq + 1):
        qs = p * tq
        w0 = max(qs - WINDOW, 0)
        valid = (w0 + i <= qs + q) & (w0 + i > qs + q - WINDOW)
        out.append(np.where(valid, 0.0, NEG_INF))
    return jnp.asarray(np.stack(out), F32)


def _cmp_to_slc_weights_t(seq):
    n_cmp = seq // CMP_STRIDE - 1
    n_slc = seq // SLC_BLOCK
    c0 = np.arange(n_cmp)[:, None] * CMP_STRIDE
    s0 = np.arange(n_slc)[None, :] * SLC_BLOCK
    ov = np.clip(np.minimum(c0 + CMP_LEN, s0 + SLC_BLOCK) - np.maximum(c0, s0), 0, None)
    m_cs = np.zeros((n_cmp + 1, n_slc), np.float32)
    m_cs[:n_cmp] = ov / CMP_LEN
    return jnp.asarray(m_cs.T, BF16)


def kernel(x, g_mix, w_in, w_gla_gate, b_gla_gate, g_gla_out, pe_cmp_k, w_cmp_k1, w_cmp_k2,
           pe_cmp_v, w_cmp_v1, w_cmp_v2, w_proj_gla, w_proj_nsa, w_out, g_ffn,
           w_ffn_gate, w_ffn_up, w_ffn_down, g_final):
    bsz, seq, d = x.shape
    m = bsz * seq
    depth = w_in.shape[0]
    mcs_t = _cmp_to_slc_weights_t(seq)
    win_bias = _window_bias()
    xc = x.reshape(m, d)
    for l in range(depth):
        w_t = jnp.swapaxes(w_in[l], 0, 1)
        h, zs = _norm_small(xc, g_mix[l][None], w_t)
        z2 = _in_proj(h, w_t)
        z3 = z2.reshape(bsz, seq, Z_SMALL)
        zs3 = zs.reshape(bsz, seq, LANES)

        w_gate_pad = jnp.zeros((LANES, GLA_DK), F32).at[SMALL_GLR:SMALL_GLR + GLA_GATE_RANK].set(w_gla_gate[l])
        o_gla = _gla(z3, zs3, w_gate_pad.astype(BF16), b_gla_gate[l][None], g_gla_out[l][None])

        cmp = _compress(z3, pe_cmp_k[l], pe_cmp_v[l], w_cmp_k1[l], w_cmp_v1[l], w_cmp_k2[l], w_cmp_v2[l])
        o_nsa = _nsa(z3, zs3, cmp, mcs_t, win_bias)

        merged, w_out_bf = _merge(o_gla.reshape(m, GLA_DV), o_nsa.reshape(m, NSA_HEADS * NSA_HD),
                                  w_proj_gla[l], w_proj_nsa[l], z2, w_out[l])
        x1, h2 = _out_proj(merged, w_out_bf, xc, g_ffn[l][None])
        act, w_down_bf = _ffn_up(h2, w_ffn_gate[l], w_ffn_up[l], w_ffn_down[l])
        xc = _ffn_down(act, w_down_bf, x1, g_final[None],
                       final_norm=(l == depth - 1))
    return xc.reshape(bsz, seq, d)
```
